```python
import math, functools
import jax, jax.numpy as jnp
from jax import lax
import numpy as np

D_MODEL = 1024
BATCH = 32
SEQ = 256
DEPTH = 2
DEC_BATCH = 8
DEC_SEQ = 2048
PAST_LEN = 256

GRID_W = 64
CHUNK = 128
Q_BLOCK = 128
EPS = 1e-6
ALPHA = (2 * DEPTH) ** 0.25
BETA = (8 * DEPTH) ** -0.25
W_A = D_MODEL // 2
DK_A = 64
DV_A = 64
H_A = W_A // DV_A
HD_B = 64
H_B = (D_MODEL // 2) // HD_B
KV_B = H_B // 4
ROPE_THETA = 10000.0
D_INNER = 2 * D_MODEL
HD_C = 64
H_C = D_INNER // HD_C
NG_C = 4
HPG_C = H_C // NG_C
D_STATE = 128
CONV_W = 3
CONV_CH = D_INNER + 2 * NG_C * D_STATE
D_FF = 2816
N_EXPERTS = 8
TOP_K = 2
D_FF_E = 1408
MIX_SIZES = (W_A, W_A, W_A, W_A, 4 * H_A, H_B * HD_B, KV_B * HD_B, KV_B * HD_B)
SSD_SIZES = (D_INNER, CONV_CH, 2 * H_C)

kernel_name = 'hybrid_mlstm_gqa_ssd_moe_diffusion_step'


def split_cols(x, sizes):
    idx = np.cumsum(sizes)[:-1].tolist()
    return jnp.split(x, idx, axis=-1)


def flip(a):
    return jnp.flip(a, axis=1)


def to_chunks(a):
    b, s = a.shape[:2]
    return a.reshape(b, s // CHUNK, CHUNK, *a.shape[2:]).swapaxes(0, 1)


def from_chunks(a):
    nc, b, l = a.shape[:3]
    return a.swapaxes(0, 1).reshape(b, nc * l, *a.shape[3:])


def layer_norm(x, g, b):
    xf = x.astype(jnp.float32)
    mu = xf.mean(-1, keepdims=True)
    var = jnp.mean(jnp.square(xf - mu), -1, keepdims=True)
    return ((xf - mu) * lax.rsqrt(var + EPS) * g.astype(jnp.float32) + b.astype(jnp.float32)).astype(x.dtype)


def rms_norm(x, g):
    xf = x.astype(jnp.float32)
    return (xf * lax.rsqrt(jnp.mean(jnp.square(xf), -1, keepdims=True) + EPS) * g.astype(jnp.float32)).astype(x.dtype)


def ada_mods(cvec, w, b):
    m = jax.nn.silu(cvec) @ w + b
    return [t[:, None, :] for t in jnp.split(m, 6, axis=-1)]


def axial_rope(x, row, col):
    half = x.shape[-1] // 2
    nf = half // 2
    inv = ROPE_THETA ** (-jnp.arange(nf, dtype=jnp.float32) / nf)

    def rot(xh, p):
        ang = p.astype(jnp.float32)[:, None] * inv
        cos, sin = jnp.cos(ang)[None, :, None, :], jnp.sin(ang)[None, :, None, :]
        x1, x2 = xh[..., :nf], xh[..., nf:]
        return jnp.concatenate([x1 * cos - x2 * sin, x2 * cos + x1 * sin], axis=-1)

    xf = x.astype(jnp.float32)
    return jnp.concatenate([rot(xf[..., :half], row), rot(xf[..., half:], col)], axis=-1).astype(x.dtype)


def block_attention(q, k, v):
    bsz, s_len = q.shape[:2]
    qb = q.reshape(bsz, s_len // Q_BLOCK, Q_BLOCK, *q.shape[2:]).swapaxes(0, 1)
    scale = HD_B ** -0.5

    def one(qblk):
        s = jnp.einsum('bqhgd,bkhd->bhgqk', qblk, k).astype(jnp.float32) * scale
        p = jax.nn.softmax(s, axis=-1).astype(v.dtype)
        return jnp.einsum('bhgqk,bkhd->bqhgd', p, v)

    o = lax.map(one, qb)
    return o.swapaxes(0, 1).reshape(bsz, s_len, -1)


def mlstm_scan(q, k, v, li, lf, c0, n0, m0):
    tril = jnp.tril(jnp.ones((CHUNK, CHUNK), bool))

    def step(carry, inp):
        c, n, m = carry
        qc, kc, vc, lic, lfc = inp
        b = jnp.cumsum(lfc, axis=1)
        a = b + m[:, None]
        dmat = jnp.where(tril[None, :, :, None], b[:, :, None] - b[:, None] + lic[:, None], -jnp.inf)
        mt = jnp.maximum(a, dmat.max(axis=2))
        w_in = jnp.exp(dmat - mt[:, :, None])
        w_st = jnp.exp(a - mt)
        sc = jnp.einsum('bthd,bshd->btsh', qc, kc) * w_in
        num = jnp.einsum('btsh,bshv->bthv', sc, vc) + w_st[..., None] * jnp.einsum('bhdv,bthd->bthv', c, qc)
        den = sc.sum(2) + w_st * jnp.einsum('bhd,bthd->bth', n, qc)
        h = num / jnp.maximum(jnp.abs(den), jnp.exp(-mt))[..., None]
        bl = b[:, -1]
        ls = bl[:, None] - b + lic
        m_new = jnp.maximum(bl + m, ls.max(1))
        ws = jnp.exp(ls - m_new[:, None])
        decay = jnp.exp(bl + m - m_new)
        c_new = decay[..., None, None] * c + jnp.einsum('bsh,bshd,bshv->bhdv', ws, kc, vc)
        n_new = decay[..., None] * n + jnp.einsum('bsh,bshd->bhd', ws, kc)
        return (c_new, n_new, m_new), h

    (c, n, m), h = lax.scan(step, (c0, n0, m0), (to_chunks(q), to_chunks(k), to_chunks(v), to_chunks(li), to_chunks(lf)))
    return from_chunks(h), c, n, m


def ssd_scan(x, dt, a_coef, bm, cm, s0):
    tril = jnp.tril(jnp.ones((CHUNK, CHUNK), bool))

    def step(s, inp):
        xc, dtc, bc, cc = inp
        cum = jnp.cumsum(dtc * a_coef, axis=1)
        seg = jnp.where(tril[None, :, :, None, None], cum[:, :, None] - cum[:, None], -jnp.inf)
        cb = jnp.einsum('btgn,bsgn->btsg', cc, bc)
        w = cb[..., None] * jnp.exp(seg) * dtc[:, None]
        y = jnp.einsum('btsgr,bsgrp->btgrp', w, xc) + jnp.einsum('bgrpn,btgn->btgrp', s, cc) * jnp.exp(cum)[..., None]
        ws = jnp.exp(cum[:, -1:] - cum) * dtc
        s_new = jnp.exp(cum[:, -1])[..., None, None] * s + jnp.einsum('bsgr,bsgrp,bsgn->bgrpn', ws, xc, bc)
        return s_new, y

    s, y = lax.scan(step, s0, (to_chunks(x), to_chunks(dt), to_chunks(bm), to_chunks(cm)))
    return from_chunks(y), s


def depthwise_conv_centred(x, w):
    return lax.conv_general_dilated(x, w[:, None, :], window_strides=(1,), padding=[(CONV_W // 2, CONV_W // 2)],
                                    dimension_numbers=('NWC', 'WIO', 'NWC'), feature_group_count=x.shape[-1])


def mlstm_attn_mixer(u, w_in, b_gates, norm_g, q_g, k_g, w_out, pos, ctx):
    f32 = jnp.float32
    bsz, s_len, _ = u.shape
    qm, km, vm, om, gates, qa, ka, va = split_cols(u @ w_in, MIX_SIZES)
    qm = qm.reshape(bsz, s_len, H_A, DK_A).astype(f32) * DK_A ** -0.5
    km = km.reshape(bsz, s_len, H_A, DK_A).astype(f32)
    vm = vm.reshape(bsz, s_len, H_A, DV_A).astype(f32)
    gates = gates.reshape(bsz, s_len, 4, H_A).astype(f32) + b_gates.astype(f32)
    li_f, lf_f = gates[:, :, 0], jax.nn.log_sigmoid(gates[:, :, 1])
    li_b, lf_b = gates[:, :, 2], jax.nn.log_sigmoid(gates[:, :, 3])
    if ctx is None:
        c0 = jnp.zeros((bsz, 2, H_A, DK_A, DV_A), f32)
        n0 = jnp.zeros((bsz, 2, H_A, DK_A), f32)
        m0 = jnp.zeros((bsz, 2, H_A), f32)
    else:
        c0, n0, m0 = (t.astype(f32) for t in ctx[2:])
    h_f, c_f, n_f, m_f = mlstm_scan(qm, km, vm, li_f, lf_f, c0[:, 0], n0[:, 0], m0[:, 0])
    h_b, c_b, n_b, m_b = mlstm_scan(flip(qm), flip(km), flip(vm), flip(li_b), flip(lf_b), c0[:, 1], n0[:, 1], m0[:, 1])
    h = h_f + flip(h_b)
    mu = h.mean(-1, keepdims=True)
    var = jnp.mean(jnp.square(h - mu), -1, keepdims=True)
    h = ((h - mu) * lax.rsqrt(var + EPS)).reshape(bsz, s_len, W_A)
    h = h * norm_g.astype(f32) * jax.nn.sigmoid(om.astype(f32))
    qa = rms_norm(qa.reshape(bsz, s_len, H_B, HD_B), q_g)
    ka = rms_norm(ka.reshape(bsz, s_len, KV_B, HD_B), k_g)
    va = va.reshape(bsz, s_len, KV_B, HD_B)
    if ctx is None:
        k_all, v_all = ka, va
    else:
        row, col = pos
        qa = axial_rope(qa, row, col)
        k_all = jnp.concatenate([ctx[0], axial_rope(ka, row, col)], axis=1)
        v_all = jnp.concatenate([ctx[1], va], axis=1)
    o = block_attention(qa.reshape(bsz, s_len, KV_B, H_B // KV_B, HD_B), k_all, v_all)
    y = jnp.concatenate([h.astype(u.dtype), o], axis=-1) @ w_out
    if ctx is None:
        dt = u.dtype
        state = (ka, va, jnp.stack([c_f, c_b], 1).astype(dt), jnp.stack([n_f, n_b], 1).astype(dt),
                 jnp.stack([m_f, m_b], 1).astype(dt))
    else:
        state = None
    return y, state


def ssd_mixer(u, w_in, conv_w, conv_b, dt_bias, a_log, d_skip, norm_g, w_out, s_init):
    f32 = jnp.float32
    bsz, s_len, _ = u.shape
    z, xbc, dt = split_cols(u @ w_in, SSD_SIZES)
    xbc = jax.nn.silu(depthwise_conv_centred(xbc, conv_w) + conv_b)
    xs, bm, cm = split_cols(xbc.astype(f32), (D_INNER, NG_C * D_STATE, NG_C * D_STATE))
    xs = xs.reshape(bsz, s_len, NG_C, HPG_C, HD_C)
    bm = bm.reshape(bsz, s_len, NG_C, D_STATE)
    cm = cm.reshape(bsz, s_len, NG_C, D_STATE)
    dt = jax.nn.softplus(dt.astype(f32).reshape(bsz, s_len, 2, H_C) + dt_bias.astype(f32))
    dt = dt.reshape(bsz, s_len, 2, NG_C, HPG_C)
    a_coef = -jnp.exp(a_log.astype(f32)).reshape(2, NG_C, HPG_C)
    if s_init is None:
        s0 = jnp.zeros((bsz, 2, NG_C, HPG_C, HD_C, D_STATE), f32)
    else:
        s0 = s_init.astype(f32).reshape(bsz, 2, NG_C, HPG_C, HD_C, D_STATE)
    y_f, s_f = ssd_scan(xs, dt[:, :, 0], a_coef[0], bm, cm, s0[:, 0])
    y_b, s_b = ssd_scan(flip(xs), flip(dt[:, :, 1]), a_coef[1], flip(bm), flip(cm), s0[:, 1])
    y = y_f + flip(y_b) + d_skip.astype(f32).reshape(NG_C, HPG_C)[..., None] * xs
    y = y.reshape(bsz, s_len, D_INNER) * jax.nn.silu(z.astype(f32))
    yg = y.reshape(bsz, s_len, NG_C, D_INNER // NG_C)
    yg = yg * lax.rsqrt(jnp.mean(jnp.square(yg), -1, keepdims=True) + EPS)
    y = yg.reshape(bsz, s_len, D_INNER) * norm_g.astype(f32)
    out = y.astype(u.dtype) @ w_out
    if s_init is None:
        state = jnp.stack([s_f, s_b], 1).reshape(bsz, 2, H_C, HD_C, D_STATE).astype(u.dtype)
    else:
        state = None
    return out, state


def swiglu(u, w1, w3, w2):
    return (jax.nn.silu(u @ w1) * (u @ w3)) @ w2


def moe_swiglu(u, router_w, router_b, w1, w3, w2):
    bsz, s_len, d = u.shape
    t = u.reshape(-1, d)
    logits = (t @ router_w).astype(jnp.float32) + router_b.astype(jnp.float32)
    top_v, top_i = lax.top_k(logits, TOP_K)
    probs = jax.nn.softmax(top_v, axis=-1)
    gate = jnp.einsum('tk,tke->te', probs, jax.nn.one_hot(top_i, N_EXPERTS, dtype=jnp.float32)).astype(u.dtype)
    out = jnp.zeros_like(t)
    for e in range(N_EXPERTS):
        out = out + gate[:, e:e + 1] * swiglu(t, w1[e], w3[e], w2[e])
    return out.reshape(bsz, s_len, d)


def post_norm_layer(x, mods, mixer, channel, ln):
    sh1, sc1, g1, sh2, sc2, g2 = mods
    y, state = mixer(x * (1 + sc1) + sh1)
    x = layer_norm(ALPHA * x + g1 * y, ln[0], ln[1])
    x = layer_norm(ALPHA * x + g2 * channel(x * (1 + sc2) + sh2), ln[2], ln[3])
    return x, state


def setup_inputs(seed: int = 0) -> dict:
    key = jax.random.key(seed)
    ks = iter(jax.random.split(key, 64))

    def nrm(shape, scale):
        return jax.random.normal(next(ks), shape, jnp.float32) * scale

    def ones_n(n):
        return 1.0 + nrm((n,), 0.02)

    def small(n):
        return nrm((n,), 0.02)

    d = D_MODEL
    w_mix_in = sum(MIX_SIZES)
    w_mix_out = W_A + H_B * HD_B
    w_ssd_in = sum(SSD_SIZES)
    fgate = jnp.linspace(3.0, 6.0, H_A)
    gate_b = jnp.stack([nrm((H_A,), 0.1), fgate + nrm((H_A,), 0.1), nrm((H_A,), 0.1), fgate + nrm((H_A,), 0.1)])
    dt0 = jnp.exp(jax.random.uniform(next(ks), (2, H_C), jnp.float32, math.log(1e-3), math.log(1e-1)))
    dt_bias = dt0 + jnp.log(-jnp.expm1(-dt0))
    a_log = jnp.log(jax.random.uniform(next(ks), (2, H_C), jnp.float32, 1.0, 16.0))
    return {
        'x_prompt': nrm((BATCH, SEQ, d), 1.0),
        'x_sample': nrm((DEC_BATCH, DEC_SEQ, d), 1.0),
        'cache_k_l0': nrm((DEC_BATCH, PAST_LEN, KV_B, HD_B), 1.0),
        'cache_v_l0': nrm((DEC_BATCH, PAST_LEN, KV_B, HD_B), 1.0),
        'state_mlstm_c_l0': nrm((DEC_BATCH, 2, H_A, DK_A, DV_A), 0.5),
        'state_mlstm_n_l0': nrm((DEC_BATCH, 2, H_A, DK_A), 1.0),
        'state_mlstm_m_l0': nrm((DEC_BATCH, 2, H_A), 1.0),
        'state_ssd_l1': nrm((DEC_BATCH, 2, H_C, HD_C, D_STATE), 0.1),
        'c': nrm((DEC_BATCH, d), 1.0),
        'c_ctx': nrm((d,), 1.0),
        'ada_w_l0': nrm((d, 6 * d), d ** -0.5),
        'ada_b_l0': small(6 * d),
        'mix_w_in_l0': nrm((d, w_mix_in), d ** -0.5),
        'mlstm_gate_b_l0': gate_b,
        'mlstm_norm_g_l0': ones_n(W_A),
        'q_norm_g_l0': ones_n(HD_B),
        'k_norm_g_l0': ones_n(HD_B),
        'mix_w_out_l0': nrm((w_mix_out, d), BETA * w_mix_out ** -0.5),
        'ln1_g_l0': ones_n(d),
        'ln1_b_l0': small(d),
        'ln2_g_l0': ones_n(d),
        'ln2_b_l0': small(d),
        'ffn_w1_l0': nrm((d, D_FF), d ** -0.5),
        'ffn_w3_l0': nrm((d, D_FF), d ** -0.5),
        'ffn_w2_l0': nrm((D_FF, d), BETA * D_FF ** -0.5),
        'ada_w_l1': nrm((d, 6 * d), d ** -0.5),
        'ada_b_l1': small(6 * d),
        'ssd_w_in_l1': nrm((d, w_ssd_in), d ** -0.5),
        'ssd_conv_w_l1': nrm((CONV_W, CONV_CH), CONV_W ** -0.5),
        'ssd_conv_b_l1': small(CONV_CH),
        'ssd_dt_bias_l1': dt_bias,
        'ssd_a_log_l1': a_log,
        'ssd_d_l1': ones_n(H_C),
        'ssd_norm_g_l1': ones_n(D_INNER),
        'ssd_w_out_l1': nrm((D_INNER, d), BETA * D_INNER ** -0.5),
        'ln1_g_l1': ones_n(d),
        'ln1_b_l1': small(d),
        'ln2_g_l1': ones_n(d),
        'ln2_b_l1': small(d),
        'router_w_l1': nrm((d, N_EXPERTS), d ** -0.5),
        'router_b_l1': nrm((N_EXPERTS,), 0.01),
        'moe_w1_l1': nrm((N_EXPERTS, d, D_FF_E), d ** -0.5),
        'moe_w3_l1': nrm((N_EXPERTS, d, D_FF_E), d ** -0.5),
        'moe_w2_l1': nrm((N_EXPERTS, D_FF_E, d), BETA * D_FF_E ** -0.5),
    }


def reference(x_prompt, x_sample, cache_k_l0, cache_v_l0, state_mlstm_c_l0, state_mlstm_n_l0, state_mlstm_m_l0,
              state_ssd_l1, c, c_ctx, ada_w_l0, ada_b_l0, mix_w_in_l0, mlstm_gate_b_l0, mlstm_norm_g_l0,
              q_norm_g_l0, k_norm_g_l0, mix_w_out_l0, ln1_g_l0, ln1_b_l0, ln2_g_l0, ln2_b_l0, ffn_w1_l0,
              ffn_w3_l0, ffn_w2_l0, ada_w_l1, ada_b_l1, ssd_w_in_l1, ssd_conv_w_l1, ssd_conv_b_l1,
              ssd_dt_bias_l1, ssd_a_log_l1, ssd_d_l1, ssd_norm_g_l1, ssd_w_out_l1, ln1_g_l1, ln1_b_l1,
              ln2_g_l1, ln2_b_l1, router_w_l1, router_b_l1, moe_w1_l1, moe_w3_l1, moe_w2_l1):
    even_params = [(ada_w_l0, ada_b_l0, mix_w_in_l0, mlstm_gate_b_l0, mlstm_norm_g_l0, q_norm_g_l0, k_norm_g_l0,
                    mix_w_out_l0, (ln1_g_l0, ln1_b_l0, ln2_g_l0, ln2_b_l0), (ffn_w1_l0, ffn_w3_l0, ffn_w2_l0))]
    odd_params = [(ada_w_l1, ada_b_l1, ssd_w_in_l1, ssd_conv_w_l1, ssd_conv_b_l1, ssd_dt_bias_l1, ssd_a_log_l1,
                   ssd_d_l1, ssd_norm_g_l1, ssd_w_out_l1, (ln1_g_l1, ln1_b_l1, ln2_g_l1, ln2_b_l1),
                   (router_w_l1, router_b_l1, moe_w1_l1, moe_w3_l1, moe_w2_l1))]
    caches = [(cache_k_l0, cache_v_l0, state_mlstm_c_l0, state_mlstm_n_l0, state_mlstm_m_l0), state_ssd_l1]

    rows = x_sample.shape[1] // GRID_W
    row = jnp.repeat(jnp.arange(rows), GRID_W)
    col = jnp.tile(jnp.arange(GRID_W), rows)
    ctx_vec = c_ctx[None, :]

    y_p, y_s = x_prompt, x_sample
    new_states = []
    for layer in range(DEPTH):
        if layer % 2 == 0:
            aw, ab, w_in, gb, mhg, qg, kg, w_out, ln, ffn = even_params[layer // 2]
            common = dict(w_in=w_in, b_gates=gb, norm_g=mhg, q_g=qg, k_g=kg, w_out=w_out)
            mixer_p = functools.partial(mlstm_attn_mixer, pos=None, ctx=None, **common)
            mixer_s = functools.partial(mlstm_attn_mixer, pos=(row, col), ctx=caches[layer], **common)
            channel = functools.partial(swiglu, w1=ffn[0], w3=ffn[1], w2=ffn[2])
        else:
            aw, ab, w_in, cw, cb, dtb, alog, dsk, ng, w_out, ln, moe = odd_params[layer // 2]
            common = dict(w_in=w_in, conv_w=cw, conv_b=cb, dt_bias=dtb, a_log=alog, d_skip=dsk, norm_g=ng,
                          w_out=w_out)
            mixer_p = functools.partial(ssd_mixer, s_init=None, **common)
            mixer_s = functools.partial(ssd_mixer, s_init=caches[layer], **common)
            channel = functools.partial(moe_swiglu, router_w=moe[0], router_b=moe[1], w1=moe[2], w3=moe[3],
                                        w2=moe[4])
        y_p, state = post_norm_layer(y_p, ada_mods(ctx_vec, aw, ab), mixer_p, channel, ln)
        new_states.append(state)
        y_s, _ = post_norm_layer(y_s, ada_mods(c, aw, ab), mixer_s, channel, ln)

    new_k_l0, new_v_l0, new_mlstm_c_l0, new_mlstm_n_l0, new_mlstm_m_l0 = new_states[0]
    new_ssd_l1 = new_states[1]
    return (y_p, y_s, new_k_l0, new_v_l0, new_mlstm_c_l0, new_mlstm_n_l0, new_mlstm_m_l0, new_ssd_l1)
```

```python
import functools

import jax
import jax.numpy as jnp
import numpy as np
from jax import lax
from jax.experimental import pallas as pl
from jax.experimental.pallas import tpu as pltpu

F32 = jnp.float32
BF16 = jnp.bfloat16

D = 1024
CH = 128
LANES = 128
EPS = 1e-6
DEPTH = 2
ALPHA = (2 * DEPTH) ** 0.25
GRID_W = 64
ROPE_THETA = 10000.0
H_A = 8
HD = 64
W_A = 512
KV_B = 2
H_C = 32
NG_C = 4
HPG_C = 8
D_STATE = 128
D_INNER = 2048
D_FF = 2816
N_EXPERTS = 8
D_FF_E = 1408
NEG = -1e30
VMEM_LIMIT = 56 * 1024 * 1024


def _cp(*sem):
    return pltpu.CompilerParams(dimension_semantics=sem, vmem_limit_bytes=VMEM_LIMIT)


def _sigmoid(x):
    return 1.0 / (1.0 + jnp.exp(-x))


def _silu(x):
    return x * _sigmoid(x)


def _softplus(x):
    return jnp.maximum(x, 0.0) + jnp.log(1.0 + jnp.exp(-jnp.abs(x)))


def _split2(x):
    hi = x.astype(BF16)
    lo = (x - hi.astype(F32)).astype(BF16)
    return hi, lo


def _split3(x):
    h1 = x.astype(BF16)
    r = x - h1.astype(F32)
    h2 = r.astype(BF16)
    h3 = (r - h2.astype(F32)).astype(BF16)
    return h1, h2, h3


def _dot(a, b):
    return jnp.dot(a, b, preferred_element_type=F32)


def _dot_nt(a, b):
    return lax.dot_general(a, b, (((1,), (1,)), ((), ())), preferred_element_type=F32)


def _tri_dot(tri, x):
    h1, h2, h3 = _split3(x)
    return _dot(tri, h1) + _dot(tri, h2) + _dot(tri, h3)


def _layer_norm(r, g, b):
    mu = jnp.mean(r, axis=-1, keepdims=True)
    d = r - mu
    var = jnp.mean(d * d, axis=-1, keepdims=True)
    return d * lax.rsqrt(var + EPS) * g + b


def _ada_kernel(c_ref, w_ref, b_ref, o_ref):
    s = _silu(c_ref[...])
    sh, sl = _split2(s)
    wh, wl = _split2(w_ref[...])
    o_ref[...] = _dot(sh, wh) + _dot(sl, wh) + _dot(sh, wl) + b_ref[...]


def _ada(cvec, w, b):
    n = w.shape[1]
    tn = 768
    return pl.pallas_call(
        _ada_kernel,
        grid=(n // tn,),
        in_specs=[pl.BlockSpec((16, D), lambda j: (0, 0)),
                  pl.BlockSpec((D, tn), lambda j: (0, j)),
                  pl.BlockSpec((1, tn), lambda j: (0, j))],
        out_specs=pl.BlockSpec((16, tn), lambda j: (0, j)),
        out_shape=jax.ShapeDtypeStruct((16, n), F32),
        compiler_params=_cp("parallel"),
        name="ada_mods",
    )(cvec, w, b.reshape(1, n))


def _mm_in_kernel(x_ref, sh_ref, sc_ref, wm_ref, wah_ref, wal_ref, om_ref, oa_ref, u_scr):
    @pl.when(pl.program_id(1) == 0)
    def _():
        u = x_ref[...] * (1.0 + sc_ref[...]) + sh_ref[...]
        uh, ul = _split2(u)
        u_scr[...] = uh
        wah = wah_ref[...]
        oa_ref[...] = _dot(uh, wah) + _dot(ul, wah) + _dot(uh, wal_ref[...])

    om_ref[...] = _dot(u_scr[...], wm_ref[...]).astype(om_ref.dtype)


def _mm_in(x, mods3, row_fn, wm, wa_hi, wa_lo, tm, tn):
    t, nm = x.shape[0], wm.shape[1]
    return pl.pallas_call(
        _mm_in_kernel,
        grid=(t // tm, nm // tn),
        in_specs=[pl.BlockSpec((tm, D), lambda i, j: (i, 0)),
                  pl.BlockSpec((None, 1, D), lambda i, j: (row_fn(i, tm), 0, 0)),
                  pl.BlockSpec((None, 1, D), lambda i, j: (row_fn(i, tm), 0, 1)),
                  pl.BlockSpec((D, tn), lambda i, j: (0, j)),
                  pl.BlockSpec((D, LANES), lambda i, j: (0, 0)),
                  pl.BlockSpec((D, LANES), lambda i, j: (0, 0))],
        out_specs=[pl.BlockSpec((tm, tn), lambda i, j: (i, j)),
                   pl.BlockSpec((tm, LANES), lambda i, j: (i, 0))],
        out_shape=[jax.ShapeDtypeStruct((t, nm), BF16), jax.ShapeDtypeStruct((t, LANES), F32)],
        scratch_shapes=[pltpu.VMEM((tm, D), BF16)],
        compiler_params=_cp("parallel", "arbitrary"),
        name="mm_in",
    )(x, mods3, mods3, wm, wa_hi, wa_lo)


def _mlstm_kernel(*refs, nc, has_init, out_state):
    q_ref, k_ref, v_ref, om_ref, g_ref, gb_ref, ng_ref, avg_ref = refs[:8]
    pos = 8
    if has_init:
        c0_ref, m0_ref = refs[pos:pos + 2]
        pos += 2
    hn_ref = refs[pos]
    pos += 1
    if out_state:
        cst_ref, mst_ref = refs[pos:pos + 2]
        pos += 2
    c_scr, m_scr, hb_scr = refs[pos:pos + 3]

    j = pl.program_id(1)
    is_fwd = j >= nc
    ci = jnp.where(is_fwd, j - nc, nc - 1 - j)
    sgn = jnp.where(is_fwd, 1, -1)

    def _init(d):
        if has_init:
            c_scr[...] = c0_ref[d]
            m_scr[...] = m0_ref[d]
        else:
            c_scr[...] = jnp.zeros(c_scr.shape, F32)
            m_scr[...] = jnp.zeros(m_scr.shape, F32)

    pl.when(j == 0)(lambda: _init(1))
    pl.when(j == nc)(lambda: _init(0))

    row = lax.broadcasted_iota(jnp.int32, (CH, CH), 0)
    col = lax.broadcasted_iota(jnp.int32, (CH, CH), 1)
    causal = (col - row) * sgn <= 0
    tri = jnp.where(causal, 1.0, 0.0).astype(BF16)
    fwd_row = (jnp.zeros((1, LANES), jnp.int32) + sgn) > 0
    fwd_mat = (jnp.zeros((CH, LANES), jnp.int32) + sgn) > 0

    g = g_ref[...] + gb_ref[...]
    gd = jnp.where(fwd_mat, g, pltpu.roll(g, LANES - 16, axis=1))
    lf = jnp.minimum(gd, 0.0) - jnp.log(1.0 + jnp.exp(-jnp.abs(gd)))
    bc = pltpu.roll(_tri_dot(tri, lf), LANES - 8, axis=1)
    r = gd - bc
    rt = r.T
    bl_all = jnp.where(fwd_row, bc[CH - 1:CH, :], bc[0:1, :])

    q = q_ref[...]
    v = v_ref[...].astype(F32)
    kt = k_ref[...].astype(F32).T
    ones = jnp.ones((CH, HD), F32)

    hs = []
    for h in range(H_A):
        sl = slice(h * HD, (h + 1) * HD)
        bcol = bc[:, h:h + 1]
        rrow = rt[h:h + 1, :]
        m_prev = m_scr[h:h + 1, 0:1]
        dm = jnp.where(causal, bcol + rrow, NEG)
        a = bcol + m_prev
        mt = jnp.maximum(a, jnp.max(dm, axis=1, keepdims=True))
        w = jnp.exp(dm - mt)
        wst = jnp.exp(a - mt)
        qh = q[:, sl]
        kht = kt[sl, :]
        vext = jnp.concatenate([v[:, sl], ones], axis=1).astype(BF16)
        s = _dot(qh, kht.astype(BF16)) * 0.125
        sc = (s * w).astype(BF16)
        cext = c_scr[h]
        nd = _dot(sc, vext) + wst * (_dot(qh, cext.astype(BF16)) * 0.125)
        den = jnp.maximum(jnp.abs(nd[:, HD:HD + 1]), jnp.exp(-mt))
        hs.append(nd[:, :HD] / den)
        bl = bl_all[:, h:h + 1]
        lsrow = bl + rrow
        m_new = jnp.maximum(bl + m_prev, jnp.max(lsrow, axis=1, keepdims=True))
        wsrow = jnp.exp(lsrow - m_new)
        decay = jnp.exp(bl + m_prev - m_new)
        c_scr[h] = decay * cext + _dot((kht * wsrow).astype(BF16), vext)
        m_scr[h:h + 1, :] = jnp.broadcast_to(m_new, (1, LANES))
    hcat = jnp.concatenate(hs, axis=1)
    rows = pl.ds(pl.multiple_of(ci * CH, CH), CH)

    @pl.when(j < nc)
    def _():
        hb_scr[rows, :] = hcat

    @pl.when(j >= nc)
    def _():
        ht = hcat + hb_scr[rows, :]
        avg = avg_ref[...]
        h1, h2 = _split2(ht)
        mu = _dot(h1, avg) + _dot(h2, avg)
        dlt = ht - mu
        d1, d2 = _split2(dlt * dlt)
        var = _dot(d1, avg) + _dot(d2, avg)
        hn = dlt * lax.rsqrt(var + EPS) * ng_ref[...] * _sigmoid(om_ref[...].astype(F32))
        hn_ref[...] = hn.astype(hn_ref.dtype)

    if out_state:
        @pl.when(j == nc - 1)
        def _():
            cst_ref[1] = c_scr[...]
            mst_ref[1] = m_scr[...]

        @pl.when(j == 2 * nc - 1)
        def _():
            cst_ref[0] = c_scr[...]
            mst_ref[0] = m_scr[...]


def _mlstm(pm, pa, gbias, norm_g, avg512, bsz, s_len, init=None, out_state=False):
    nc = s_len // CH
    t = bsz * s_len

    def chunk(b, j):
        return b * nc + jnp.where(j >= nc, j - nc, nc - 1 - j)

    def chunk_fwd(b, j):
        return b * nc + jnp.maximum(j - nc, 0)

    in_specs = [pl.BlockSpec((CH, W_A), lambda b, j: (chunk(b, j), 0)),
                pl.BlockSpec((CH, W_A), lambda b, j: (chunk(b, j), 1)),
                pl.BlockSpec((CH, W_A), lambda b, j: (chunk(b, j), 2)),
                pl.BlockSpec((CH, W_A), lambda b, j: (chunk_fwd(b, j), 3)),
                pl.BlockSpec((CH, LANES), lambda b, j: (chunk(b, j), 0)),
                pl.BlockSpec((1, LANES), lambda b, j: (0, 0)),
                pl.BlockSpec((1, W_A), lambda b, j: (0, 0)),
                pl.BlockSpec((W_A, W_A), lambda b, j: (0, 0))]
    args = [pm, pm, pm, pm, pa, gbias, norm_g, avg512]
    if init is not None:
        in_specs += [pl.BlockSpec((None, 2, H_A, HD, LANES), lambda b, j: (b, 0, 0, 0, 0)),
                     pl.BlockSpec((None, 2, H_A, LANES), lambda b, j: (b, 0, 0, 0))]
        args += list(init)
    out_specs = [pl.BlockSpec((CH, W_A), lambda b, j: (chunk_fwd(b, j), 0))]
    out_shape = [jax.ShapeDtypeStruct((t, W_A), BF16)]
    if out_state:
        out_specs += [pl.BlockSpec((None, 2, H_A, HD, LANES), lambda b, j: (b, 0, 0, 0, 0)),
                      pl.BlockSpec((None, 2, H_A, LANES), lambda b, j: (b, 0, 0, 0))]
        out_shape += [jax.ShapeDtypeStruct((bsz, 2, H_A, HD, LANES), F32),
                      jax.ShapeDtypeStruct((bsz, 2, H_A, LANES), F32)]
    return pl.pallas_call(
        functools.partial(_mlstm_kernel, nc=nc, has_init=init is not None, out_state=out_state),
        grid=(bsz, 2 * nc),
        in_specs=in_specs,
        out_specs=out_specs,
        out_shape=out_shape,
        scratch_shapes=[pltpu.VMEM((H_A, HD, LANES), F32), pltpu.VMEM((H_A, LANES), F32),
                        pltpu.VMEM((s_len, W_A), F32)],
        compiler_params=_cp("parallel", "arbitrary"),
        name="mlstm",
    )(*args)


def _group_rms(x, avg, g):
    s1, s2 = _split2(x * x)
    ms = _dot(s1, avg) + _dot(s2, avg)
    return x * lax.rsqrt(ms + EPS) * g


def _rope(x, cos, sin):
    lane = lax.broadcasted_iota(jnp.int32, x.shape, 1)
    swapped = jnp.where((lane % 32) < 16, pltpu.roll(x, LANES - 16, axis=1), pltpu.roll(x, 16, axis=1))
    return x * cos + swapped * sin


def _qkprep_kernel(*refs, rope, emit_kn):
    q_ref, k_ref, v_ref, qg_ref, kg_ref, avg_ref = refs[:6]
    pos = 6
    if rope:
        cos_ref, sin_ref = refs[pos:pos + 2]
        pos += 2
    qn_ref, kh_ref, vh_ref = refs[pos:pos + 3]
    pos += 3
    avg = avg_ref[...]
    qs = []
    for c in range(W_A // LANES):
        sl = slice(c * LANES, (c + 1) * LANES)
        xn = _group_rms(q_ref[:, sl].astype(F32), avg, qg_ref[:, sl])
        if rope:
            xn = _rope(xn, cos_ref[...], sin_ref[...])
        qs.append(xn * 0.125)
    qn_ref[...] = jnp.concatenate(qs, axis=1).astype(qn_ref.dtype)
    kn = _group_rms(k_ref[...].astype(F32), avg, kg_ref[...])
    if emit_kn:
        refs[pos][...] = kn
    if rope:
        kn = _rope(kn, cos_ref[...], sin_ref[...])
    v = v_ref[...].astype(F32)
    ones = jnp.ones((v.shape[0], HD), F32)
    for gi in range(KV_B):
        sl = slice(gi * HD, (gi + 1) * HD)
        kh_ref[gi] = kn[:, sl].astype(kh_ref.dtype)
        vh_ref[gi] = jnp.concatenate([v[:, sl], ones], axis=1).astype(vh_ref.dtype)


def _qkprep(pm, q_g, k_g, avg128, s_len, tables=None, emit_kn=False):
    t = pm.shape[0]
    tm = 256
    nt = s_len // tm
    qoff, koff, voff = 2048 // W_A, 2560 // LANES, 2688 // LANES
    in_specs = [pl.BlockSpec((tm, W_A), lambda i: (i, qoff)),
                pl.BlockSpec((tm, LANES), lambda i: (i, koff)),
                pl.BlockSpec((tm, LANES), lambda i: (i, voff)),
                pl.BlockSpec((1, W_A), lambda i: (0, 0)),
                pl.BlockSpec((1, LANES), lambda i: (0, 0)),
                pl.BlockSpec((LANES, LANES), lambda i: (0, 0))]
    args = [pm, pm, pm, q_g, k_g, avg128]
    if tables is not None:
        in_specs += [pl.BlockSpec((tm, LANES), lambda i: (i % nt, 0))] * 2
        args += list(tables)
    out_specs = [pl.BlockSpec((tm, W_A), lambda i: (i, 0)),
                 pl.BlockSpec((KV_B, tm, HD), lambda i: (0, i, 0)),
                 pl.BlockSpec((KV_B, tm, LANES), lambda i: (0, i, 0))]
    out_shape = [jax.ShapeDtypeStruct((t, W_A), BF16),
                 jax.ShapeDtypeStruct((KV_B, t, HD), BF16),
                 jax.ShapeDtypeStruct((KV_B, t, LANES), BF16)]
    if emit_kn:
        out_specs.append(pl.BlockSpec((tm, LANES), lambda i: (i, 0)))
        out_shape.append(jax.ShapeDtypeStruct((t, LANES), F32))
    return pl.pallas_call(
        functools.partial(_qkprep_kernel, rope=tables is not None, emit_kn=emit_kn),
        grid=(t // tm,),
        in_specs=in_specs,
        out_specs=out_specs,
        out_shape=out_shape,
        compiler_params=_cp("parallel"),
        name="qkprep",
    )(*args)


def _attn_kernel(*refs, has_ctx):
    if has_ctx:
        q_ref, k_ref, v_ref, kc_ref, vc_ref, o_ref = refs
    else:
        q_ref, k_ref, v_ref, o_ref = refs
    k = k_ref[...]
    v = v_ref[...]
    outs = []
    for hq in range(4):
        qh = q_ref[:, hq * HD:(hq + 1) * HD]
        s = _dot_nt(qh, k)
        m = jnp.max(s, axis=1, keepdims=True)
        if has_ctx:
            sc = _dot_nt(qh, kc_ref[...])
            m = jnp.maximum(m, jnp.max(sc, axis=1, keepdims=True))
        oe = _dot(jnp.exp(s - m).astype(BF16), v)
        if has_ctx:
            oe = oe + _dot(jnp.exp(sc - m).astype(BF16), vc_ref[...])
        outs.append(oe[:, :HD] / oe[:, HD:HD + 1])
    o_ref[...] = jnp.concatenate(outs, axis=1).astype(o_ref.dtype)


def _attn(qn, kh, vh, bsz, s_len, ctx=None):
    t = qn.shape[0]
    tq = min(256, s_len)
    nq = s_len // tq
    in_specs = [pl.BlockSpec((tq, 256), lambda b, g, i: (b * nq + i, g)),
                pl.BlockSpec((None, s_len, HD), lambda b, g, i: (g, b, 0)),
                pl.BlockSpec((None, s_len, LANES), lambda b, g, i: (g, b, 0))]
    args = [qn, kh, vh]
    if ctx is not None:
        past = ctx[0].shape[1] // bsz
        in_specs += [pl.BlockSpec((None, past, HD), lambda b, g, i: (g, b, 0)),
                     pl.BlockSpec((None, past, LANES), lambda b, g, i: (g, b, 0))]
        args += list(ctx)
    return pl.pallas_call(
        functools.partial(_attn_kernel, has_ctx=ctx is not None),
        grid=(bsz, KV_B, nq),
        in_specs=in_specs,
        out_specs=pl.BlockSpec((tq, 256), lambda b, g, i: (b * nq + i, g)),
        out_shape=jax.ShapeDtypeStruct((t, W_A), BF16),
        compiler_params=_cp("parallel", "parallel", "parallel"),
        name="attn",
    )(*args)


def _out_kernel(*refs, n_in, router):
    a_refs = refs[:n_in]
    w_refs = refs[n_in:2 * n_in]
    x_ref, g1_ref, sh2_ref, sc2_ref, lng_ref, lnb_ref = refs[2 * n_in:2 * n_in + 6]
    pos = 2 * n_in + 6
    if router:
        rwh_ref, rwl_ref, rb_ref = refs[pos:pos + 3]
        pos += 3
    x1_ref, u2_ref = refs[pos:pos + 2]
    y = _dot(a_refs[0][...], w_refs[0][...])
    for a_ref, w_ref in zip(a_refs[1:], w_refs[1:]):
        y = y + _dot(a_ref[...], w_ref[...])
    x1 = _layer_norm(ALPHA * x_ref[...] + g1_ref[...] * y, lng_ref[...], lnb_ref[...])
    x1_ref[...] = x1
    u2 = x1 * (1.0 + sc2_ref[...]) + sh2_ref[...]
    uh, ul = _split2(u2)
    u2_ref[...] = uh
    if router:
        gate_ref = refs[pos + 2]
        rwh = rwh_ref[...]
        logits = _dot(uh, rwh) + _dot(ul, rwh) + _dot(uh, rwl_ref[...]) + rb_ref[...]
        lane = lax.broadcasted_iota(jnp.int32, logits.shape, 1)
        m1 = jnp.max(logits, axis=1, keepdims=True)
        i1 = jnp.min(jnp.where(logits == m1, lane, LANES), axis=1, keepdims=True)
        rest = jnp.where(lane == i1, 2.0 * NEG, logits)
        m2 = jnp.max(rest, axis=1, keepdims=True)
        i2 = jnp.min(jnp.where(rest == m2, lane, LANES), axis=1, keepdims=True)
        e = jnp.exp(m2 - m1)
        p1 = 1.0 / (1.0 + e)
        gate_ref[...] = jnp.where(lane == i1, p1, jnp.where(lane == i2, e * p1, 0.0))


def _out_proj(acts, ws, x, mods3, row_fn, ln_g, ln_b, router=None):
    t = x.shape[0]
    tm = 512
    n_in = len(acts)
    in_specs = [pl.BlockSpec((tm, a.shape[1]), lambda i: (i, 0)) for a in acts]
    in_specs += [pl.BlockSpec(w.shape, lambda i: (0, 0)) for w in ws]
    in_specs += [pl.BlockSpec((tm, D), lambda i: (i, 0))]
    in_specs += [pl.BlockSpec((None, 1, D), functools.partial(lambda i, k: (row_fn(i, tm), 0, k), k=k))
                 for k in (2, 3, 4)]
    in_specs += [pl.BlockSpec((1, D), lambda i: (0, 0))] * 2
    args = list(acts) + list(ws) + [x, mods3, mods3, mods3, ln_g.reshape(1, D), ln_b.reshape(1, D)]
    out_specs = [pl.BlockSpec((tm, D), lambda i: (i, 0)), pl.BlockSpec((tm, D), lambda i: (i, 0))]
    out_shape = [jax.ShapeDtypeStruct((t, D), F32), jax.ShapeDtypeStruct((t, D), BF16)]
    if router is not None:
        in_specs += [pl.BlockSpec((D, LANES), lambda i: (0, 0))] * 2 + [pl.BlockSpec((1, LANES), lambda i: (0, 0))]
        args += list(router)
        out_specs.append(pl.BlockSpec((tm, LANES), lambda i: (i, 0)))
        out_shape.append(jax.ShapeDtypeStruct((t, LANES), F32))
    return pl.pallas_call(
        functools.partial(_out_kernel, n_in=n_in, router=router is not None),
        grid=(t // tm,),
        in_specs=in_specs,
        out_specs=out_specs,
        out_shape=out_shape,
        compiler_params=_cp("parallel"),
        name="out_proj",
    )(*args)


def _ffn_kernel(*refs, gated, chunks_per_expert):
    u_ref, w1_ref, w3_ref, w2_ref = refs[:4]
    pos = 4
    if gated:
        gate_ref = refs[pos]
        pos += 1
    x1_ref, g2_ref, lng_ref, lnb_ref, o_ref, acc_scr = refs[pos:pos + 6]
    j = pl.program_id(1)

    @pl.when(j == 0)
    def _():
        acc_scr[...] = jnp.zeros(acc_scr.shape, F32)

    u = u_ref[...]
    h1 = _dot(u, w1_ref[...])
    h = _silu(h1) * _dot(u, w3_ref[...])
    if gated:
        gate = gate_ref[...]
        lane = lax.broadcasted_iota(jnp.int32, gate.shape, 1)
        h = h * jnp.sum(jnp.where(lane == j // chunks_per_expert, gate, 0.0), axis=1, keepdims=True)
    acc_scr[...] += _dot(h.astype(BF16), w2_ref[...])

    @pl.when(j == pl.num_programs(1) - 1)
    def _():
        o_ref[...] = _layer_norm(ALPHA * x1_ref[...] + g2_ref[...] * acc_scr[...], lng_ref[...], lnb_ref[...])


def _ffn(u2, w1, w3, w2, x1, mods3, row_fn, ln_g, ln_b, tf, gate=None):
    t = u2.shape[0]
    tm = 512
    n_e, _, f = w1.shape
    cpe = f // tf
    in_specs = [pl.BlockSpec((tm, D), lambda i, j: (i, 0)),
                pl.BlockSpec((None, D, tf), lambda i, j: (j // cpe, 0, j % cpe)),
                pl.BlockSpec((None, D, tf), lambda i, j: (j // cpe, 0, j % cpe)),
                pl.BlockSpec((None, tf, D), lambda i, j: (j // cpe, j % cpe, 0))]
    args = [u2, w1, w3, w2]
    if gate is not None:
        in_specs.append(pl.BlockSpec((tm, LANES), lambda i, j: (i, 0)))
        args.append(gate)
    in_specs += [pl.BlockSpec((tm, D), lambda i, j: (i, 0)),
                 pl.BlockSpec((None, 1, D), lambda i, j: (row_fn(i, tm), 0, 5)),
                 pl.BlockSpec((1, D), lambda i, j: (0, 0)),
                 pl.BlockSpec((1, D), lambda i, j: (0, 0))]
    args += [x1, mods3, ln_g.reshape(1, D), ln_b.reshape(1, D)]
    return pl.pallas_call(
        functools.partial(_ffn_kernel, gated=gate is not None, chunks_per_expert=cpe),
        grid=(t // tm, n_e * cpe),
        in_specs=in_specs,
        out_specs=pl.BlockSpec((tm, D), lambda i, j: (i, 0)),
        out_shape=jax.ShapeDtypeStruct((t, D), F32),
        scratch_shapes=[pltpu.VMEM((tm, D), F32)],
        compiler_params=_cp("parallel", "arbitrary"),
        name="ffn",
    )(*args)


def _conv_kernel(x_ref, w_ref, b_ref, o_ref):
    x = x_ref[...].astype(F32)
    s_len = x.shape[0]
    t = lax.broadcasted_iota(jnp.int32, x.shape, 0)
    prev = jnp.where(t == 0, 0.0, pltpu.roll(x, 1, axis=0))
    nxt = jnp.where(t == s_len - 1, 0.0, pltpu.roll(x, s_len - 1, axis=0))
    y = w_ref[0:1, :] * prev + w_ref[1:2, :] * x + w_ref[2:3, :] * nxt + b_ref[...]
    o_ref[...] = _silu(y).astype(o_ref.dtype)


def _conv(pm, conv_w, conv_b, bsz, s_len):
    t = pm.shape[0]
    tc = 512
    n_ch = conv_w.shape[1]
    off = D_INNER // tc
    return pl.pallas_call(
        _conv_kernel,
        grid=(bsz, n_ch // tc),
        in_specs=[pl.BlockSpec((s_len, tc), lambda b, c: (b, off + c)),
                  pl.BlockSpec((3, tc), lambda b, c: (0, c)),
                  pl.BlockSpec((1, tc), lambda b, c: (0, c))],
        out_specs=pl.BlockSpec((s_len, tc), lambda b, c: (b, c)),
        out_shape=jax.ShapeDtypeStruct((t, n_ch), BF16),
        compiler_params=_cp("parallel", "parallel"),
        name="ssd_conv",
    )(pm, conv_w, conv_b.reshape(1, n_ch))


def _ssd_kernel(*refs, nc, has_init, out_state):
    x_ref, b_ref, c_ref, dt_ref, z_ref, dtb_ref, alog_ref, dsk_ref, ng_ref = refs[:9]
    pos = 9
    if has_init:
        s0_ref = refs[pos]
        pos += 1
    yn_ref = refs[pos]
    pos += 1
    if out_state:
        sst_ref = refs[pos]
        pos += 1
    s_scr, yb_scr, y_scr, xw_scr, dec_scr = refs[pos:pos + 5]

    j = pl.program_id(1)
    is_fwd = j >= nc
    ci = jnp.where(is_fwd, j - nc, nc - 1 - j)
    sgn = jnp.where(is_fwd, 1, -1)

    def _init(d):
        if has_init:
            s_scr[...] = s0_ref[d]
        else:
            s_scr[...] = jnp.zeros(s_scr.shape, F32)

    pl.when(j == 0)(lambda: _init(1))
    pl.when(j == nc)(lambda: _init(0))

    row = lax.broadcasted_iota(jnp.int32, (CH, CH), 0)
    col = lax.broadcasted_iota(jnp.int32, (CH, CH), 1)
    causal = (col - row) * sgn <= 0
    tri = jnp.where(causal, 1.0, 0.0).astype(BF16)
    fwd_row = (jnp.zeros((1, LANES), jnp.int32) + sgn) > 0
    fwd_mat = (jnp.zeros((CH, LANES), jnp.int32) + sgn) > 0

    raw = dt_ref[...] + dtb_ref[...]
    raw = jnp.where(fwd_mat, raw, pltpu.roll(raw, LANES - H_C, axis=1))
    alog = alog_ref[...]
    alog = jnp.where(fwd_row, alog, pltpu.roll(alog, LANES - H_C, axis=1))
    dt = _softplus(raw)
    cum = _tri_dot(tri, dt * (-jnp.exp(alog)))
    cum_t = cum.T
    dt_t = dt.T
    ecum = jnp.exp(cum)
    clast = jnp.where(fwd_row, cum[CH - 1:CH, :], cum[0:1, :])
    ws = jnp.exp(clast - cum) * dt
    dec = jnp.exp(clast)

    for gi in range(NG_C):
        gsl = slice(gi * D_STATE, (gi + 1) * D_STATE)
        bg = b_ref[:, gsl]
        cg = c_ref[:, gsl]
        cb = _dot_nt(cg, bg)
        bg_t = bg.astype(F32).T.astype(BF16)
        ys = _dot(cg, s_scr[gi].astype(BF16))
        for pr in range(HPG_C // 2):
            y_pair, xw_pair, dec_pair = [], [], []
            for rr in (2 * pr, 2 * pr + 1):
                r = gi * HPG_C + rr
                seg = jnp.where(causal, cum[:, r:r + 1] - cum_t[r:r + 1, :], NEG)
                w = (cb * jnp.exp(seg) * dt_t[r:r + 1, :]).astype(BF16)
                xr = x_ref[:, r * HD:(r + 1) * HD]
                y_pair.append(_dot(w, xr) + ys[:, rr * HD:(rr + 1) * HD] * ecum[:, r:r + 1])
                xw_pair.append(xr.astype(F32) * ws[:, r:r + 1])
                dec_pair.append(jnp.broadcast_to(dec[:, r:r + 1], (1, HD)))
            y_scr[:, (gi * 4 + pr) * LANES:(gi * 4 + pr + 1) * LANES] = jnp.concatenate(y_pair, axis=1)
            xw_scr[:, pr * LANES:(pr + 1) * LANES] = jnp.concatenate(xw_pair, axis=1).astype(BF16)
            dec_scr[:, pr * LANES:(pr + 1) * LANES] = jnp.concatenate(dec_pair, axis=1)
        s_scr[gi] = s_scr[gi] * dec_scr[...] + _dot(bg_t, xw_scr[...])

    rows = pl.ds(pl.multiple_of(ci * CH, CH), CH)

    @pl.when(j < nc)
    def _():
        yb_scr[rows, :] = y_scr[...]

    @pl.when(j >= nc)
    def _():
        y = y_scr[...] + yb_scr[rows, :] + dsk_ref[...] * x_ref[...].astype(F32)
        y = y * _silu(z_ref[...].astype(F32))
        outs = []
        gw = D_INNER // NG_C
        for gi in range(NG_C):
            yg = y[:, gi * gw:(gi + 1) * gw]
            outs.append(yg * lax.rsqrt(jnp.mean(yg * yg, axis=1, keepdims=True) + EPS))
        yn_ref[...] = (jnp.concatenate(outs, axis=1) * ng_ref[...]).astype(yn_ref.dtype)

    if out_state:
        @pl.when(j == nc - 1)
        def _():
            sst_ref[1] = s_scr[...]

        @pl.when(j == 2 * nc - 1)
        def _():
            sst_ref[0] = s_scr[...]


def _ssd(xc, pa, pm, dtb, alog, dsk, norm_g, bsz, s_len, init=None, out_state=False):
    nc = s_len // CH
    t = bsz * s_len
    gw = NG_C * D_STATE

    def chunk(b, j):
        return b * nc + jnp.where(j >= nc, j - nc, nc - 1 - j)

    def chunk_fwd(b, j):
        return b * nc + jnp.maximum(j - nc, 0)

    in_specs = [pl.BlockSpec((CH, D_INNER), lambda b, j: (chunk(b, j), 0)),
                pl.BlockSpec((CH, gw), lambda b, j: (chunk(b, j), D_INNER // gw)),
                pl.BlockSpec((CH, gw), lambda b, j: (chunk(b, j), D_INNER // gw + 1)),
                pl.BlockSpec((CH, LANES), lambda b, j: (chunk(b, j), 0)),
                pl.BlockSpec((CH, D_INNER), lambda b, j: (chunk_fwd(b, j), 0)),
                pl.BlockSpec((1, LANES), lambda b, j: (0, 0)),
                pl.BlockSpec((1, LANES), lambda b, j: (0, 0)),
                pl.BlockSpec((1, D_INNER), lambda b, j: (0, 0)),
                pl.BlockSpec((1, D_INNER), lambda b, j: (0, 0))]
    args = [xc, xc, xc, pa, pm, dtb, alog, dsk, norm_g]
    state_spec = pl.BlockSpec((None, 2, NG_C, D_STATE, HPG_C * HD), lambda b, j: (b, 0, 0, 0, 0))
    if init is not None:
        in_specs.append(state_spec)
        args.append(init)
    out_specs = [pl.BlockSpec((CH, D_INNER), lambda b, j: (chunk_fwd(b, j), 0))]
    out_shape = [jax.ShapeDtypeStruct((t, D_INNER), BF16)]
    if out_state:
        out_specs.append(state_spec)
        out_shape.append(jax.ShapeDtypeStruct((bsz, 2, NG_C, D_STATE, HPG_C * HD), F32))
    return pl.pallas_call(
        functools.partial(_ssd_kernel, nc=nc, has_init=init is not None, out_state=out_state),
        grid=(bsz, 2 * nc),
        in_specs=in_specs,
        out_specs=out_specs,
        out_shape=out_shape,
        scratch_shapes=[pltpu.VMEM((NG_C, D_STATE, HPG_C * HD), F32),
                        pltpu.VMEM((s_len, D_INNER), F32),
                        pltpu.VMEM((CH, D_INNER), F32),
                        pltpu.VMEM((CH, HPG_C * HD), BF16),
                        pltpu.VMEM((1, HPG_C * HD), F32)],
        compiler_params=_cp("parallel", "arbitrary"),
        name="ssd_scan",
    )(*args)


def _rope_tables(s_len):
    nf = HD // 4
    inv = ROPE_THETA ** (-jnp.arange(nf, dtype=F32) / nf)
    tpos = jnp.arange(s_len)
    ang_r = (tpos // GRID_W).astype(F32)[:, None] * inv
    ang_c = (tpos % GRID_W).astype(F32)[:, None] * inv
    cos = jnp.concatenate([jnp.cos(ang_r)] * 2 + [jnp.cos(ang_c)] * 2, axis=1)
    sin = jnp.concatenate([-jnp.sin(ang_r), jnp.sin(ang_r), -jnp.sin(ang_c), jnp.sin(ang_c)], axis=1)
    return jnp.tile(cos, (1, 2)), jnp.tile(sin, (1, 2))


def _block_avg(n):
    idx = np.arange(n) // HD
    return jnp.asarray((idx[:, None] == idx[None, :]).astype(np.float32) / HD, dtype=BF16)


def _pad_cols(w, n):
    return jnp.pad(w, ((0, 0), (0, n - w.shape[1])))


def _row_prompt(i, tm):
    return 0


def _make_row_sample(s_len):
    def row(i, tm):
        return 1 + (i * tm) // s_len
    return row


def kernel(x_prompt, x_sample, cache_k_l0, cache_v_l0, state_mlstm_c_l0, state_mlstm_n_l0, state_mlstm_m_l0, state_ssd_l1, c, c_ctx, ada_w_l0, ada_b_l0, mix_w_in_l0, mlstm_gate_b_l0, mlstm_norm_g_l0, q_norm_g_l0, k_norm_g_l0, mix_w_out_l0, ln1_g_l0, ln1_b_l0, ln2_g_l0, ln2_b_l0, ffn_w1_l0, ffn_w3_l0, ffn_w2_l0, ada_w_l1, ada_b_l1, ssd_w_in_l1, ssd_conv_w_l1, ssd_conv_b_l1, ssd_dt_bias_l1, ssd_a_log_l1, ssd_d_l1, ssd_norm_g_l1, ssd_w_out_l1, ln1_g_l1, ln1_b_l1, ln2_g_l1, ln2_b_l1, router_w_l1, router_b_l1, moe_w1_l1, moe_w3_l1, moe_w2_l1):
    bp, sp, _ = x_prompt.shape
    bs, ss, _ = x_sample.shape
    past = cache_k_l0.shape[1]
    groups = [
        dict(x=x_prompt.reshape(bp * sp, D), b=bp, s=sp, row=_row_prompt, prompt=True),
        dict(x=x_sample.reshape(bs * ss, D), b=bs, s=ss, row=_make_row_sample(ss), prompt=False),
    ]

    cvec = jnp.zeros((16, D), F32).at[0].set(c_ctx).at[1:1 + bs].set(c)
    mods0 = _ada(cvec, ada_w_l0, ada_b_l0).reshape(16, 1, 6 * D)
    mods1 = _ada(cvec, ada_w_l1, ada_b_l1).reshape(16, 1, 6 * D)

    w0_main = jnp.concatenate([mix_w_in_l0[:, :2048], mix_w_in_l0[:, 2080:]], axis=1).astype(BF16)
    w0_aux_hi, w0_aux_lo = _split2(_pad_cols(mix_w_in_l0[:, 2048:2080], LANES))
    gbias = _pad_cols(mlstm_gate_b_l0.reshape(1, 4 * H_A), LANES)
    avg512 = _block_avg(W_A)
    avg128 = _block_avg(LANES)
    q_g = jnp.tile(q_norm_g_l0, W_A // HD).reshape(1, W_A)
    k_g = jnp.tile(k_norm_g_l0, KV_B).reshape(1, LANES)
    w_out_h = mix_w_out_l0[:W_A].astype(BF16)
    w_out_o = mix_w_out_l0[W_A:].astype(BF16)
    ffn_w1 = ffn_w1_l0.astype(BF16)[None]
    ffn_w3 = ffn_w3_l0.astype(BF16)[None]
    ffn_w2 = ffn_w2_l0.astype(BF16)[None]
    tables = _rope_tables(ss)
    kc = cache_k_l0.transpose(2, 0, 1, 3).reshape(KV_B, bs * past, HD).astype(BF16)
    vc = cache_v_l0.transpose(2, 0, 1, 3).reshape(KV_B, bs * past, HD)
    vc = jnp.concatenate([vc, jnp.ones_like(vc)], axis=-1).astype(BF16)
    c0 = jnp.concatenate([state_mlstm_c_l0, state_mlstm_n_l0[..., None],
                          jnp.zeros(state_mlstm_c_l0.shape[:-1] + (HD - 1,), F32)], axis=-1)
    m0 = jnp.broadcast_to(state_mlstm_m_l0[..., None], state_mlstm_m_l0.shape + (LANES,))

    n_main1 = 2 * D_INNER + 2 * NG_C * D_STATE
    w1_main = ssd_w_in_l1[:, :n_main1].astype(BF16)
    w1_aux_hi, w1_aux_lo = _split2(_pad_cols(ssd_w_in_l1[:, n_main1:], LANES))
    dtb = _pad_cols(ssd_dt_bias_l1.reshape(1, 2 * H_C), LANES)
    alog = _pad_cols(ssd_a_log_l1.reshape(1, 2 * H_C), LANES)
    dsk = jnp.repeat(ssd_d_l1, HD).reshape(1, D_INNER)
    ssd_ng = ssd_norm_g_l1.reshape(1, D_INNER)
    w_out1 = ssd_w_out_l1.astype(BF16)
    rw_hi, rw_lo = _split2(_pad_cols(router_w_l1, LANES))
    rb = jnp.full((1, LANES), NEG, F32).at[0, :N_EXPERTS].set(router_b_l1)
    moe_w1 = moe_w1_l1.astype(BF16)
    moe_w3 = moe_w3_l1.astype(BF16)
    moe_w2 = moe_w2_l1.astype(BF16)
    s0 = state_ssd_l1.reshape(bs, 2, NG_C, HPG_C, HD, D_STATE).transpose(0, 1, 2, 5, 3, 4)
    s0 = s0.reshape(bs, 2, NG_C, D_STATE, HPG_C * HD)

    outs = {}
    for grp in groups:
        x, b, s, row, prompt = grp["x"], grp["b"], grp["s"], grp["row"], grp["prompt"]
        pm, pa = _mm_in(x, mods0, row, w0_main, w0_aux_hi, w0_aux_lo, tm=512, tn=w0_main.shape[1])
        if prompt:
            hn, cst, mst = _mlstm(pm, pa, gbias, mlstm_norm_g_l0.reshape(1, W_A), avg512, b, s, out_state=True)
            qn, kh, vh, kn = _qkprep(pm, q_g, k_g, avg128, s, emit_kn=True)
            o = _attn(qn, kh, vh, b, s)
            outs["k"] = kn.reshape(b, s, KV_B, HD)
            outs["v"] = pm[:, 2688:2816].astype(F32).reshape(b, s, KV_B, HD)
            outs["c"] = cst[..., :HD]
            outs["n"] = cst[..., HD]
            outs["m"] = mst[..., 0]
        else:
            hn = _mlstm(pm, pa, gbias, mlstm_norm_g_l0.reshape(1, W_A), avg512, b, s, init=(c0, m0))[0]
            qn, kh, vh = _qkprep(pm, q_g, k_g, avg128, s, tables=tables)
            o = _attn(qn, kh, vh, b, s, ctx=(kc, vc))
        x1, u2 = _out_proj([hn, o], [w_out_h, w_out_o], x, mods0, row, ln1_g_l0, ln1_b_l0)
        x2 = _ffn(u2, ffn_w1, ffn_w3, ffn_w2, x1, mods0, row, ln2_g_l0, ln2_b_l0, tf=D_FF // 2)
        pm, pa = _mm_in(x2, mods1, row, w1_main, w1_aux_hi, w1_aux_lo, tm=512, tn=1280)
        xc = _conv(pm, ssd_conv_w_l1, ssd_conv_b_l1, b, s)
        if prompt:
            yn, sst = _ssd(xc, pa, pm, dtb, alog, dsk, ssd_ng, b, s, out_state=True)
            sst = sst.reshape(b, 2, NG_C, D_STATE, HPG_C, HD).transpose(0, 1, 2, 4, 5, 3)
            outs["s"] = sst.reshape(b, 2, H_C, HD, D_STATE)
        else:
            yn = _ssd(xc, pa, pm, dtb, alog, dsk, ssd_ng, b, s, init=s0)[0]
        x1, u2, gate = _out_proj([yn], [w_out1], x2, mods1, row, ln1_g_l1, ln1_b_l1, router=(rw_hi, rw_lo, rb))
        x3 = _ffn(u2, moe_w1, moe_w3, moe_w2, x1, mods1, row, ln2_g_l1, ln2_b_l1, tf=D_FF_E, gate=gate)
        outs["y_p" if prompt else "y_s"] = x3.reshape(b, s, D)

    return (outs["y_p"], outs["y_s"], outs["k"], outs["v"], outs["c"], outs["n"], outs["m"], outs["s"])
```

```python
import functools

import jax
import jax.numpy as jnp
import numpy as np
from jax import lax
from jax.experimental import pallas as pl
from jax.experimental.pallas import tpu as pltpu

F32 = jnp.float32
BF16 = jnp.bfloat16

D = 1024
CH = 128
LANES = 128
EPS = 1e-6
DEPTH = 2
ALPHA = (2 * DEPTH) ** 0.25
GRID_W = 64
ROPE_THETA = 10000.0
H_A = 8
HD = 64
W_A = 512
KV_B = 2
H_C = 32
NG_C = 4
HPG_C = 8
D_STATE = 128
D_INNER = 2048
D_FF = 2816
N_EXPERTS = 8
D_FF_E = 1408
NEG = -1e30
VMEM_LIMIT = 56 * 1024 * 1024


def _cp(*sem):
    return pltpu.CompilerParams(dimension_semantics=sem, vmem_limit_bytes=VMEM_LIMIT)


def _sigmoid(x):
    return 1.0 / (1.0 + jnp.exp(-x))


def _silu(x):
    return x * _sigmoid(x)


def _softplus(x):
    return jnp.maximum(x, 0.0) + jnp.log(1.0 + jnp.exp(-jnp.abs(x)))


def _split2(x):
    hi = x.astype(BF16)
    lo = (x - hi.astype(F32)).astype(BF16)
    return hi, lo


def _split3(x):
    h1 = x.astype(BF16)
    r = x - h1.astype(F32)
    h2 = r.astype(BF16)
    h3 = (r - h2.astype(F32)).astype(BF16)
    return h1, h2, h3


def _dot(a, b):
    return jnp.dot(a, b, preferred_element_type=F32)


def _dot_nt(a, b):
    return lax.dot_general(a, b, (((1,), (1,)), ((), ())), preferred_element_type=F32)


def _tri_dot(tri, x):
    h1, h2, h3 = _split3(x)
    return _dot(tri, h1) + _dot(tri, h2) + _dot(tri, h3)


def _layer_norm(r, g, b):
    mu = jnp.mean(r, axis=-1, keepdims=True)
    d = r - mu
    var = jnp.mean(d * d, axis=-1, keepdims=True)
    return d * lax.rsqrt(var + EPS) * g + b


def _ada_kernel(c_ref, w_ref, b_ref, o_ref):
    s = _silu(c_ref[...])
    sh, sl = _split2(s)
    wh, wl = _split2(w_ref[...])
    o_ref[...] = _dot(sh, wh) + _dot(sl, wh) + _dot(sh, wl) + b_ref[...]


def _ada(cvec, w, b):
    n = w.shape[1]
    tn = 768
    return pl.pallas_call(
        _ada_kernel,
        grid=(n // tn,),
        in_specs=[pl.BlockSpec((16, D), lambda j: (0, 0)),
                  pl.BlockSpec((D, tn), lambda j: (0, j)),
                  pl.BlockSpec((1, tn), lambda j: (0, j))],
        out_specs=pl.BlockSpec((16, tn), lambda j: (0, j)),
        out_shape=jax.ShapeDtypeStruct((16, n), F32),
        compiler_params=_cp("parallel"),
        name="ada_mods",
    )(cvec, w, b.reshape(1, n))


def _mm_in_kernel(x_ref, sh_ref, sc_ref, wm_ref, wah_ref, wal_ref, om_ref, oa_ref, u_scr):
    @pl.when(pl.program_id(1) == 0)
    def _():
        u = x_ref[...] * (1.0 + sc_ref[...]) + sh_ref[...]
        uh, ul = _split2(u)
        u_scr[...] = uh
        wah = wah_ref[...]
        oa_ref[...] = _dot(uh, wah) + _dot(ul, wah) + _dot(uh, wal_ref[...])

    om_ref[...] = _dot(u_scr[...], wm_ref[...]).astype(om_ref.dtype)


def _mm_in(x, mods3, row_fn, wm, wa_hi, wa_lo, tm, tn):
    t, nm = x.shape[0], wm.shape[1]
    return pl.pallas_call(
        _mm_in_kernel,
        grid=(t // tm, nm // tn),
        in_specs=[pl.BlockSpec((tm, D), lambda i, j: (i, 0)),
                  pl.BlockSpec((None, 1, D), lambda i, j: (row_fn(i, tm), 0, 0)),
                  pl.BlockSpec((None, 1, D), lambda i, j: (row_fn(i, tm), 0, 1)),
                  pl.BlockSpec((D, tn), lambda i, j: (0, j)),
                  pl.BlockSpec((D, LANES), lambda i, j: (0, 0)),
                  pl.BlockSpec((D, LANES), lambda i, j: (0, 0))],
        out_specs=[pl.BlockSpec((tm, tn), lambda i, j: (i, j)),
                   pl.BlockSpec((tm, LANES), lambda i, j: (i, 0))],
        out_shape=[jax.ShapeDtypeStruct((t, nm), BF16), jax.ShapeDtypeStruct((t, LANES), F32)],
        scratch_shapes=[pltpu.VMEM((tm, D), BF16)],
        compiler_params=_cp("parallel", "arbitrary"),
        name="mm_in",
    )(x, mods3, mods3, wm, wa_hi, wa_lo)


def _mlstm_kernel(*refs, nc, has_init, out_state):
    q_ref, k_ref, v_ref, om_ref, g_ref, gn_ref, gb_ref, ng_ref = refs[:8]
    pos = 8
    if has_init:
        c0_ref, m0_ref = refs[pos:pos + 2]
        pos += 2
    hn_ref = refs[pos]
    pos += 1
    if out_state:
        cst_ref, mst_ref = refs[pos:pos + 2]
        pos += 2
    c_scr, m_scr, hb_scr, r_scr, rows_scr = refs[pos:pos + 5]

    j = pl.program_id(1)
    is_fwd = j >= nc
    ci = jnp.where(is_fwd, j - nc, nc - 1 - j)
    sgn = jnp.where(is_fwd, 1, -1)

    def _init(d):
        if has_init:
            c_scr[...] = c0_ref[d]
            m_scr[...] = m0_ref[d]
        else:
            c_scr[...] = jnp.zeros(c_scr.shape, F32)
            m_scr[...] = jnp.zeros(m_scr.shape, F32)

    pl.when(j == 0)(lambda: _init(1))
    pl.when(j == nc)(lambda: _init(0))

    def _prep(graw_ref, sg, slot):
        row = lax.broadcasted_iota(jnp.int32, (CH, CH), 0)
        col = lax.broadcasted_iota(jnp.int32, (CH, CH), 1)
        tri = jnp.where((col - row) * sg <= 0, 1.0, 0.0).astype(BF16)
        f_mat = (jnp.zeros((CH, LANES), jnp.int32) + sg) > 0
        f8 = (jnp.zeros((H_A, LANES), jnp.int32) + sg) > 0
        lane8 = lax.broadcasted_iota(jnp.int32, (H_A, LANES), 1)
        g = graw_ref[...] + gb_ref[...]
        li = jnp.where(f_mat, g, pltpu.roll(g, LANES - 16, axis=1))
        lfr = jnp.where(f_mat, pltpu.roll(g, LANES - 8, axis=1), pltpu.roll(g, LANES - 24, axis=1))
        lf = jnp.minimum(lfr, 0.0) - jnp.log(1.0 + jnp.exp(-jnp.abs(lfr)))
        bc = _tri_dot(tri, lf)
        rr = li - bc
        rr_t = rr.T[0:H_A, :]
        cmx = rr_t
        for dlt in (1, 2, 4, 8, 16, 32, 64):
            before = jnp.where(lane8 >= dlt, pltpu.roll(cmx, dlt, axis=1), NEG)
            after = jnp.where(lane8 < LANES - dlt, pltpu.roll(cmx, LANES - dlt, axis=1), NEG)
            cmx = jnp.maximum(cmx, jnp.where(f8, before, after))
        r_scr[slot] = rr
        rows_scr[slot, 0] = rr_t
        rows_scr[slot, 1] = bc.T[0:H_A, :]
        rows_scr[slot, 2] = cmx

    pl.when(j == 0)(lambda: _prep(g_ref, sgn, 0))

    slot = j % 2
    row = lax.broadcasted_iota(jnp.int32, (CH, CH), 0)
    col = lax.broadcasted_iota(jnp.int32, (CH, CH), 1)
    causal_t = (row - col) * sgn <= 0
    fwd8 = (jnp.zeros((H_A, LANES), jnp.int32) + sgn) > 0
    r = r_scr[slot]
    r_t = rows_scr[slot, 0]
    b_t = rows_scr[slot, 1]
    cm = rows_scr[slot, 2]
    _prep(gn_ref, jnp.where(j + 1 >= nc, 1, -1), (j + 1) % 2)

    m_prev = m_scr[...]
    g_t = jnp.maximum(m_prev, cm)
    mt_t = b_t + g_t
    wst_t = jnp.exp(m_prev - g_t)
    emt_t = jnp.exp(-mt_t)
    fwd_c = fwd8[:, 0:1]
    m_new = jnp.where(fwd_c, mt_t[:, LANES - 1:LANES], mt_t[:, 0:1])
    b_last = jnp.where(fwd_c, b_t[:, LANES - 1:LANES], b_t[:, 0:1])
    ws_t = jnp.exp(r_t + (b_last - m_new))
    decay = jnp.exp(b_last + m_prev[:, 0:1] - m_new)
    m_scr[...] = jnp.broadcast_to(m_new, (H_A, LANES))

    q_t = q_ref[...].astype(F32).T.astype(BF16)
    v_t = v_ref[...].astype(F32).T
    ones = jnp.ones((HD, CH), F32)

    hts = []
    for h in range(H_A):
        sl = slice(h * HD, (h + 1) * HD)
        w_t = jnp.exp(jnp.where(causal_t, r[:, h:h + 1] - g_t[h:h + 1, :], NEG))
        kh = k_ref[:, sl]
        qh_t = q_t[sl, :]
        vext_t = jnp.concatenate([v_t[sl, :], ones], axis=0)
        sc_t = (_dot(kh, qh_t) * 0.125 * w_t).astype(BF16)
        cext_t = c_scr[h]
        nd_t = _dot(vext_t.astype(BF16), sc_t) + (_dot(cext_t.astype(BF16), qh_t) * 0.125) * wst_t[h:h + 1, :]
        den = jnp.maximum(jnp.abs(nd_t[HD:HD + 1, :]), emt_t[h:h + 1, :])
        hts.append(nd_t[0:HD, :] / den)
        c_scr[h] = decay[h:h + 1, :] * cext_t + _dot((vext_t * ws_t[h:h + 1, :]).astype(BF16), kh)
    hcat_t = jnp.concatenate(hts, axis=0)

    @pl.when(j < nc)
    def _():
        hb_scr[ci] = hcat_t

    @pl.when(j >= nc)
    def _():
        ht = hcat_t + hb_scr[ci]
        normed = []
        for h in range(H_A):
            hh = ht[h * HD:(h + 1) * HD, :]
            dlt = hh - jnp.mean(hh, axis=0, keepdims=True)
            normed.append(dlt * lax.rsqrt(jnp.mean(dlt * dlt, axis=0, keepdims=True) + EPS))
        hn = jnp.concatenate(normed, axis=0).T * ng_ref[...] * _sigmoid(om_ref[...].astype(F32))
        hn_ref[...] = hn.astype(hn_ref.dtype)

    if out_state:
        @pl.when(j == nc - 1)
        def _():
            cst_ref[1] = c_scr[...]
            mst_ref[1] = m_scr[...]

        @pl.when(j == 2 * nc - 1)
        def _():
            cst_ref[0] = c_scr[...]
            mst_ref[0] = m_scr[...]


def _mlstm(pm, pa, gbias, norm_g, bsz, s_len, init=None, out_state=False):
    nc = s_len // CH
    t = bsz * s_len

    def chunk(b, j):
        return b * nc + jnp.where(j >= nc, j - nc, nc - 1 - j)

    def chunk_next(b, j):
        return chunk(b, jnp.minimum(j + 1, 2 * nc - 1))

    def chunk_fwd(b, j):
        return b * nc + jnp.maximum(j - nc, 0)

    in_specs = [pl.BlockSpec((CH, W_A), lambda b, j: (chunk(b, j), 0)),
                pl.BlockSpec((CH, W_A), lambda b, j: (chunk(b, j), 1)),
                pl.BlockSpec((CH, W_A), lambda b, j: (chunk(b, j), 2)),
                pl.BlockSpec((CH, W_A), lambda b, j: (chunk_fwd(b, j), 3)),
                pl.BlockSpec((CH, LANES), lambda b, j: (chunk(b, j), 0)),
                pl.BlockSpec((CH, LANES), lambda b, j: (chunk_next(b, j), 0)),
                pl.BlockSpec((1, LANES), lambda b, j: (0, 0)),
                pl.BlockSpec((1, W_A), lambda b, j: (0, 0))]
    args = [pm, pm, pm, pm, pa, pa, gbias, norm_g]
    if init is not None:
        in_specs += [pl.BlockSpec((None, 2, H_A, LANES, HD), lambda b, j: (b, 0, 0, 0, 0)),
                     pl.BlockSpec((None, 2, H_A, LANES), lambda b, j: (b, 0, 0, 0))]
        args += list(init)
    out_specs = [pl.BlockSpec((CH, W_A), lambda b, j: (chunk_fwd(b, j), 0))]
    out_shape = [jax.ShapeDtypeStruct((t, W_A), BF16)]
    if out_state:
        out_specs += [pl.BlockSpec((None, 2, H_A, LANES, HD), lambda b, j: (b, 0, 0, 0, 0)),
                      pl.BlockSpec((None, 2, H_A, LANES), lambda b, j: (b, 0, 0, 0))]
        out_shape += [jax.ShapeDtypeStruct((bsz, 2, H_A, LANES, HD), F32),
                      jax.ShapeDtypeStruct((bsz, 2, H_A, LANES), F32)]
    return pl.pallas_call(
        functools.partial(_mlstm_kernel, nc=nc, has_init=init is not None, out_state=out_state),
        grid=(bsz, 2 * nc),
        in_specs=in_specs,
        out_specs=out_specs,
        out_shape=out_shape,
        scratch_shapes=[pltpu.VMEM((H_A, LANES, HD), F32), pltpu.VMEM((H_A, LANES), F32),
                        pltpu.VMEM((nc, W_A, CH), F32),
                        pltpu.VMEM((2, CH, LANES), F32), pltpu.VMEM((2, 3, H_A, LANES), F32)],
        compiler_params=_cp("parallel", "arbitrary"),
        name="mlstm",
    )(*args)


def _group_rms(x, avg, g):
    s1, s2 = _split2(x * x)
    ms = _dot(s1, avg) + _dot(s2, avg)
    return x * lax.rsqrt(ms + EPS) * g


def _rope(x, cos, sin):
    lane = lax.broadcasted_iota(jnp.int32, x.shape, 1)
    swapped = jnp.where((lane % 32) < 16, pltpu.roll(x, LANES - 16, axis=1), pltpu.roll(x, 16, axis=1))
    return x * cos + swapped * sin


def _qkprep_kernel(*refs, rope, emit_kn):
    q_ref, k_ref, v_ref, qg_ref, kg_ref, avg_ref = refs[:6]
    pos = 6
    if rope:
        cos_ref, sin_ref = refs[pos:pos + 2]
        pos += 2
    qn_ref, kh_ref, vh_ref = refs[pos:pos + 3]
    pos += 3
    avg = avg_ref[...]
    qs = []
    for c in range(W_A // LANES):
        sl = slice(c * LANES, (c + 1) * LANES)
        xn = _group_rms(q_ref[:, sl].astype(F32), avg, qg_ref[:, sl])
        if rope:
            xn = _rope(xn, cos_ref[...], sin_ref[...])
        qs.append(xn * 0.125)
    qn_ref[...] = jnp.concatenate(qs, axis=1).astype(qn_ref.dtype)
    kn = _group_rms(k_ref[...].astype(F32), avg, kg_ref[...])
    if emit_kn:
        refs[pos][...] = kn
    if rope:
        kn = _rope(kn, cos_ref[...], sin_ref[...])
    v = v_ref[...].astype(F32)
    ones = jnp.ones((v.shape[0], HD), F32)
    for gi in range(KV_B):
        sl = slice(gi * HD, (gi + 1) * HD)
        kh_ref[gi] = kn[:, sl].astype(kh_ref.dtype)
        vh_ref[gi] = jnp.concatenate([v[:, sl], ones], axis=1).astype(vh_ref.dtype)


def _qkprep(pm, q_g, k_g, avg128, s_len, tables=None, emit_kn=False):
    t = pm.shape[0]
    tm = 256
    nt = s_len // tm
    qoff, koff, voff = 2048 // W_A, 2560 // LANES, 2688 // LANES
    in_specs = [pl.BlockSpec((tm, W_A), lambda i: (i, qoff)),
                pl.BlockSpec((tm, LANES), lambda i: (i, koff)),
                pl.BlockSpec((tm, LANES), lambda i: (i, voff)),
                pl.BlockSpec((1, W_A), lambda i: (0, 0)),
                pl.BlockSpec((1, LANES), lambda i: (0, 0)),
                pl.BlockSpec((LANES, LANES), lambda i: (0, 0))]
    args = [pm, pm, pm, q_g, k_g, avg128]
    if tables is not None:
        in_specs += [pl.BlockSpec((tm, LANES), lambda i: (i % nt, 0))] * 2
        args += list(tables)
    out_specs = [pl.BlockSpec((tm, W_A), lambda i: (i, 0)),
                 pl.BlockSpec((KV_B, tm, HD), lambda i: (0, i, 0)),
                 pl.BlockSpec((KV_B, tm, LANES), lambda i: (0, i, 0))]
    out_shape = [jax.ShapeDtypeStruct((t, W_A), BF16),
                 jax.ShapeDtypeStruct((KV_B, t, HD), BF16),
                 jax.ShapeDtypeStruct((KV_B, t, LANES), BF16)]
    if emit_kn:
        out_specs.append(pl.BlockSpec((tm, LANES), lambda i: (i, 0)))
        out_shape.append(jax.ShapeDtypeStruct((t, LANES), F32))
    return pl.pallas_call(
        functools.partial(_qkprep_kernel, rope=tables is not None, emit_kn=emit_kn),
        grid=(t // tm,),
        in_specs=in_specs,
        out_specs=out_specs,
        out_shape=out_shape,
        compiler_params=_cp("parallel"),
        name="qkprep",
    )(*args)


def _attn_kernel(*refs, has_ctx):
    if has_ctx:
        q_ref, k_ref, v_ref, kc_ref, vc_ref, o_ref = refs
    else:
        q_ref, k_ref, v_ref, o_ref = refs
    k = k_ref[...]
    v = v_ref[...]
    outs = []
    for hq in range(4):
        qh = q_ref[:, hq * HD:(hq + 1) * HD]
        s = _dot_nt(qh, k)
        m = jnp.max(s, axis=1, keepdims=True)
        if has_ctx:
            sc = _dot_nt(qh, kc_ref[...])
            m = jnp.maximum(m, jnp.max(sc, axis=1, keepdims=True))
        oe = _dot(jnp.exp(s - m).astype(BF16), v)
        if has_ctx:
            oe = oe + _dot(jnp.exp(sc - m).astype(BF16), vc_ref[...])
        outs.append(oe[:, :HD] / oe[:, HD:HD + 1])
    o_ref[...] = jnp.concatenate(outs, axis=1).astype(o_ref.dtype)


def _attn(qn, kh, vh, bsz, s_len, ctx=None):
    t = qn.shape[0]
    tq = min(256, s_len)
    nq = s_len // tq
    in_specs = [pl.BlockSpec((tq, 256), lambda b, g, i: (b * nq + i, g)),
                pl.BlockSpec((None, s_len, HD), lambda b, g, i: (g, b, 0)),
                pl.BlockSpec((None, s_len, LANES), lambda b, g, i: (g, b, 0))]
    args = [qn, kh, vh]
    if ctx is not None:
        past = ctx[0].shape[1] // bsz
        in_specs += [pl.BlockSpec((None, past, HD), lambda b, g, i: (g, b, 0)),
                     pl.BlockSpec((None, past, LANES), lambda b, g, i: (g, b, 0))]
        args += list(ctx)
    return pl.pallas_call(
        functools.partial(_attn_kernel, has_ctx=ctx is not None),
        grid=(bsz, KV_B, nq),
        in_specs=in_specs,
        out_specs=pl.BlockSpec((tq, 256), lambda b, g, i: (b * nq + i, g)),
        out_shape=jax.ShapeDtypeStruct((t, W_A), BF16),
        compiler_params=_cp("parallel", "parallel", "parallel"),
        name="attn",
    )(*args)


def _out_kernel(*refs, n_in, router):
    a_refs = refs[:n_in]
    w_refs = refs[n_in:2 * n_in]
    x_ref, g1_ref, sh2_ref, sc2_ref, lng_ref, lnb_ref = refs[2 * n_in:2 * n_in + 6]
    pos = 2 * n_in + 6
    if router:
        rwh_ref, rwl_ref, rb_ref = refs[pos:pos + 3]
        pos += 3
    x1_ref, u2_ref = refs[pos:pos + 2]
    y = _dot(a_refs[0][...], w_refs[0][...])
    for a_ref, w_ref in zip(a_refs[1:], w_refs[1:]):
        y = y + _dot(a_ref[...], w_ref[...])
    x1 = _layer_norm(ALPHA * x_ref[...] + g1_ref[...] * y, lng_ref[...], lnb_ref[...])
    x1_ref[...] = x1
    u2 = x1 * (1.0 + sc2_ref[...]) + sh2_ref[...]
    uh, ul = _split2(u2)
    u2_ref[...] = uh
    if router:
        gate_ref = refs[pos + 2]
        rwh = rwh_ref[...]
        logits = _dot(uh, rwh) + _dot(ul, rwh) + _dot(uh, rwl_ref[...]) + rb_ref[...]
        lane = lax.broadcasted_iota(jnp.int32, logits.shape, 1)
        m1 = jnp.max(logits, axis=1, keepdims=True)
        i1 = jnp.min(jnp.where(logits == m1, lane, LANES), axis=1, keepdims=True)
        rest = jnp.where(lane == i1, 2.0 * NEG, logits)
        m2 = jnp.max(rest, axis=1, keepdims=True)
        i2 = jnp.min(jnp.where(rest == m2, lane, LANES), axis=1, keepdims=True)
        e = jnp.exp(m2 - m1)
        p1 = 1.0 / (1.0 + e)
        gate_ref[...] = jnp.where(lane == i1, p1, jnp.where(lane == i2, e * p1, 0.0))


def _out_proj(acts, ws, x, mods3, row_fn, ln_g, ln_b, router=None):
    t = x.shape[0]
    tm = 512
    n_in = len(acts)
    in_specs = [pl.BlockSpec((tm, a.shape[1]), lambda i: (i, 0)) for a in acts]
    in_specs += [pl.BlockSpec(w.shape, lambda i: (0, 0)) for w in ws]
    in_specs += [pl.BlockSpec((tm, D), lambda i: (i, 0))]
    in_specs += [pl.BlockSpec((None, 1, D), functools.partial(lambda i, k: (row_fn(i, tm), 0, k), k=k))
                 for k in (2, 3, 4)]
    in_specs += [pl.BlockSpec((1, D), lambda i: (0, 0))] * 2
    args = list(acts) + list(ws) + [x, mods3, mods3, mods3, ln_g.reshape(1, D), ln_b.reshape(1, D)]
    out_specs = [pl.BlockSpec((tm, D), lambda i: (i, 0)), pl.BlockSpec((tm, D), lambda i: (i, 0))]
    out_shape = [jax.ShapeDtypeStruct((t, D), F32), jax.ShapeDtypeStruct((t, D), BF16)]
    if router is not None:
        in_specs += [pl.BlockSpec((D, LANES), lambda i: (0, 0))] * 2 + [pl.BlockSpec((1, LANES), lambda i: (0, 0))]
        args += list(router)
        out_specs.append(pl.BlockSpec((tm, LANES), lambda i: (i, 0)))
        out_shape.append(jax.ShapeDtypeStruct((t, LANES), F32))
    return pl.pallas_call(
        functools.partial(_out_kernel, n_in=n_in, router=router is not None),
        grid=(t // tm,),
        in_specs=in_specs,
        out_specs=out_specs,
        out_shape=out_shape,
        compiler_params=_cp("parallel"),
        name="out_proj",
    )(*args)


def _ffn_kernel(*refs, gated, chunks_per_expert):
    u_ref, w1_ref, w3_ref, w2_ref = refs[:4]
    pos = 4
    if gated:
        gate_ref = refs[pos]
        pos += 1
    x1_ref, g2_ref, lng_ref, lnb_ref, o_ref, acc_scr = refs[pos:pos + 6]
    j = pl.program_id(1)

    @pl.when(j == 0)
    def _():
        acc_scr[...] = jnp.zeros(acc_scr.shape, F32)

    u = u_ref[...]
    h1 = _dot(u, w1_ref[...])
    h = _silu(h1) * _dot(u, w3_ref[...])
    if gated:
        gate = gate_ref[...]
        lane = lax.broadcasted_iota(jnp.int32, gate.shape, 1)
        h = h * jnp.sum(jnp.where(lane == j // chunks_per_expert, gate, 0.0), axis=1, keepdims=True)
    acc_scr[...] += _dot(h.astype(BF16), w2_ref[...])

    @pl.when(j == pl.num_programs(1) - 1)
    def _():
        o_ref[...] = _layer_norm(ALPHA * x1_ref[...] + g2_ref[...] * acc_scr[...], lng_ref[...], lnb_ref[...])


def _ffn(u2, w1, w3, w2, x1, mods3, row_fn, ln_g, ln_b, tf, gate=None):
    t = u2.shape[0]
    tm = 512
    n_e, _, f = w1.shape
    cpe = f // tf
    in_specs = [pl.BlockSpec((tm, D), lambda i, j: (i, 0)),
                pl.BlockSpec((None, D, tf), lambda i, j: (j // cpe, 0, j % cpe)),
                pl.BlockSpec((None, D, tf), lambda i, j: (j // cpe, 0, j % cpe)),
                pl.BlockSpec((None, tf, D), lambda i, j: (j // cpe, j % cpe, 0))]
    args = [u2, w1, w3, w2]
    if gate is not None:
        in_specs.append(pl.BlockSpec((tm, LANES), lambda i, j: (i, 0)))
        args.append(gate)
    in_specs += [pl.BlockSpec((tm, D), lambda i, j: (i, 0)),
                 pl.BlockSpec((None, 1, D), lambda i, j: (row_fn(i, tm), 0, 5)),
                 pl.BlockSpec((1, D), lambda i, j: (0, 0)),
                 pl.BlockSpec((1, D), lambda i, j: (0, 0))]
    args += [x1, mods3, ln_g.reshape(1, D), ln_b.reshape(1, D)]
    return pl.pallas_call(
        functools.partial(_ffn_kernel, gated=gate is not None, chunks_per_expert=cpe),
        grid=(t // tm, n_e * cpe),
        in_specs=in_specs,
        out_specs=pl.BlockSpec((tm, D), lambda i, j: (i, 0)),
        out_shape=jax.ShapeDtypeStruct((t, D), F32),
        scratch_shapes=[pltpu.VMEM((tm, D), F32)],
        compiler_params=_cp("parallel", "arbitrary"),
        name="ffn",
    )(*args)


def _conv_kernel(x_ref, w_ref, b_ref, o_ref, *, transpose):
    x = x_ref[...].astype(F32)
    s_len = x.shape[0]
    t = lax.broadcasted_iota(jnp.int32, x.shape, 0)
    prev = jnp.where(t == 0, 0.0, pltpu.roll(x, 1, axis=0))
    nxt = jnp.where(t == s_len - 1, 0.0, pltpu.roll(x, s_len - 1, axis=0))
    y = _silu(w_ref[0:1, :] * prev + w_ref[1:2, :] * x + w_ref[2:3, :] * nxt + b_ref[...])
    o_ref[...] = (y.T if transpose else y).astype(o_ref.dtype)


def _conv(pm, conv_w, conv_b, bsz, s_len, col_off, transpose):
    t = pm.shape[0]
    tc = 512
    n_ch = conv_w.shape[1]
    off = col_off // tc
    if transpose:
        out_spec = pl.BlockSpec((None, tc, s_len), lambda b, c: (b, c, 0))
        out_shape = jax.ShapeDtypeStruct((bsz, n_ch, s_len), BF16)
    else:
        out_spec = pl.BlockSpec((s_len, tc), lambda b, c: (b, c))
        out_shape = jax.ShapeDtypeStruct((t, n_ch), BF16)
    return pl.pallas_call(
        functools.partial(_conv_kernel, transpose=transpose),
        grid=(bsz, n_ch // tc),
        in_specs=[pl.BlockSpec((s_len, tc), lambda b, c: (b, off + c)),
                  pl.BlockSpec((3, tc), lambda b, c: (0, c)),
                  pl.BlockSpec((1, tc), lambda b, c: (0, c))],
        out_specs=out_spec,
        out_shape=out_shape,
        compiler_params=_cp("parallel", "parallel"),
        name="ssd_conv",
    )(pm, conv_w, conv_b.reshape(1, n_ch))


def _ssd_kernel(*refs, nc, has_init, out_state):
    xt_ref, b_ref, c_ref, dt_ref, dtn_ref, z_ref, dtb_ref, alog_ref, dsk_ref, ng_ref = refs[:10]
    pos = 10
    if has_init:
        s0_ref = refs[pos]
        pos += 1
    yn_ref = refs[pos]
    pos += 1
    if out_state:
        sst_ref = refs[pos]
        pos += 1
    s_scr, yb_scr, yt_scr, xw_scr, cum_scr, rows_scr = refs[pos:pos + 6]

    j = pl.program_id(1)
    is_fwd = j >= nc
    ci = jnp.where(is_fwd, j - nc, nc - 1 - j)
    sgn = jnp.where(is_fwd, 1, -1)

    def _init(d):
        if has_init:
            s_scr[...] = s0_ref[d]
        else:
            s_scr[...] = jnp.zeros(s_scr.shape, F32)

    pl.when(j == 0)(lambda: _init(1))
    pl.when(j == nc)(lambda: _init(0))

    def _prep(dtraw_ref, sg, slot):
        row = lax.broadcasted_iota(jnp.int32, (CH, CH), 0)
        col = lax.broadcasted_iota(jnp.int32, (CH, CH), 1)
        tri = jnp.where((col - row) * sg <= 0, 1.0, 0.0).astype(BF16)
        f_row = (jnp.zeros((1, LANES), jnp.int32) + sg) > 0
        f_mat = (jnp.zeros((CH, LANES), jnp.int32) + sg) > 0
        f_hc = (jnp.zeros((H_C, LANES), jnp.int32) + sg) > 0
        raw = dtraw_ref[...] + dtb_ref[...]
        raw = jnp.where(f_mat, raw, pltpu.roll(raw, LANES - H_C, axis=1))
        alog = alog_ref[...]
        alog = jnp.where(f_row, alog, pltpu.roll(alog, LANES - H_C, axis=1))
        dtv = _softplus(raw)
        cumv = _tri_dot(tri, dtv * (-jnp.exp(alog)))
        cumv_t = cumv.T[0:H_C, :]
        dtv_t = dtv.T[0:H_C, :]
        last = jnp.where(f_hc, jnp.broadcast_to(cumv_t[:, CH - 1:CH], (H_C, LANES)),
                         jnp.broadcast_to(cumv_t[:, 0:1], (H_C, LANES)))
        cum_scr[slot] = cumv
        rows_scr[slot, 0] = cumv_t
        rows_scr[slot, 1] = dtv_t
        rows_scr[slot, 2] = jnp.exp(cumv_t)
        rows_scr[slot, 3] = jnp.exp(last - cumv_t) * dtv_t
        rows_scr[slot, 4] = jnp.exp(last)

    pl.when(j == 0)(lambda: _prep(dt_ref, sgn, 0))

    slot = j % 2
    row = lax.broadcasted_iota(jnp.int32, (CH, CH), 0)
    col = lax.broadcasted_iota(jnp.int32, (CH, CH), 1)
    causal_t = (row - col) * sgn <= 0
    cum = cum_scr[slot]
    cum_t = rows_scr[slot, 0]
    dt_t = rows_scr[slot, 1]
    ecum_t = rows_scr[slot, 2]
    ws_t = rows_scr[slot, 3]
    dec = rows_scr[slot, 4]
    _prep(dtn_ref, jnp.where(j + 1 >= nc, 1, -1), (j + 1) % 2)

    for gi in range(NG_C):
        gsl = slice(gi * D_STATE, (gi + 1) * D_STATE)
        bg = b_ref[:, gsl]
        cg = c_ref[:, gsl]
        cb_t = _dot_nt(bg, cg)
        ys_t = _dot_nt(s_scr[gi].astype(BF16), cg)
        for rr in range(HPG_C):
            r = gi * HPG_C + rr
            rsl = slice(r * HD, (r + 1) * HD)
            seg_t = jnp.where(causal_t, cum_t[r:r + 1, :] - cum[:, r:r + 1], NEG)
            w_t = (cb_t * jnp.exp(seg_t)).astype(BF16)
            xr_t = xt_ref[rsl, :].astype(F32)
            xd_t = (xr_t * dt_t[r:r + 1, :]).astype(BF16)
            yt_scr[rsl, :] = _dot(xd_t, w_t) + ys_t[rr * HD:(rr + 1) * HD, :] * ecum_t[r:r + 1, :]
            xw_scr[rr * HD:(rr + 1) * HD, :] = (xr_t * ws_t[r:r + 1, :]).astype(BF16)
        upd = _dot(xw_scr[...], bg)
        for rr in range(HPG_C):
            r = gi * HPG_C + rr
            hsl = slice(rr * HD, (rr + 1) * HD)
            s_scr[gi, hsl, :] = s_scr[gi, hsl, :] * dec[r:r + 1, :] + upd[hsl, :]

    @pl.when(j < nc)
    def _():
        yb_scr[ci] = yt_scr[...]

    @pl.when(j >= nc)
    def _():
        yt = yt_scr[...] + yb_scr[ci] + dsk_ref[...] * xt_ref[...].astype(F32)
        y = yt.T * _silu(z_ref[...].astype(F32))
        outs = []
        gw = D_INNER // NG_C
        for gi in range(NG_C):
            yg = y[:, gi * gw:(gi + 1) * gw]
            outs.append(yg * lax.rsqrt(jnp.mean(yg * yg, axis=1, keepdims=True) + EPS))
        yn_ref[...] = (jnp.concatenate(outs, axis=1) * ng_ref[...]).astype(yn_ref.dtype)

    if out_state:
        @pl.when(j == nc - 1)
        def _():
            sst_ref[1] = s_scr[...]

        @pl.when(j == 2 * nc - 1)
        def _():
            sst_ref[0] = s_scr[...]


def _ssd(xt, bc, pa, pm, dtb, alog, dsk, norm_g, bsz, s_len, init=None, out_state=False):
    nc = s_len // CH
    t = bsz * s_len
    gw = NG_C * D_STATE

    def chunk_i(j):
        return jnp.where(j >= nc, j - nc, nc - 1 - j)

    def chunk(b, j):
        return b * nc + chunk_i(j)

    def chunk_next(b, j):
        return chunk(b, jnp.minimum(j + 1, 2 * nc - 1))

    def chunk_fwd(b, j):
        return b * nc + jnp.maximum(j - nc, 0)

    in_specs = [pl.BlockSpec((None, D_INNER, CH), lambda b, j: (b, 0, chunk_i(j))),
                pl.BlockSpec((CH, gw), lambda b, j: (chunk(b, j), 0)),
                pl.BlockSpec((CH, gw), lambda b, j: (chunk(b, j), 1)),
                pl.BlockSpec((CH, LANES), lambda b, j: (chunk(b, j), 0)),
                pl.BlockSpec((CH, LANES), lambda b, j: (chunk_next(b, j), 0)),
                pl.BlockSpec((CH, D_INNER), lambda b, j: (chunk_fwd(b, j), 0)),
                pl.BlockSpec((1, LANES), lambda b, j: (0, 0)),
                pl.BlockSpec((1, LANES), lambda b, j: (0, 0)),
                pl.BlockSpec((D_INNER, LANES), lambda b, j: (0, 0)),
                pl.BlockSpec((1, D_INNER), lambda b, j: (0, 0))]
    args = [xt, bc, bc, pa, pa, pm, dtb, alog, dsk, norm_g]
    state_spec = pl.BlockSpec((None, 2, NG_C, HPG_C * HD, D_STATE), lambda b, j: (b, 0, 0, 0, 0))
    if init is not None:
        in_specs.append(state_spec)
        args.append(init)
    out_specs = [pl.BlockSpec((CH, D_INNER), lambda b, j: (chunk_fwd(b, j), 0))]
    out_shape = [jax.ShapeDtypeStruct((t, D_INNER), BF16)]
    if out_state:
        out_specs.append(state_spec)
        out_shape.append(jax.ShapeDtypeStruct((bsz, 2, NG_C, HPG_C * HD, D_STATE), F32))
    return pl.pallas_call(
        functools.partial(_ssd_kernel, nc=nc, has_init=init is not None, out_state=out_state),
        grid=(bsz, 2 * nc),
        in_specs=in_specs,
        out_specs=out_specs,
        out_shape=out_shape,
        scratch_shapes=[pltpu.VMEM((NG_C, HPG_C * HD, D_STATE), F32),
                        pltpu.VMEM((nc, D_INNER, CH), F32),
                        pltpu.VMEM((D_INNER, CH), F32),
                        pltpu.VMEM((HPG_C * HD, CH), BF16),
                        pltpu.VMEM((2, CH, LANES), F32), pltpu.VMEM((2, 5, H_C, LANES), F32)],
        compiler_params=_cp("parallel", "arbitrary"),
        name="ssd_scan",
    )(*args)


def _rope_tables(s_len):
    nf = HD // 4
    inv = ROPE_THETA ** (-jnp.arange(nf, dtype=F32) / nf)
    tpos = jnp.arange(s_len)
    ang_r = (tpos // GRID_W).astype(F32)[:, None] * inv
    ang_c = (tpos % GRID_W).astype(F32)[:, None] * inv
    cos = jnp.concatenate([jnp.cos(ang_r)] * 2 + [jnp.cos(ang_c)] * 2, axis=1)
    sin = jnp.concatenate([-jnp.sin(ang_r), jnp.sin(ang_r), -jnp.sin(ang_c), jnp.sin(ang_c)], axis=1)
    return jnp.tile(cos, (1, 2)), jnp.tile(sin, (1, 2))


def _block_avg(n):
    idx = np.arange(n) // HD
    return jnp.asarray((idx[:, None] == idx[None, :]).astype(np.float32) / HD, dtype=BF16)


def _pad_cols(w, n):
    return jnp.pad(w, ((0, 0), (0, n - w.shape[1])))


def _row_prompt(i, tm):
    return 0


def _make_row_sample(s_len):
    def row(i, tm):
        return 1 + (i * tm) // s_len
    return row


def kernel(x_prompt, x_sample, cache_k_l0, cache_v_l0, state_mlstm_c_l0, state_mlstm_n_l0, state_mlstm_m_l0, state_ssd_l1, c, c_ctx, ada_w_l0, ada_b_l0, mix_w_in_l0, mlstm_gate_b_l0, mlstm_norm_g_l0, q_norm_g_l0, k_norm_g_l0, mix_w_out_l0, ln1_g_l0, ln1_b_l0, ln2_g_l0, ln2_b_l0, ffn_w1_l0, ffn_w3_l0, ffn_w2_l0, ada_w_l1, ada_b_l1, ssd_w_in_l1, ssd_conv_w_l1, ssd_conv_b_l1, ssd_dt_bias_l1, ssd_a_log_l1, ssd_d_l1, ssd_norm_g_l1, ssd_w_out_l1, ln1_g_l1, ln1_b_l1, ln2_g_l1, ln2_b_l1, router_w_l1, router_b_l1, moe_w1_l1, moe_w3_l1, moe_w2_l1):
    bp, sp, _ = x_prompt.shape
    bs, ss, _ = x_sample.shape
    past = cache_k_l0.shape[1]
    groups = [
        dict(x=x_prompt.reshape(bp * sp, D), b=bp, s=sp, row=_row_prompt, prompt=True),
        dict(x=x_sample.reshape(bs * ss, D), b=bs, s=ss, row=_make_row_sample(ss), prompt=False),
    ]

    cvec = jnp.zeros((16, D), F32).at[0].set(c_ctx).at[1:1 + bs].set(c)
    mods0 = _ada(cvec, ada_w_l0, ada_b_l0).reshape(16, 1, 6 * D)
    mods1 = _ada(cvec, ada_w_l1, ada_b_l1).reshape(16, 1, 6 * D)

    w0_main = jnp.concatenate([mix_w_in_l0[:, :2048], mix_w_in_l0[:, 2080:]], axis=1).astype(BF16)
    w0_aux_hi, w0_aux_lo = _split2(_pad_cols(mix_w_in_l0[:, 2048:2080], LANES))
    gbias = _pad_cols(mlstm_gate_b_l0.reshape(1, 4 * H_A), LANES)
    avg128 = _block_avg(LANES)
    q_g = jnp.tile(q_norm_g_l0, W_A // HD).reshape(1, W_A)
    k_g = jnp.tile(k_norm_g_l0, KV_B).reshape(1, LANES)
    w_out_h = mix_w_out_l0[:W_A].astype(BF16)
    w_out_o = mix_w_out_l0[W_A:].astype(BF16)
    ffn_w1 = ffn_w1_l0.astype(BF16)[None]
    ffn_w3 = ffn_w3_l0.astype(BF16)[None]
    ffn_w2 = ffn_w2_l0.astype(BF16)[None]
    tables = _rope_tables(ss)
    kc = cache_k_l0.transpose(2, 0, 1, 3).reshape(KV_B, bs * past, HD).astype(BF16)
    vc = cache_v_l0.transpose(2, 0, 1, 3).reshape(KV_B, bs * past, HD)
    vc = jnp.concatenate([vc, jnp.ones_like(vc)], axis=-1).astype(BF16)
    c0 = jnp.concatenate([jnp.swapaxes(state_mlstm_c_l0, -1, -2),
                          jnp.broadcast_to(state_mlstm_n_l0[..., None, :], state_mlstm_c_l0.shape)], axis=-2)
    m0 = jnp.broadcast_to(state_mlstm_m_l0[..., None], state_mlstm_m_l0.shape + (LANES,))

    n_main1 = 2 * D_INNER + 2 * NG_C * D_STATE
    w1_main = ssd_w_in_l1[:, :n_main1].astype(BF16)
    w1_aux_hi, w1_aux_lo = _split2(_pad_cols(ssd_w_in_l1[:, n_main1:], LANES))
    dtb = _pad_cols(ssd_dt_bias_l1.reshape(1, 2 * H_C), LANES)
    alog = _pad_cols(ssd_a_log_l1.reshape(1, 2 * H_C), LANES)
    dsk = jnp.broadcast_to(jnp.repeat(ssd_d_l1, HD)[:, None], (D_INNER, LANES))
    ssd_ng = ssd_norm_g_l1.reshape(1, D_INNER)
    w_out1 = ssd_w_out_l1.astype(BF16)
    rw_hi, rw_lo = _split2(_pad_cols(router_w_l1, LANES))
    rb = jnp.full((1, LANES), NEG, F32).at[0, :N_EXPERTS].set(router_b_l1)
    moe_w1 = moe_w1_l1.astype(BF16)
    moe_w3 = moe_w3_l1.astype(BF16)
    moe_w2 = moe_w2_l1.astype(BF16)
    s0 = state_ssd_l1.reshape(bs, 2, NG_C, HPG_C * HD, D_STATE)

    outs = {}
    for grp in groups:
        x, b, s, row, prompt = grp["x"], grp["b"], grp["s"], grp["row"], grp["prompt"]
        pm, pa = _mm_in(x, mods0, row, w0_main, w0_aux_hi, w0_aux_lo, tm=512, tn=w0_main.shape[1])
        if prompt:
            hn, cst, mst = _mlstm(pm, pa, gbias, mlstm_norm_g_l0.reshape(1, W_A), b, s, out_state=True)
            qn, kh, vh, kn = _qkprep(pm, q_g, k_g, avg128, s, emit_kn=True)
            o = _attn(qn, kh, vh, b, s)
            outs["k"] = kn.reshape(b, s, KV_B, HD)
            outs["v"] = pm[:, 2688:2816].astype(F32).reshape(b, s, KV_B, HD)
            outs["c"] = jnp.swapaxes(cst[..., :HD, :], -1, -2)
            outs["n"] = cst[..., HD, :]
            outs["m"] = mst[..., 0]
        else:
            hn = _mlstm(pm, pa, gbias, mlstm_norm_g_l0.reshape(1, W_A), b, s, init=(c0, m0))[0]
            qn, kh, vh = _qkprep(pm, q_g, k_g, avg128, s, tables=tables)
            o = _attn(qn, kh, vh, b, s, ctx=(kc, vc))
        x1, u2 = _out_proj([hn, o], [w_out_h, w_out_o], x, mods0, row, ln1_g_l0, ln1_b_l0)
        x2 = _ffn(u2, ffn_w1, ffn_w3, ffn_w2, x1, mods0, row, ln2_g_l0, ln2_b_l0, tf=D_FF // 2)
        pm, pa = _mm_in(x2, mods1, row, w1_main, w1_aux_hi, w1_aux_lo, tm=512, tn=1280)
        xt = _conv(pm, ssd_conv_w_l1[:, :D_INNER], ssd_conv_b_l1[:D_INNER], b, s, D_INNER, True)
        bc = _conv(pm, ssd_conv_w_l1[:, D_INNER:], ssd_conv_b_l1[D_INNER:], b, s, 2 * D_INNER, False)
        if prompt:
            yn, sst = _ssd(xt, bc, pa, pm, dtb, alog, dsk, ssd_ng, b, s, out_state=True)
            outs["s"] = sst.reshape(b, 2, H_C, HD, D_STATE)
        else:
            yn = _ssd(xt, bc, pa, pm, dtb, alog, dsk, ssd_ng, b, s, init=s0)[0]
        x1, u2, gate = _out_proj([yn], [w_out1], x2, mods1, row, ln1_g_l1, ln1_b_l1, router=(rw_hi, rw_lo, rb))
        x3 = _ffn(u2, moe_w1, moe_w3, moe_w2, x1, mods1, row, ln2_g_l1, ln2_b_l1, tf=D_FF_E, gate=gate)
        outs["y_p" if prompt else "y_s"] = x3.reshape(b, s, D)

    return (outs["y_p"], outs["y_s"], outs["k"], outs["v"], outs["c"], outs["n"], outs["m"], outs["s"])
```

```python
import functools

import jax
import jax.numpy as jnp
import numpy as np
from jax import lax
from jax.experimental import pallas as pl
from jax.experimental.pallas import tpu as pltpu

F32 = jnp.float32
BF16 = jnp.bfloat16

D = 1024
CH = 128
LANES = 128
EPS = 1e-6
DEPTH = 2
ALPHA = (2 * DEPTH) ** 0.25
GRID_W = 64
ROPE_THETA = 10000.0
H_A = 8
HD = 64
W_A = 512
KV_B = 2
H_C = 32
NG_C = 4
HPG_C = 8
D_STATE = 128
D_INNER = 2048
D_FF = 2816
N_EXPERTS = 8
D_FF_E = 1408
NEG = -1e30
VMEM_LIMIT = 56 * 1024 * 1024


def _cp(*sem):
    return pltpu.CompilerParams(dimension_semantics=sem, vmem_limit_bytes=VMEM_LIMIT)


def _sigmoid(x):
    return 1.0 / (1.0 + jnp.exp(-x))


def _silu(x):
    return x * _sigmoid(x)


def _softplus(x):
    return jnp.maximum(x, 0.0) + jnp.log(1.0 + jnp.exp(-jnp.abs(x)))


def _split2(x):
    hi = x.astype(BF16)
    lo = (x - hi.astype(F32)).astype(BF16)
    return hi, lo


def _split3(x):
    h1 = x.astype(BF16)
    r = x - h1.astype(F32)
    h2 = r.astype(BF16)
    h3 = (r - h2.astype(F32)).astype(BF16)
    return h1, h2, h3


def _dot(a, b):
    return jnp.dot(a, b, preferred_element_type=F32)


def _dot_nt(a, b):
    return lax.dot_general(a, b, (((1,), (1,)), ((), ())), preferred_element_type=F32)


def _tri_dot(tri, x):
    h1, h2, h3 = _split3(x)
    return _dot(tri, h1) + _dot(tri, h2) + _dot(tri, h3)


def _layer_norm(r, g, b):
    mu = jnp.mean(r, axis=-1, keepdims=True)
    d = r - mu
    var = jnp.mean(d * d, axis=-1, keepdims=True)
    return d * lax.rsqrt(var + EPS) * g + b


def _ada_kernel(c_ref, w_ref, b_ref, o_ref):
    s = _silu(c_ref[...])
    sh, sl = _split2(s)
    wh, wl = _split2(w_ref[...])
    o_ref[...] = _dot(sh, wh) + _dot(sl, wh) + _dot(sh, wl) + b_ref[...]


def _ada(cvec, w, b):
    n = w.shape[1]
    tn = 768
    return pl.pallas_call(
        _ada_kernel,
        grid=(n // tn,),
        in_specs=[pl.BlockSpec((16, D), lambda j: (0, 0)),
                  pl.BlockSpec((D, tn), lambda j: (0, j)),
                  pl.BlockSpec((1, tn), lambda j: (0, j))],
        out_specs=pl.BlockSpec((16, tn), lambda j: (0, j)),
        out_shape=jax.ShapeDtypeStruct((16, n), F32),
        compiler_params=_cp("parallel"),
        name="ada_mods",
    )(cvec, w, b.reshape(1, n))


def _mm_in_kernel(x_ref, sh_ref, sc_ref, wm_ref, wah_ref, wal_ref, om_ref, oa_ref, *, n_chunks):
    u = x_ref[...] * (1.0 + sc_ref[...]) + sh_ref[...]
    uh, ul = _split2(u)
    wah = wah_ref[...]
    oa_ref[...] = _dot(uh, wah) + _dot(ul, wah) + _dot(uh, wal_ref[...])
    tn = wm_ref.shape[1] // n_chunks
    for c in range(n_chunks):
        om_ref[:, c * tn:(c + 1) * tn] = _dot(uh, wm_ref[:, c * tn:(c + 1) * tn]).astype(om_ref.dtype)


def _mm_in(x, mods3, row_fn, wm, wa_hi, wa_lo, tm, n_chunks):
    t, nm = x.shape[0], wm.shape[1]
    once = pl.Buffered(1)
    return pl.pallas_call(
        functools.partial(_mm_in_kernel, n_chunks=n_chunks),
        grid=(t // tm,),
        in_specs=[pl.BlockSpec((tm, D), lambda i: (i, 0)),
                  pl.BlockSpec((None, 1, D), lambda i: (row_fn(i, tm), 0, 0)),
                  pl.BlockSpec((None, 1, D), lambda i: (row_fn(i, tm), 0, 1)),
                  pl.BlockSpec((D, nm), lambda i: (0, 0), pipeline_mode=once),
                  pl.BlockSpec((D, LANES), lambda i: (0, 0), pipeline_mode=once),
                  pl.BlockSpec((D, LANES), lambda i: (0, 0), pipeline_mode=once)],
        out_specs=[pl.BlockSpec((tm, nm), lambda i: (i, 0)),
                   pl.BlockSpec((tm, LANES), lambda i: (i, 0))],
        out_shape=[jax.ShapeDtypeStruct((t, nm), BF16), jax.ShapeDtypeStruct((t, LANES), F32)],
        compiler_params=_cp("parallel"),
        name="mm_in",
    )(x, mods3, mods3, wm, wa_hi, wa_lo)


def _mlstm_kernel(*refs, nc, has_init, out_state):
    q_ref, k_ref, v_ref, om_ref, g_ref, gn_ref, gb_ref, ng_ref = refs[:8]
    pos = 8
    if has_init:
        c0_ref, m0_ref = refs[pos:pos + 2]
        pos += 2
    hn_ref = refs[pos]
    pos += 1
    if out_state:
        cst_ref, mst_ref = refs[pos:pos + 2]
        pos += 2
    c_scr, m_scr, hb_scr, r_scr, rows_scr = refs[pos:pos + 5]

    j = pl.program_id(1)
    is_fwd = j >= nc
    ci = jnp.where(is_fwd, j - nc, nc - 1 - j)
    sgn = jnp.where(is_fwd, 1, -1)

    def _init(d):
        if has_init:
            c_scr[...] = c0_ref[d]
            m_scr[...] = m0_ref[d]
        else:
            c_scr[...] = jnp.zeros(c_scr.shape, F32)
            m_scr[...] = jnp.zeros(m_scr.shape, F32)

    pl.when(j == 0)(lambda: _init(1))
    pl.when(j == nc)(lambda: _init(0))

    def _prep(graw_ref, sg, slot):
        row = lax.broadcasted_iota(jnp.int32, (CH, CH), 0)
        col = lax.broadcasted_iota(jnp.int32, (CH, CH), 1)
        tri = jnp.where((col - row) * sg <= 0, 1.0, 0.0).astype(BF16)
        f_mat = (jnp.zeros((CH, LANES), jnp.int32) + sg) > 0
        f8 = (jnp.zeros((H_A, LANES), jnp.int32) + sg) > 0
        lane8 = lax.broadcasted_iota(jnp.int32, (H_A, LANES), 1)
        g = graw_ref[...] + gb_ref[...]
        li = jnp.where(f_mat, g, pltpu.roll(g, LANES - 16, axis=1))
        lfr = jnp.where(f_mat, pltpu.roll(g, LANES - 8, axis=1), pltpu.roll(g, LANES - 24, axis=1))
        lf = jnp.minimum(lfr, 0.0) - jnp.log(1.0 + jnp.exp(-jnp.abs(lfr)))
        bc = _tri_dot(tri, lf)
        rr = li - bc
        rr_t = rr.T[0:H_A, :]
        cmx = rr_t
        for dlt in (1, 2, 4, 8, 16, 32, 64):
            before = jnp.where(lane8 >= dlt, pltpu.roll(cmx, dlt, axis=1), NEG)
            after = jnp.where(lane8 < LANES - dlt, pltpu.roll(cmx, LANES - dlt, axis=1), NEG)
            cmx = jnp.maximum(cmx, jnp.where(f8, before, after))
        r_scr[slot] = rr
        rows_scr[slot, 0] = rr_t
        rows_scr[slot, 1] = bc.T[0:H_A, :]
        rows_scr[slot, 2] = cmx

    pl.when(j == 0)(lambda: _prep(g_ref, sgn, 0))

    slot = j % 2
    row = lax.broadcasted_iota(jnp.int32, (CH, CH), 0)
    col = lax.broadcasted_iota(jnp.int32, (CH, CH), 1)
    causal_t = (row - col) * sgn <= 0
    fwd8 = (jnp.zeros((H_A, LANES), jnp.int32) + sgn) > 0
    r = r_scr[slot]
    r_t = rows_scr[slot, 0]
    b_t = rows_scr[slot, 1]
    cm = rows_scr[slot, 2]
    _prep(gn_ref, jnp.where(j + 1 >= nc, 1, -1), (j + 1) % 2)

    m_prev = m_scr[...]
    g_t = jnp.maximum(m_prev, cm)
    mt_t = b_t + g_t
    wst_t = jnp.exp(m_prev - g_t)
    emt_t = jnp.exp(-mt_t)
    fwd_c = fwd8[:, 0:1]
    m_new = jnp.where(fwd_c, mt_t[:, LANES - 1:LANES], mt_t[:, 0:1])
    b_last = jnp.where(fwd_c, b_t[:, LANES - 1:LANES], b_t[:, 0:1])
    ws_t = jnp.exp(r_t + (b_last - m_new))
    decay = jnp.exp(b_last + m_prev[:, 0:1] - m_new)
    m_scr[...] = jnp.broadcast_to(m_new, (H_A, LANES))

    q_t = q_ref[...].astype(F32).T.astype(BF16)
    v_t = v_ref[...].astype(F32).T
    ones = jnp.ones((HD, CH), F32)

    hts = []
    for h in range(H_A):
        sl = slice(h * HD, (h + 1) * HD)
        w_t = jnp.exp(jnp.where(causal_t, r[:, h:h + 1] - g_t[h:h + 1, :], NEG))
        kh = k_ref[:, sl]
        qh_t = q_t[sl, :]
        vext_t = jnp.concatenate([v_t[sl, :], ones], axis=0)
        sc_t = (_dot(kh, qh_t) * 0.125 * w_t).astype(BF16)
        cext_t = c_scr[h]
        nd_t = _dot(vext_t.astype(BF16), sc_t) + (_dot(cext_t.astype(BF16), qh_t) * 0.125) * wst_t[h:h + 1, :]
        den = jnp.maximum(jnp.abs(nd_t[HD:HD + 1, :]), emt_t[h:h + 1, :])
        hts.append(nd_t[0:HD, :] / den)
        c_scr[h] = decay[h:h + 1, :] * cext_t + _dot((vext_t * ws_t[h:h + 1, :]).astype(BF16), kh)
    hcat_t = jnp.concatenate(hts, axis=0)

    @pl.when(j < nc)
    def _():
        hb_scr[ci] = hcat_t

    @pl.when(j >= nc)
    def _():
        ht = hcat_t + hb_scr[ci]
        normed = []
        for h in range(H_A):
            hh = ht[h * HD:(h + 1) * HD, :]
            dlt = hh - jnp.mean(hh, axis=0, keepdims=True)
            normed.append(dlt * lax.rsqrt(jnp.mean(dlt * dlt, axis=0, keepdims=True) + EPS))
        hn = jnp.concatenate(normed, axis=0).T * ng_ref[...] * _sigmoid(om_ref[...].astype(F32))
        hn_ref[...] = hn.astype(hn_ref.dtype)

    if out_state:
        @pl.when(j == nc - 1)
        def _():
            cst_ref[1] = c_scr[...]
            mst_ref[1] = m_scr[...]

        @pl.when(j == 2 * nc - 1)
        def _():
            cst_ref[0] = c_scr[...]
            mst_ref[0] = m_scr[...]


def _mlstm(pm, pa, gbias, norm_g, bsz, s_len, init=None, out_state=False):
    nc = s_len // CH
    t = bsz * s_len

    def chunk(b, j):
        return b * nc + jnp.where(j >= nc, j - nc, nc - 1 - j)

    def chunk_next(b, j):
        return chunk(b, jnp.minimum(j + 1, 2 * nc - 1))

    def chunk_fwd(b, j):
        return b * nc + jnp.maximum(j - nc, 0)

    in_specs = [pl.BlockSpec((CH, W_A), lambda b, j: (chunk(b, j), 0)),
                pl.BlockSpec((CH, W_A), lambda b, j: (chunk(b, j), 1)),
                pl.BlockSpec((CH, W_A), lambda b, j: (chunk(b, j), 2)),
                pl.BlockSpec((CH, W_A), lambda b, j: (chunk_fwd(b, j), 3)),
                pl.BlockSpec((CH, LANES), lambda b, j: (chunk(b, j), 0)),
                pl.BlockSpec((CH, LANES), lambda b, j: (chunk_next(b, j), 0)),
                pl.BlockSpec((1, LANES), lambda b, j: (0, 0)),
                pl.BlockSpec((1, W_A), lambda b, j: (0, 0))]
    args = [pm, pm, pm, pm, pa, pa, gbias, norm_g]
    if init is not None:
        in_specs += [pl.BlockSpec((None, 2, H_A, LANES, HD), lambda b, j: (b, 0, 0, 0, 0)),
                     pl.BlockSpec((None, 2, H_A, LANES), lambda b, j: (b, 0, 0, 0))]
        args += list(init)
    out_specs = [pl.BlockSpec((CH, W_A), lambda b, j: (chunk_fwd(b, j), 0))]
    out_shape = [jax.ShapeDtypeStruct((t, W_A), BF16)]
    if out_state:
        out_specs += [pl.BlockSpec((None, 2, H_A, LANES, HD), lambda b, j: (b, 0, 0, 0, 0)),
                      pl.BlockSpec((None, 2, H_A, LANES), lambda b, j: (b, 0, 0, 0))]
        out_shape += [jax.ShapeDtypeStruct((bsz, 2, H_A, LANES, HD), F32),
                      jax.ShapeDtypeStruct((bsz, 2, H_A, LANES), F32)]
    return pl.pallas_call(
        functools.partial(_mlstm_kernel, nc=nc, has_init=init is not None, out_state=out_state),
        grid=(bsz, 2 * nc),
        in_specs=in_specs,
        out_specs=out_specs,
        out_shape=out_shape,
        scratch_shapes=[pltpu.VMEM((H_A, LANES, HD), F32), pltpu.VMEM((H_A, LANES), F32),
                        pltpu.VMEM((nc, W_A, CH), F32),
                        pltpu.VMEM((2, CH, LANES), F32), pltpu.VMEM((2, 3, H_A, LANES), F32)],
        compiler_params=_cp("parallel", "arbitrary"),
        name="mlstm",
    )(*args)


def _group_rms(x, avg, g):
    s1, s2 = _split2(x * x)
    ms = _dot(s1, avg) + _dot(s2, avg)
    return x * lax.rsqrt(ms + EPS) * g


def _rope(x, cos, sin):
    lane = lax.broadcasted_iota(jnp.int32, x.shape, 1)
    swapped = jnp.where((lane % 32) < 16, pltpu.roll(x, LANES - 16, axis=1), pltpu.roll(x, 16, axis=1))
    return x * cos + swapped * sin


def _qkprep_kernel(*refs, rope, emit_kn):
    q_ref, k_ref, v_ref, qg_ref, kg_ref, avg_ref = refs[:6]
    pos = 6
    if rope:
        cos_ref, sin_ref = refs[pos:pos + 2]
        pos += 2
    qn_ref, kh_ref, vh_ref = refs[pos:pos + 3]
    pos += 3
    avg = avg_ref[...]
    qs = []
    for c in range(W_A // LANES):
        sl = slice(c * LANES, (c + 1) * LANES)
        xn = _group_rms(q_ref[:, sl].astype(F32), avg, qg_ref[:, sl])
        if rope:
            xn = _rope(xn, cos_ref[...], sin_ref[...])
        qs.append(xn * 0.125)
    qn_ref[...] = jnp.concatenate(qs, axis=1).astype(qn_ref.dtype)
    kn = _group_rms(k_ref[...].astype(F32), avg, kg_ref[...])
    if emit_kn:
        refs[pos][...] = kn
    if rope:
        kn = _rope(kn, cos_ref[...], sin_ref[...])
    v = v_ref[...].astype(F32)
    ones = jnp.ones((v.shape[0], HD), F32)
    for gi in range(KV_B):
        sl = slice(gi * HD, (gi + 1) * HD)
        kh_ref[gi] = kn[:, sl].astype(kh_ref.dtype)
        vh_ref[gi] = jnp.concatenate([v[:, sl], ones], axis=1).astype(vh_ref.dtype)


def _qkprep(pm, q_g, k_g, avg128, s_len, tables=None, emit_kn=False):
    t = pm.shape[0]
    tm = 256
    nt = s_len // tm
    qoff, koff, voff = 2048 // W_A, 2560 // LANES, 2688 // LANES
    in_specs = [pl.BlockSpec((tm, W_A), lambda i: (i, qoff)),
                pl.BlockSpec((tm, LANES), lambda i: (i, koff)),
                pl.BlockSpec((tm, LANES), lambda i: (i, voff)),
                pl.BlockSpec((1, W_A), lambda i: (0, 0)),
                pl.BlockSpec((1, LANES), lambda i: (0, 0)),
                pl.BlockSpec((LANES, LANES), lambda i: (0, 0))]
    args = [pm, pm, pm, q_g, k_g, avg128]
    if tables is not None:
        in_specs += [pl.BlockSpec((tm, LANES), lambda i: (i % nt, 0))] * 2
        args += list(tables)
    out_specs = [pl.BlockSpec((tm, W_A), lambda i: (i, 0)),
                 pl.BlockSpec((KV_B, tm, HD), lambda i: (0, i, 0)),
                 pl.BlockSpec((KV_B, tm, LANES), lambda i: (0, i, 0))]
    out_shape = [jax.ShapeDtypeStruct((t, W_A), BF16),
                 jax.ShapeDtypeStruct((KV_B, t, HD), BF16),
                 jax.ShapeDtypeStruct((KV_B, t, LANES), BF16)]
    if emit_kn:
        out_specs.append(pl.BlockSpec((tm, LANES), lambda i: (i, 0)))
        out_shape.append(jax.ShapeDtypeStruct((t, LANES), F32))
    return pl.pallas_call(
        functools.partial(_qkprep_kernel, rope=tables is not None, emit_kn=emit_kn),
        grid=(t // tm,),
        in_specs=in_specs,
        out_specs=out_specs,
        out_shape=out_shape,
        compiler_params=_cp("parallel"),
        name="qkprep",
    )(*args)


def _attn_kernel(*refs, has_ctx):
    if has_ctx:
        q_ref, k_ref, v_ref, kc_ref, vc_ref, o_ref = refs
    else:
        q_ref, k_ref, v_ref, o_ref = refs
    k = k_ref[...]
    v = v_ref[...]
    outs = []
    for hq in range(4):
        qh = q_ref[:, hq * HD:(hq + 1) * HD]
        s = _dot_nt(qh, k)
        m = jnp.max(s, axis=1, keepdims=True)
        if has_ctx:
            sc = _dot_nt(qh, kc_ref[...])
            m = jnp.maximum(m, jnp.max(sc, axis=1, keepdims=True))
        oe = _dot(jnp.exp(s - m).astype(BF16), v)
        if has_ctx:
            oe = oe + _dot(jnp.exp(sc - m).astype(BF16), vc_ref[...])
        outs.append(oe[:, :HD] / oe[:, HD:HD + 1])
    o_ref[...] = jnp.concatenate(outs, axis=1).astype(o_ref.dtype)


def _attn(qn, kh, vh, bsz, s_len, ctx=None):
    t = qn.shape[0]
    tq = min(256, s_len)
    nq = s_len // tq
    in_specs = [pl.BlockSpec((tq, 256), lambda b, g, i: (b * nq + i, g)),
                pl.BlockSpec((None, s_len, HD), lambda b, g, i: (g, b, 0)),
                pl.BlockSpec((None, s_len, LANES), lambda b, g, i: (g, b, 0))]
    args = [qn, kh, vh]
    if ctx is not None:
        past = ctx[0].shape[1] // bsz
        in_specs += [pl.BlockSpec((None, past, HD), lambda b, g, i: (g, b, 0)),
                     pl.BlockSpec((None, past, LANES), lambda b, g, i: (g, b, 0))]
        args += list(ctx)
    return pl.pallas_call(
        functools.partial(_attn_kernel, has_ctx=ctx is not None),
        grid=(bsz, KV_B, nq),
        in_specs=in_specs,
        out_specs=pl.BlockSpec((tq, 256), lambda b, g, i: (b * nq + i, g)),
        out_shape=jax.ShapeDtypeStruct((t, W_A), BF16),
        compiler_params=_cp("parallel", "parallel", "parallel"),
        name="attn",
    )(*args)


def _out_kernel(*refs, n_in, router):
    a_refs = refs[:n_in]
    w_refs = refs[n_in:2 * n_in]
    x_ref, g1_ref, sh2_ref, sc2_ref, lng_ref, lnb_ref = refs[2 * n_in:2 * n_in + 6]
    pos = 2 * n_in + 6
    if router:
        rwh_ref, rwl_ref, rb_ref = refs[pos:pos + 3]
        pos += 3
    x1_ref, u2_ref = refs[pos:pos + 2]
    y = _dot(a_refs[0][...], w_refs[0][...])
    for a_ref, w_ref in zip(a_refs[1:], w_refs[1:]):
        y = y + _dot(a_ref[...], w_ref[...])
    x1 = _layer_norm(ALPHA * x_ref[...] + g1_ref[...] * y, lng_ref[...], lnb_ref[...])
    x1_ref[...] = x1
    u2 = x1 * (1.0 + sc2_ref[...]) + sh2_ref[...]
    uh, ul = _split2(u2)
    u2_ref[...] = uh
    if router:
        gate_ref = refs[pos + 2]
        rwh = rwh_ref[...]
        logits = _dot(uh, rwh) + _dot(ul, rwh) + _dot(uh, rwl_ref[...]) + rb_ref[...]
        lane = lax.broadcasted_iota(jnp.int32, logits.shape, 1)
        m1 = jnp.max(logits, axis=1, keepdims=True)
        i1 = jnp.min(jnp.where(logits == m1, lane, LANES), axis=1, keepdims=True)
        rest = jnp.where(lane == i1, 2.0 * NEG, logits)
        m2 = jnp.max(rest, axis=1, keepdims=True)
        i2 = jnp.min(jnp.where(rest == m2, lane, LANES), axis=1, keepdims=True)
        e = jnp.exp(m2 - m1)
        p1 = 1.0 / (1.0 + e)
        gate_ref[...] = jnp.where(lane == i1, p1, jnp.where(lane == i2, e * p1, 0.0))


def _out_proj(acts, ws, x, mods3, row_fn, ln_g, ln_b, router=None):
    t = x.shape[0]
    tm = 512
    n_in = len(acts)
    in_specs = [pl.BlockSpec((tm, a.shape[1]), lambda i: (i, 0)) for a in acts]
    in_specs += [pl.BlockSpec(w.shape, lambda i: (0, 0)) for w in ws]
    in_specs += [pl.BlockSpec((tm, D), lambda i: (i, 0))]
    in_specs += [pl.BlockSpec((None, 1, D), functools.partial(lambda i, k: (row_fn(i, tm), 0, k), k=k))
                 for k in (2, 3, 4)]
    in_specs += [pl.BlockSpec((1, D), lambda i: (0, 0))] * 2
    args = list(acts) + list(ws) + [x, mods3, mods3, mods3, ln_g.reshape(1, D), ln_b.reshape(1, D)]
    out_specs = [pl.BlockSpec((tm, D), lambda i: (i, 0)), pl.BlockSpec((tm, D), lambda i: (i, 0))]
    out_shape = [jax.ShapeDtypeStruct((t, D), F32), jax.ShapeDtypeStruct((t, D), BF16)]
    if router is not None:
        in_specs += [pl.BlockSpec((D, LANES), lambda i: (0, 0))] * 2 + [pl.BlockSpec((1, LANES), lambda i: (0, 0))]
        args += list(router)
        out_specs.append(pl.BlockSpec((tm, LANES), lambda i: (i, 0)))
        out_shape.append(jax.ShapeDtypeStruct((t, LANES), F32))
    return pl.pallas_call(
        functools.partial(_out_kernel, n_in=n_in, router=router is not None),
        grid=(t // tm,),
        in_specs=in_specs,
        out_specs=out_specs,
        out_shape=out_shape,
        compiler_params=_cp("parallel"),
        name="out_proj",
    )(*args)


def _ffn_kernel(u_ref, w1_ref, w3_ref, w2_ref, x1_ref, g2_ref, lng_ref, lnb_ref, o_ref, acc_scr):
    j = pl.program_id(1)

    @pl.when(j == 0)
    def _():
        acc_scr[...] = jnp.zeros(acc_scr.shape, F32)

    u = u_ref[...]
    h = _silu(_dot(u, w1_ref[...])) * _dot(u, w3_ref[...])
    acc_scr[...] += _dot(h.astype(BF16), w2_ref[...])

    @pl.when(j == pl.num_programs(1) - 1)
    def _():
        o_ref[...] = _layer_norm(ALPHA * x1_ref[...] + g2_ref[...] * acc_scr[...], lng_ref[...], lnb_ref[...])


def _ffn(u2, w1, w3, w2, x1, mods3, row_fn, ln_g, ln_b, tf):
    t = u2.shape[0]
    tm = 512
    f = w1.shape[1]
    return pl.pallas_call(
        _ffn_kernel,
        grid=(t // tm, f // tf),
        in_specs=[pl.BlockSpec((tm, D), lambda i, j: (i, 0)),
                  pl.BlockSpec((D, tf), lambda i, j: (0, j)),
                  pl.BlockSpec((D, tf), lambda i, j: (0, j)),
                  pl.BlockSpec((tf, D), lambda i, j: (j, 0)),
                  pl.BlockSpec((tm, D), lambda i, j: (i, 0)),
                  pl.BlockSpec((None, 1, D), lambda i, j: (row_fn(i, tm), 0, 5)),
                  pl.BlockSpec((1, D), lambda i, j: (0, 0)),
                  pl.BlockSpec((1, D), lambda i, j: (0, 0))],
        out_specs=pl.BlockSpec((tm, D), lambda i, j: (i, 0)),
        out_shape=jax.ShapeDtypeStruct((t, D), F32),
        scratch_shapes=[pltpu.VMEM((tm, D), F32)],
        compiler_params=_cp("parallel", "arbitrary"),
        name="ffn",
    )(u2, w1, w3, w2, x1, mods3, ln_g.reshape(1, D), ln_b.reshape(1, D))


MOE_CAP = 320


def _moe_kernel(u_ref, w1_ref, w3_ref, w2_ref, gate_ref, tril_ref, x1_ref, g2_ref, lng_ref, lnb_ref, o_ref,
                acc_scr, rank_scr, rank_t_scr, gate_t_scr):
    e = pl.program_id(1)
    tm = u_ref.shape[0]

    @pl.when(e == 0)
    def _():
        acc_scr[...] = jnp.zeros(acc_scr.shape, F32)
        gate = gate_ref[...]
        routed = gate > 0.0
        rank = jnp.where(routed, _dot(tril_ref[...], jnp.where(routed, 1.0, 0.0).astype(BF16)), -1.0)
        rank_scr[...] = rank
        rank_t_scr[...] = rank.T
        gate_t_scr[...] = gate.T

    lane = lax.broadcasted_iota(jnp.int32, (tm, LANES), 1)
    rank_col = jnp.sum(jnp.where(lane == e, rank_scr[...], 0.0), axis=1, keepdims=True)
    rank_row = rank_t_scr[pl.ds(e, 1), :]
    gate_row = gate_t_scr[pl.ds(e, 1), :]
    count = jnp.sum(jnp.where(rank_row >= 0.0, 1, 0))
    u = u_ref[...]

    def one_pass(blk, carry):
        off = (blk * MOE_CAP).astype(F32)
        slot_c = lax.broadcasted_iota(jnp.int32, (MOE_CAP, tm), 0).astype(F32) + off
        sel = jnp.where(rank_row == slot_c, 1.0, 0.0)
        weight = jnp.sum(sel * gate_row, axis=1, keepdims=True)
        xe = _dot(sel.astype(BF16), u).astype(BF16)
        h = _silu(_dot(xe, w1_ref[...])) * _dot(xe, w3_ref[...])
        y = _dot(h.astype(BF16), w2_ref[...]) * weight
        slot_r = lax.broadcasted_iota(jnp.int32, (tm, MOE_CAP), 1).astype(F32) + off
        sel_t = jnp.where(rank_col == slot_r, 1.0, 0.0).astype(BF16)
        acc_scr[...] += _dot(sel_t, y.astype(BF16))
        return carry

    lax.fori_loop(0, (count + MOE_CAP - 1) // MOE_CAP, one_pass, 0)

    @pl.when(e == pl.num_programs(1) - 1)
    def _():
        o_ref[...] = _layer_norm(ALPHA * x1_ref[...] + g2_ref[...] * acc_scr[...], lng_ref[...], lnb_ref[...])


def _moe(u2, w1, w3, w2, gate, x1, mods3, row_fn, ln_g, ln_b, tm):
    t = u2.shape[0]
    n_e, _, f = w1.shape
    tril = jnp.tril(jnp.ones((tm, tm), F32), -1).astype(BF16)
    once = pl.Buffered(1)
    in_specs = [pl.BlockSpec((tm, D), lambda i, e: (i, 0), pipeline_mode=once),
                pl.BlockSpec((None, D, f), lambda i, e: (e, 0, 0)),
                pl.BlockSpec((None, D, f), lambda i, e: (e, 0, 0)),
                pl.BlockSpec((None, f, D), lambda i, e: (e, 0, 0)),
                pl.BlockSpec((tm, LANES), lambda i, e: (i, 0), pipeline_mode=once),
                pl.BlockSpec((tm, tm), lambda i, e: (0, 0), pipeline_mode=once),
                pl.BlockSpec((tm, D), lambda i, e: (i, 0), pipeline_mode=once),
                pl.BlockSpec((None, 1, D), lambda i, e: (row_fn(i, tm), 0, 5)),
                pl.BlockSpec((1, D), lambda i, e: (0, 0)),
                pl.BlockSpec((1, D), lambda i, e: (0, 0))]
    return pl.pallas_call(
        _moe_kernel,
        grid=(t // tm, n_e),
        in_specs=in_specs,
        out_specs=pl.BlockSpec((tm, D), lambda i, e: (i, 0)),
        out_shape=jax.ShapeDtypeStruct((t, D), F32),
        scratch_shapes=[pltpu.VMEM((tm, D), F32), pltpu.VMEM((tm, LANES), F32),
                        pltpu.VMEM((LANES, tm), F32), pltpu.VMEM((LANES, tm), F32)],
        compiler_params=_cp("parallel", "arbitrary"),
        name="moe",
    )(u2, w1, w3, w2, gate, tril, x1, mods3, ln_g.reshape(1, D), ln_b.reshape(1, D))


def _conv_kernel(x_ref, w_ref, b_ref, o_ref, *, transpose):
    x = x_ref[...].astype(F32)
    s_len = x.shape[0]
    t = lax.broadcasted_iota(jnp.int32, x.shape, 0)
    prev = jnp.where(t == 0, 0.0, pltpu.roll(x, 1, axis=0))
    nxt = jnp.where(t == s_len - 1, 0.0, pltpu.roll(x, s_len - 1, axis=0))
    y = _silu(w_ref[0:1, :] * prev + w_ref[1:2, :] * x + w_ref[2:3, :] * nxt + b_ref[...])
    o_ref[...] = (y.T if transpose else y).astype(o_ref.dtype)


def _conv(pm, conv_w, conv_b, bsz, s_len, col_off, transpose):
    t = pm.shape[0]
    tc = 512
    n_ch = conv_w.shape[1]
    off = col_off // tc
    if transpose:
        out_spec = pl.BlockSpec((None, tc, s_len), lambda b, c: (b, c, 0))
        out_shape = jax.ShapeDtypeStruct((bsz, n_ch, s_len), BF16)
    else:
        out_spec = pl.BlockSpec((s_len, tc), lambda b, c: (b, c))
        out_shape = jax.ShapeDtypeStruct((t, n_ch), BF16)
    return pl.pallas_call(
        functools.partial(_conv_kernel, transpose=transpose),
        grid=(bsz, n_ch // tc),
        in_specs=[pl.BlockSpec((s_len, tc), lambda b, c: (b, off + c)),
                  pl.BlockSpec((3, tc), lambda b, c: (0, c)),
                  pl.BlockSpec((1, tc), lambda b, c: (0, c))],
        out_specs=out_spec,
        out_shape=out_shape,
        compiler_params=_cp("parallel", "parallel"),
        name="ssd_conv",
    )(pm, conv_w, conv_b.reshape(1, n_ch))


def _ssd_kernel(*refs, nc, has_init, out_state):
    xt_ref, b_ref, c_ref, dt_ref, dtn_ref, z_ref, dtb_ref, alog_ref, dsk_ref, ng_ref = refs[:10]
    pos = 10
    if has_init:
        s0_ref = refs[pos]
        pos += 1
    yn_ref = refs[pos]
    pos += 1
    if out_state:
        sst_ref = refs[pos]
        pos += 1
    s_scr, yb_scr, yt_scr, xw_scr, cum_scr, rows_scr = refs[pos:pos + 6]

    j = pl.program_id(1)
    is_fwd = j >= nc
    ci = jnp.where(is_fwd, j - nc, nc - 1 - j)
    sgn = jnp.where(is_fwd, 1, -1)

    def _init(d):
        if has_init:
            s_scr[...] = s0_ref[d]
        else:
            s_scr[...] = jnp.zeros(s_scr.shape, F32)

    pl.when(j == 0)(lambda: _init(1))
    pl.when(j == nc)(lambda: _init(0))

    def _prep(dtraw_ref, sg, slot):
        row = lax.broadcasted_iota(jnp.int32, (CH, CH), 0)
        col = lax.broadcasted_iota(jnp.int32, (CH, CH), 1)
        tri = jnp.where((col - row) * sg <= 0, 1.0, 0.0).astype(BF16)
        f_row = (jnp.zeros((1, LANES), jnp.int32) + sg) > 0
        f_mat = (jnp.zeros((CH, LANES), jnp.int32) + sg) > 0
        f_hc = (jnp.zeros((H_C, LANES), jnp.int32) + sg) > 0
        raw = dtraw_ref[...] + dtb_ref[...]
        raw = jnp.where(f_mat, raw, pltpu.roll(raw, LANES - H_C, axis=1))
        alog = alog_ref[...]
        alog = jnp.where(f_row, alog, pltpu.roll(alog, LANES - H_C, axis=1))
        dtv = _softplus(raw)
        cumv = _tri_dot(tri, dtv * (-jnp.exp(alog)))
        cumv_t = cumv.T[0:H_C, :]
        dtv_t = dtv.T[0:H_C, :]
        last = jnp.where(f_hc, jnp.broadcast_to(cumv_t[:, CH - 1:CH], (H_C, LANES)),
                         jnp.broadcast_to(cumv_t[:, 0:1], (H_C, LANES)))
        cum_scr[slot] = cumv
        rows_scr[slot, 0] = cumv_t
        rows_scr[slot, 1] = dtv_t
        rows_scr[slot, 2] = jnp.exp(cumv_t)
        rows_scr[slot, 3] = jnp.exp(last - cumv_t) * dtv_t
        rows_scr[slot, 4] = jnp.exp(last)

    pl.when(j == 0)(lambda: _prep(dt_ref, sgn, 0))

    slot = j % 2
    row = lax.broadcasted_iota(jnp.int32, (CH, CH), 0)
    col = lax.broadcasted_iota(jnp.int32, (CH, CH), 1)
    causal_t = (row - col) * sgn <= 0
    cum = cum_scr[slot]
    cum_t = rows_scr[slot, 0]
    dt_t = rows_scr[slot, 1]
    ecum_t = rows_scr[slot, 2]
    ws_t = rows_scr[slot, 3]
    dec = rows_scr[slot, 4]
    _prep(dtn_ref, jnp.where(j + 1 >= nc, 1, -1), (j + 1) % 2)

    for gi in range(NG_C):
        gsl = slice(gi * D_STATE, (gi + 1) * D_STATE)
        bg = b_ref[:, gsl]
        cg = c_ref[:, gsl]
        cb_t = _dot_nt(bg, cg)
        ys_t = _dot_nt(s_scr[gi].astype(BF16), cg)
        for rr in range(HPG_C):
            r = gi * HPG_C + rr
            rsl = slice(r * HD, (r + 1) * HD)
            seg_t = jnp.where(causal_t, cum_t[r:r + 1, :] - cum[:, r:r + 1], NEG)
            w_t = (cb_t * jnp.exp(seg_t)).astype(BF16)
            xr_t = xt_ref[rsl, :].astype(F32)
            xd_t = (xr_t * dt_t[r:r + 1, :]).astype(BF16)
            yt_scr[rsl, :] = _dot(xd_t, w_t) + ys_t[rr * HD:(rr + 1) * HD, :] * ecum_t[r:r + 1, :]
            xw_scr[rr * HD:(rr + 1) * HD, :] = (xr_t * ws_t[r:r + 1, :]).astype(BF16)
        upd = _dot(xw_scr[...], bg)
        for rr in range(HPG_C):
            r = gi * HPG_C + rr
            hsl = slice(rr * HD, (rr + 1) * HD)
            s_scr[gi, hsl, :] = s_scr[gi, hsl, :] * dec[r:r + 1, :] + upd[hsl, :]

    @pl.when(j < nc)
    def _():
        yb_scr[ci] = yt_scr[...]

    @pl.when(j >= nc)
    def _():
        yt = yt_scr[...] + yb_scr[ci] + dsk_ref[...] * xt_ref[...].astype(F32)
        y = yt.T * _silu(z_ref[...].astype(F32))
        outs = []
        gw = D_INNER // NG_C
        for gi in range(NG_C):
            yg = y[:, gi * gw:(gi + 1) * gw]
            outs.append(yg * lax.rsqrt(jnp.mean(yg * yg, axis=1, keepdims=True) + EPS))
        yn_ref[...] = (jnp.concatenate(outs, axis=1) * ng_ref[...]).astype(yn_ref.dtype)

    if out_state:
        @pl.when(j == nc - 1)
        def _():
            sst_ref[1] = s_scr[...]

        @pl.when(j == 2 * nc - 1)
        def _():
            sst_ref[0] = s_scr[...]


def _ssd(xt, bc, pa, pm, dtb, alog, dsk, norm_g, bsz, s_len, init=None, out_state=False):
    nc = s_len // CH
    t = bsz * s_len
    gw = NG_C * D_STATE

    def chunk_i(j):
        return jnp.where(j >= nc, j - nc, nc - 1 - j)

    def chunk(b, j):
        return b * nc + chunk_i(j)

    def chunk_next(b, j):
        return chunk(b, jnp.minimum(j + 1, 2 * nc - 1))

    def chunk_fwd(b, j):
        return b * nc + jnp.maximum(j - nc, 0)

    in_specs = [pl.BlockSpec((None, D_INNER, CH), lambda b, j: (b, 0, chunk_i(j))),
                pl.BlockSpec((CH, gw), lambda b, j: (chunk(b, j), 0)),
                pl.BlockSpec((CH, gw), lambda b, j: (chunk(b, j), 1)),
                pl.BlockSpec((CH, LANES), lambda b, j: (chunk(b, j), 0)),
                pl.BlockSpec((CH, LANES), lambda b, j: (chunk_next(b, j), 0)),
                pl.BlockSpec((CH, D_INNER), lambda b, j: (chunk_fwd(b, j), 0)),
                pl.BlockSpec((1, LANES), lambda b, j: (0, 0)),
                pl.BlockSpec((1, LANES), lambda b, j: (0, 0)),
                pl.BlockSpec((D_INNER, LANES), lambda b, j: (0, 0)),
                pl.BlockSpec((1, D_INNER), lambda b, j: (0, 0))]
    args = [xt, bc, bc, pa, pa, pm, dtb, alog, dsk, norm_g]
    state_spec = pl.BlockSpec((None, 2, NG_C, HPG_C * HD, D_STATE), lambda b, j: (b, 0, 0, 0, 0))
    if init is not None:
        in_specs.append(state_spec)
        args.append(init)
    out_specs = [pl.BlockSpec((CH, D_INNER), lambda b, j: (chunk_fwd(b, j), 0))]
    out_shape = [jax.ShapeDtypeStruct((t, D_INNER), BF16)]
    if out_state:
        out_specs.append(state_spec)
        out_shape.append(jax.ShapeDtypeStruct((bsz, 2, NG_C, HPG_C * HD, D_STATE), F32))
    return pl.pallas_call(
        functools.partial(_ssd_kernel, nc=nc, has_init=init is not None, out_state=out_state),
        grid=(bsz, 2 * nc),
        in_specs=in_specs,
        out_specs=out_specs,
        out_shape=out_shape,
        scratch_shapes=[pltpu.VMEM((NG_C, HPG_C * HD, D_STATE), F32),
                        pltpu.VMEM((nc, D_INNER, CH), F32),
                        pltpu.VMEM((D_INNER, CH), F32),
                        pltpu.VMEM((HPG_C * HD, CH), BF16),
                        pltpu.VMEM((2, CH, LANES), F32), pltpu.VMEM((2, 5, H_C, LANES), F32)],
        compiler_params=_cp("parallel", "arbitrary"),
        name="ssd_scan",
    )(*args)


def _rope_tables(s_len):
    nf = HD // 4
    inv = ROPE_THETA ** (-jnp.arange(nf, dtype=F32) / nf)
    tpos = jnp.arange(s_len)
    ang_r = (tpos // GRID_W).astype(F32)[:, None] * inv
    ang_c = (tpos % GRID_W).astype(F32)[:, None] * inv
    cos = jnp.concatenate([jnp.cos(ang_r)] * 2 + [jnp.cos(ang_c)] * 2, axis=1)
    sin = jnp.concatenate([-jnp.sin(ang_r), jnp.sin(ang_r), -jnp.sin(ang_c), jnp.sin(ang_c)], axis=1)
    return jnp.tile(cos, (1, 2)), jnp.tile(sin, (1, 2))


def _block_avg(n):
    idx = np.arange(n) // HD
    return jnp.asarray((idx[:, None] == idx[None, :]).astype(np.float32) / HD, dtype=BF16)


def _pad_cols(w, n):
    return jnp.pad(w, ((0, 0), (0, n - w.shape[1])))


def _row_prompt(i, tm):
    return 0


def _make_row_sample(s_len):
    def row(i, tm):
        return 1 + (i * tm) // s_len
    return row


def kernel(x_prompt, x_sample, cache_k_l0, cache_v_l0, state_mlstm_c_l0, state_mlstm_n_l0, state_mlstm_m_l0, state_ssd_l1, c, c_ctx, ada_w_l0, ada_b_l0, mix_w_in_l0, mlstm_gate_b_l0, mlstm_norm_g_l0, q_norm_g_l0, k_norm_g_l0, mix_w_out_l0, ln1_g_l0, ln1_b_l0, ln2_g_l0, ln2_b_l0, ffn_w1_l0, ffn_w3_l0, ffn_w2_l0, ada_w_l1, ada_b_l1, ssd_w_in_l1, ssd_conv_w_l1, ssd_conv_b_l1, ssd_dt_bias_l1, ssd_a_log_l1, ssd_d_l1, ssd_norm_g_l1, ssd_w_out_l1, ln1_g_l1, ln1_b_l1, ln2_g_l1, ln2_b_l1, router_w_l1, router_b_l1, moe_w1_l1, moe_w3_l1, moe_w2_l1):
    bp, sp, _ = x_prompt.shape
    bs, ss, _ = x_sample.shape
    past = cache_k_l0.shape[1]
    groups = [
        dict(x=x_prompt.reshape(bp * sp, D), b=bp, s=sp, row=_row_prompt, prompt=True),
        dict(x=x_sample.reshape(bs * ss, D), b=bs, s=ss, row=_make_row_sample(ss), prompt=False),
    ]

    cvec = jnp.zeros((16, D), F32).at[0].set(c_ctx).at[1:1 + bs].set(c)
    mods0 = _ada(cvec, ada_w_l0, ada_b_l0).reshape(16, 1, 6 * D)
    mods1 = _ada(cvec, ada_w_l1, ada_b_l1).reshape(16, 1, 6 * D)

    w0_main = jnp.concatenate([mix_w_in_l0[:, :2048], mix_w_in_l0[:, 2080:]], axis=1).astype(BF16)
    w0_aux_hi, w0_aux_lo = _split2(_pad_cols(mix_w_in_l0[:, 2048:2080], LANES))
    gbias = _pad_cols(mlstm_gate_b_l0.reshape(1, 4 * H_A), LANES)
    avg128 = _block_avg(LANES)
    q_g = jnp.tile(q_norm_g_l0, W_A // HD).reshape(1, W_A)
    k_g = jnp.tile(k_norm_g_l0, KV_B).reshape(1, LANES)
    w_out_h = mix_w_out_l0[:W_A].astype(BF16)
    w_out_o = mix_w_out_l0[W_A:].astype(BF16)
    ffn_w1 = ffn_w1_l0.astype(BF16)
    ffn_w3 = ffn_w3_l0.astype(BF16)
    ffn_w2 = ffn_w2_l0.astype(BF16)
    tables = _rope_tables(ss)
    kc = cache_k_l0.transpose(2, 0, 1, 3).reshape(KV_B, bs * past, HD).astype(BF16)
    vc = cache_v_l0.transpose(2, 0, 1, 3).reshape(KV_B, bs * past, HD)
    vc = jnp.concatenate([vc, jnp.ones_like(vc)], axis=-1).astype(BF16)
    c0 = jnp.concatenate([jnp.swapaxes(state_mlstm_c_l0, -1, -2),
                          jnp.broadcast_to(state_mlstm_n_l0[..., None, :], state_mlstm_c_l0.shape)], axis=-2)
    m0 = jnp.broadcast_to(state_mlstm_m_l0[..., None], state_mlstm_m_l0.shape + (LANES,))

    n_main1 = 2 * D_INNER + 2 * NG_C * D_STATE
    w1_main = ssd_w_in_l1[:, :n_main1].astype(BF16)
    w1_aux_hi, w1_aux_lo = _split2(_pad_cols(ssd_w_in_l1[:, n_main1:], LANES))
    dtb = _pad_cols(ssd_dt_bias_l1.reshape(1, 2 * H_C), LANES)
    alog = _pad_cols(ssd_a_log_l1.reshape(1, 2 * H_C), LANES)
    dsk = jnp.broadcast_to(jnp.repeat(ssd_d_l1, HD)[:, None], (D_INNER, LANES))
    ssd_ng = ssd_norm_g_l1.reshape(1, D_INNER)
    w_out1 = ssd_w_out_l1.astype(BF16)
    rw_hi, rw_lo = _split2(_pad_cols(router_w_l1, LANES))
    rb = jnp.full((1, LANES), NEG, F32).at[0, :N_EXPERTS].set(router_b_l1)
    moe_w1 = moe_w1_l1.astype(BF16)
    moe_w3 = moe_w3_l1.astype(BF16)
    moe_w2 = moe_w2_l1.astype(BF16)
    s0 = state_ssd_l1.reshape(bs, 2, NG_C, HPG_C * HD, D_STATE)

    outs = {}
    for grp in groups:
        x, b, s, row, prompt = grp["x"], grp["b"], grp["s"], grp["row"], grp["prompt"]
        pm, pa = _mm_in(x, mods0, row, w0_main, w0_aux_hi, w0_aux_lo, tm=512, n_chunks=2)
        if prompt:
            hn, cst, mst = _mlstm(pm, pa, gbias, mlstm_norm_g_l0.reshape(1, W_A), b, s, out_state=True)
            qn, kh, vh, kn = _qkprep(pm, q_g, k_g, avg128, s, emit_kn=True)
            o = _attn(qn, kh, vh, b, s)
            outs["k"] = kn.reshape(b, s, KV_B, HD)
            outs["v"] = pm[:, 2688:2816].astype(F32).reshape(b, s, KV_B, HD)
            outs["c"] = jnp.swapaxes(cst[..., :HD, :], -1, -2)
            outs["n"] = cst[..., HD, :]
            outs["m"] = mst[..., 0]
        else:
            hn = _mlstm(pm, pa, gbias, mlstm_norm_g_l0.reshape(1, W_A), b, s, init=(c0, m0))[0]
            qn, kh, vh = _qkprep(pm, q_g, k_g, avg128, s, tables=tables)
            o = _attn(qn, kh, vh, b, s, ctx=(kc, vc))
        x1, u2 = _out_proj([hn, o], [w_out_h, w_out_o], x, mods0, row, ln1_g_l0, ln1_b_l0)
        x2 = _ffn(u2, ffn_w1, ffn_w3, ffn_w2, x1, mods0, row, ln2_g_l0, ln2_b_l0, tf=D_FF // 2)
        pm, pa = _mm_in(x2, mods1, row, w1_main, w1_aux_hi, w1_aux_lo, tm=512, n_chunks=4)
        xt = _conv(pm, ssd_conv_w_l1[:, :D_INNER], ssd_conv_b_l1[:D_INNER], b, s, D_INNER, True)
        bc = _conv(pm, ssd_conv_w_l1[:, D_INNER:], ssd_conv_b_l1[D_INNER:], b, s, 2 * D_INNER, False)
        if prompt:
            yn, sst = _ssd(xt, bc, pa, pm, dtb, alog, dsk, ssd_ng, b, s, out_state=True)
            outs["s"] = sst.reshape(b, 2, H_C, HD, D_STATE)
        else:
            yn = _ssd(xt, bc, pa, pm, dtb, alog, dsk, ssd_ng, b, s, init=s0)[0]
        x1, u2, gate = _out_proj([yn], [w_out1], x2, mods1, row, ln1_g_l1, ln1_b_l1, router=(rw_hi, rw_lo, rb))
        x3 = _moe(u2, moe_w1, moe_w3, moe_w2, gate, x1, mods1, row, ln2_g_l1, ln2_b_l1,
                  tm=min(1024, b * s if prompt else s))
        outs["y_p" if prompt else "y_s"] = x3.reshape(b, s, D)

    return (outs["y_p"], outs["y_s"], outs["k"], outs["v"], outs["c"], outs["n"], outs["m"], outs["s"])
```

```python
import functools

import jax
import jax.numpy as jnp
import numpy as np
from jax import lax
from jax.experimental import pallas as pl
from jax.experimental.pallas import tpu as pltpu

F32 = jnp.float32
BF16 = jnp.bfloat16

D = 1024
CH = 128
LANES = 128
EPS = 1e-6
DEPTH = 2
ALPHA = (2 * DEPTH) ** 0.25
GRID_W = 64
ROPE_THETA = 10000.0
H_A = 8
HD = 64
W_A = 512
KV_B = 2
H_C = 32
NG_C = 4
HPG_C = 8
D_STATE = 128
D_INNER = 2048
D_FF = 2816
N_EXPERTS = 8
D_FF_E = 1408
NEG = -1e30
LOG2E = 1.4426950408889634
VMEM_LIMIT = 56 * 1024 * 1024


def _cp(*sem):
    return pltpu.CompilerParams(dimension_semantics=sem, vmem_limit_bytes=VMEM_LIMIT)


def _sigmoid(x):
    return 1.0 / (1.0 + jnp.exp(-x))


def _silu(x):
    return x * _sigmoid(x)


def _softplus(x):
    return jnp.maximum(x, 0.0) + jnp.log(1.0 + jnp.exp(-jnp.abs(x)))


def _split2(x):
    hi = x.astype(BF16)
    lo = (x - hi.astype(F32)).astype(BF16)
    return hi, lo


def _split3(x):
    h1 = x.astype(BF16)
    r = x - h1.astype(F32)
    h2 = r.astype(BF16)
    h3 = (r - h2.astype(F32)).astype(BF16)
    return h1, h2, h3


def _dot(a, b):
    return jnp.dot(a, b, preferred_element_type=F32)


def _dot_nt(a, b):
    return lax.dot_general(a, b, (((1,), (1,)), ((), ())), preferred_element_type=F32)


def _tri_dot(tri, x):
    h1, h2, h3 = _split3(x)
    return _dot(tri, h1) + _dot(tri, h2) + _dot(tri, h3)


def _layer_norm(r, g, b):
    mu = jnp.mean(r, axis=-1, keepdims=True)
    d = r - mu
    var = jnp.mean(d * d, axis=-1, keepdims=True)
    return d * lax.rsqrt(var + EPS) * g + b


def _ada_kernel(c_ref, w_ref, b_ref, o_ref):
    s = _silu(c_ref[...])
    sh, sl = _split2(s)
    wh, wl = _split2(w_ref[...])
    o_ref[...] = _dot(sh, wh) + _dot(sl, wh) + _dot(sh, wl) + b_ref[...]


def _ada(cvec, w, b):
    n = w.shape[1]
    tn = 768
    return pl.pallas_call(
        _ada_kernel,
        grid=(n // tn,),
        in_specs=[pl.BlockSpec((16, D), lambda j: (0, 0)),
                  pl.BlockSpec((D, tn), lambda j: (0, j)),
                  pl.BlockSpec((1, tn), lambda j: (0, j))],
        out_specs=pl.BlockSpec((16, tn), lambda j: (0, j)),
        out_shape=jax.ShapeDtypeStruct((16, n), F32),
        compiler_params=_cp("parallel"),
        name="ada_mods",
    )(cvec, w, b.reshape(1, n))


def _mm_in_kernel(x_ref, sh_ref, sc_ref, wm_ref, wah_ref, wal_ref, om_ref, oa_ref, *, n_chunks):
    u = x_ref[...] * (1.0 + sc_ref[...]) + sh_ref[...]
    uh, ul = _split2(u)
    wah = wah_ref[...]
    oa_ref[...] = _dot(uh, wah) + _dot(ul, wah) + _dot(uh, wal_ref[...])
    tn = wm_ref.shape[1] // n_chunks
    for c in range(n_chunks):
        om_ref[:, c * tn:(c + 1) * tn] = _dot(uh, wm_ref[:, c * tn:(c + 1) * tn]).astype(om_ref.dtype)


def _mm_in(x, mods3, row_fn, wm, wa_hi, wa_lo, tm, n_chunks):
    t, nm = x.shape[0], wm.shape[1]
    once = pl.Buffered(1)
    return pl.pallas_call(
        functools.partial(_mm_in_kernel, n_chunks=n_chunks),
        grid=(t // tm,),
        in_specs=[pl.BlockSpec((tm, D), lambda i: (i, 0)),
                  pl.BlockSpec((None, 1, D), lambda i: (row_fn(i, tm), 0, 0)),
                  pl.BlockSpec((None, 1, D), lambda i: (row_fn(i, tm), 0, 1)),
                  pl.BlockSpec((D, nm), lambda i: (0, 0), pipeline_mode=once),
                  pl.BlockSpec((D, LANES), lambda i: (0, 0), pipeline_mode=once),
                  pl.BlockSpec((D, LANES), lambda i: (0, 0), pipeline_mode=once)],
        out_specs=[pl.BlockSpec((tm, nm), lambda i: (i, 0)),
                   pl.BlockSpec((tm, LANES), lambda i: (i, 0))],
        out_shape=[jax.ShapeDtypeStruct((t, nm), BF16), jax.ShapeDtypeStruct((t, LANES), F32)],
        compiler_params=_cp("parallel"),
        name="mm_in",
    )(x, mods3, mods3, wm, wa_hi, wa_lo)


def _mlstm_kernel(*refs, nc, has_init, out_state):
    q_ref, k_ref, v_ref, om_ref, g_ref, gn_ref, gb_ref, ng_ref = refs[:8]
    pos = 8
    if has_init:
        c0_ref, m0_ref = refs[pos:pos + 2]
        pos += 2
    hn_ref = refs[pos]
    pos += 1
    if out_state:
        cst_ref, mst_ref = refs[pos:pos + 2]
        pos += 2
    c_scr, m_scr, hb_scr, r_scr, rows_scr = refs[pos:pos + 5]

    j = pl.program_id(1)
    is_fwd = j >= nc
    ci = jnp.where(is_fwd, j - nc, nc - 1 - j)
    sgn = jnp.where(is_fwd, 1, -1)

    def _init(d):
        if has_init:
            c_scr[...] = c0_ref[d]
            m_scr[...] = m0_ref[d]
        else:
            c_scr[...] = jnp.zeros(c_scr.shape, F32)
            m_scr[...] = jnp.zeros(m_scr.shape, F32)

    pl.when(j == 0)(lambda: _init(1))
    pl.when(j == nc)(lambda: _init(0))

    def _prep(graw_ref, sg, slot):
        row = lax.broadcasted_iota(jnp.int32, (CH, CH), 0)
        col = lax.broadcasted_iota(jnp.int32, (CH, CH), 1)
        tri = jnp.where((col - row) * sg <= 0, 1.0, 0.0).astype(BF16)
        f_mat = (jnp.zeros((CH, LANES), jnp.int32) + sg) > 0
        f8 = (jnp.zeros((H_A, LANES), jnp.int32) + sg) > 0
        lane8 = lax.broadcasted_iota(jnp.int32, (H_A, LANES), 1)
        g = graw_ref[...] + gb_ref[...]
        li = jnp.where(f_mat, g, pltpu.roll(g, LANES - 16, axis=1))
        lfr = jnp.where(f_mat, pltpu.roll(g, LANES - 8, axis=1), pltpu.roll(g, LANES - 24, axis=1))
        lf = jnp.minimum(lfr, 0.0) - jnp.log(1.0 + jnp.exp(-jnp.abs(lfr)))
        bc = _tri_dot(tri, lf)
        rr = li - bc
        rr_t = rr.T[0:H_A, :]
        cmx = rr_t
        for dlt in (1, 2, 4, 8, 16, 32, 64):
            before = jnp.where(lane8 >= dlt, pltpu.roll(cmx, dlt, axis=1), NEG)
            after = jnp.where(lane8 < LANES - dlt, pltpu.roll(cmx, LANES - dlt, axis=1), NEG)
            cmx = jnp.maximum(cmx, jnp.where(f8, before, after))
        r_scr[slot] = rr * LOG2E
        rows_scr[slot, 0] = rr_t
        rows_scr[slot, 1] = bc.T[0:H_A, :]
        rows_scr[slot, 2] = cmx

    pl.when(j == 0)(lambda: _prep(g_ref, sgn, 0))

    slot = j % 2
    row = lax.broadcasted_iota(jnp.int32, (CH, CH), 0)
    col = lax.broadcasted_iota(jnp.int32, (CH, CH), 1)
    causal_t = (row - col) * sgn <= 0
    fwd8 = (jnp.zeros((H_A, LANES), jnp.int32) + sgn) > 0
    r = r_scr[slot]
    r_t = rows_scr[slot, 0]
    b_t = rows_scr[slot, 1]
    cm = rows_scr[slot, 2]
    _prep(gn_ref, jnp.where(j + 1 >= nc, 1, -1), (j + 1) % 2)

    m_prev = m_scr[...]
    g_t = jnp.maximum(m_prev, cm)
    mt_t = b_t + g_t
    wst_t = jnp.exp(m_prev - g_t)
    emt_t = jnp.exp(-mt_t)
    fwd_c = fwd8[:, 0:1]
    m_new = jnp.where(fwd_c, mt_t[:, LANES - 1:LANES], mt_t[:, 0:1])
    b_last = jnp.where(fwd_c, b_t[:, LANES - 1:LANES], b_t[:, 0:1])
    ws_t = jnp.exp(r_t + (b_last - m_new))
    decay = jnp.exp(b_last + m_prev[:, 0:1] - m_new)
    m_scr[...] = jnp.broadcast_to(m_new, (H_A, LANES))

    q_t = q_ref[...].astype(F32).T.astype(BF16)
    v_t = v_ref[...].astype(F32).T
    ones = jnp.ones((HD, CH), F32)

    hts = []
    g2_t = g_t * LOG2E
    for h in range(H_A):
        sl = slice(h * HD, (h + 1) * HD)
        w_t = jnp.exp2(jnp.where(causal_t, r[:, h:h + 1] - g2_t[h:h + 1, :], NEG))
        kh = k_ref[:, sl]
        qh_t = q_t[sl, :]
        vext_t = jnp.concatenate([v_t[sl, :], ones], axis=0)
        sc_t = (_dot(kh, qh_t) * 0.125 * w_t).astype(BF16)
        cext_t = c_scr[h]
        nd_t = _dot(vext_t.astype(BF16), sc_t) + (_dot(cext_t.astype(BF16), qh_t) * 0.125) * wst_t[h:h + 1, :]
        den = jnp.maximum(jnp.abs(nd_t[HD:HD + 1, :]), emt_t[h:h + 1, :])
        hts.append(nd_t[0:HD, :] / den)
        c_scr[h] = decay[h:h + 1, :] * cext_t + _dot((vext_t * ws_t[h:h + 1, :]).astype(BF16), kh)
    hcat_t = jnp.concatenate(hts, axis=0)

    @pl.when(j < nc)
    def _():
        hb_scr[ci] = hcat_t

    @pl.when(j >= nc)
    def _():
        ht = hcat_t + hb_scr[ci]
        normed = []
        for h in range(H_A):
            hh = ht[h * HD:(h + 1) * HD, :]
            dlt = hh - jnp.mean(hh, axis=0, keepdims=True)
            normed.append(dlt * lax.rsqrt(jnp.mean(dlt * dlt, axis=0, keepdims=True) + EPS))
        hn = jnp.concatenate(normed, axis=0).T * ng_ref[...] * _sigmoid(om_ref[...].astype(F32))
        hn_ref[...] = hn.astype(hn_ref.dtype)

    if out_state:
        @pl.when(j == nc - 1)
        def _():
            cst_ref[1] = c_scr[...]
            mst_ref[1] = m_scr[...]

        @pl.when(j == 2 * nc - 1)
        def _():
            cst_ref[0] = c_scr[...]
            mst_ref[0] = m_scr[...]


def _mlstm(pm, pa, gbias, norm_g, bsz, s_len, init=None, out_state=False):
    nc = s_len // CH
    t = bsz * s_len

    def chunk(b, j):
        return b * nc + jnp.where(j >= nc, j - nc, nc - 1 - j)

    def chunk_next(b, j):
        return chunk(b, jnp.minimum(j + 1, 2 * nc - 1))

    def chunk_fwd(b, j):
        return b * nc + jnp.maximum(j - nc, 0)

    in_specs = [pl.BlockSpec((CH, W_A), lambda b, j: (chunk(b, j), 0)),
                pl.BlockSpec((CH, W_A), lambda b, j: (chunk(b, j), 1)),
                pl.BlockSpec((CH, W_A), lambda b, j: (chunk(b, j), 2)),
                pl.BlockSpec((CH, W_A), lambda b, j: (chunk_fwd(b, j), 3)),
                pl.BlockSpec((CH, LANES), lambda b, j: (chunk(b, j), 0)),
                pl.BlockSpec((CH, LANES), lambda b, j: (chunk_next(b, j), 0)),
                pl.BlockSpec((1, LANES), lambda b, j: (0, 0)),
                pl.BlockSpec((1, W_A), lambda b, j: (0, 0))]
    args = [pm, pm, pm, pm, pa, pa, gbias, norm_g]
    if init is not None:
        in_specs += [pl.BlockSpec((None, 2, H_A, LANES, HD), lambda b, j: (b, 0, 0, 0, 0)),
                     pl.BlockSpec((None, 2, H_A, LANES), lambda b, j: (b, 0, 0, 0))]
        args += list(init)
    out_specs = [pl.BlockSpec((CH, W_A), lambda b, j: (chunk_fwd(b, j), 0))]
    out_shape = [jax.ShapeDtypeStruct((t, W_A), BF16)]
    if out_state:
        out_specs += [pl.BlockSpec((None, 2, H_A, LANES, HD), lambda b, j: (b, 0, 0, 0, 0)),
                      pl.BlockSpec((None, 2, H_A, LANES), lambda b, j: (b, 0, 0, 0))]
        out_shape += [jax.ShapeDtypeStruct((bsz, 2, H_A, LANES, HD), F32),
                      jax.ShapeDtypeStruct((bsz, 2, H_A, LANES), F32)]
    return pl.pallas_call(
        functools.partial(_mlstm_kernel, nc=nc, has_init=init is not None, out_state=out_state),
        grid=(bsz, 2 * nc),
        in_specs=in_specs,
        out_specs=out_specs,
        out_shape=out_shape,
        scratch_shapes=[pltpu.VMEM((H_A, LANES, HD), F32), pltpu.VMEM((H_A, LANES), F32),
                        pltpu.VMEM((nc, W_A, CH), F32),
                        pltpu.VMEM((2, CH, LANES), F32), pltpu.VMEM((2, 3, H_A, LANES), F32)],
        compiler_params=_cp("parallel", "arbitrary"),
        name="mlstm",
    )(*args)


def _group_rms(x, avg, g):
    s1, s2 = _split2(x * x)
    ms = _dot(s1, avg) + _dot(s2, avg)
    return x * lax.rsqrt(ms + EPS) * g


def _rope(x, cos, sin):
    lane = lax.broadcasted_iota(jnp.int32, x.shape, 1)
    swapped = jnp.where((lane % 32) < 16, pltpu.roll(x, LANES - 16, axis=1), pltpu.roll(x, 16, axis=1))
    return x * cos + swapped * sin


def _qkprep_kernel(*refs, rope, emit_kn):
    q_ref, k_ref, v_ref, qg_ref, kg_ref, avg_ref = refs[:6]
    pos = 6
    if rope:
        cos_ref, sin_ref = refs[pos:pos + 2]
        pos += 2
    qn_ref, kh_ref, vh_ref = refs[pos:pos + 3]
    pos += 3
    avg = avg_ref[...]
    qs = []
    for c in range(W_A // LANES):
        sl = slice(c * LANES, (c + 1) * LANES)
        xn = _group_rms(q_ref[:, sl].astype(F32), avg, qg_ref[:, sl])
        if rope:
            xn = _rope(xn, cos_ref[...], sin_ref[...])
        qs.append(xn * 0.125)
    qn_ref[...] = jnp.concatenate(qs, axis=1).astype(qn_ref.dtype)
    kn = _group_rms(k_ref[...].astype(F32), avg, kg_ref[...])
    if emit_kn:
        refs[pos][...] = kn
    if rope:
        kn = _rope(kn, cos_ref[...], sin_ref[...])
    v = v_ref[...].astype(F32)
    ones = jnp.ones((v.shape[0], HD), F32)
    for gi in range(KV_B):
        sl = slice(gi * HD, (gi + 1) * HD)
        kh_ref[gi] = kn[:, sl].astype(kh_ref.dtype)
        vh_ref[gi] = jnp.concatenate([v[:, sl], ones], axis=1).astype(vh_ref.dtype)


def _qkprep(pm, q_g, k_g, avg128, s_len, tables=None, emit_kn=False):
    t = pm.shape[0]
    tm = 256
    nt = s_len // tm
    qoff, koff, voff = 2048 // W_A, 2560 // LANES, 2688 // LANES
    in_specs = [pl.BlockSpec((tm, W_A), lambda i: (i, qoff)),
                pl.BlockSpec((tm, LANES), lambda i: (i, koff)),
                pl.BlockSpec((tm, LANES), lambda i: (i, voff)),
                pl.BlockSpec((1, W_A), lambda i: (0, 0)),
                pl.BlockSpec((1, LANES), lambda i: (0, 0)),
                pl.BlockSpec((LANES, LANES), lambda i: (0, 0))]
    args = [pm, pm, pm, q_g, k_g, avg128]
    if tables is not None:
        in_specs += [pl.BlockSpec((tm, LANES), lambda i: (i % nt, 0))] * 2
        args += list(tables)
    out_specs = [pl.BlockSpec((tm, W_A), lambda i: (i, 0)),
                 pl.BlockSpec((KV_B, tm, HD), lambda i: (0, i, 0)),
                 pl.BlockSpec((KV_B, tm, LANES), lambda i: (0, i, 0))]
    out_shape = [jax.ShapeDtypeStruct((t, W_A), BF16),
                 jax.ShapeDtypeStruct((KV_B, t, HD), BF16),
                 jax.ShapeDtypeStruct((KV_B, t, LANES), BF16)]
    if emit_kn:
        out_specs.append(pl.BlockSpec((tm, LANES), lambda i: (i, 0)))
        out_shape.append(jax.ShapeDtypeStruct((t, LANES), F32))
    return pl.pallas_call(
        functools.partial(_qkprep_kernel, rope=tables is not None, emit_kn=emit_kn),
        grid=(t // tm,),
        in_specs=in_specs,
        out_specs=out_specs,
        out_shape=out_shape,
        compiler_params=_cp("parallel"),
        name="qkprep",
    )(*args)


def _attn_kernel(*refs, has_ctx):
    if has_ctx:
        q_ref, k_ref, v_ref, kc_ref, vc_ref, o_ref = refs
    else:
        q_ref, k_ref, v_ref, o_ref = refs
    k = k_ref[...]
    v = v_ref[...]
    outs = []
    for hq in range(4):
        qh = q_ref[:, hq * HD:(hq + 1) * HD]
        s = _dot_nt(qh, k)
        m = jnp.max(s, axis=1, keepdims=True)
        if has_ctx:
            sc = _dot_nt(qh, kc_ref[...])
            m = jnp.maximum(m, jnp.max(sc, axis=1, keepdims=True))
        oe = _dot(jnp.exp(s - m).astype(BF16), v)
        if has_ctx:
            oe = oe + _dot(jnp.exp(sc - m).astype(BF16), vc_ref[...])
        outs.append(oe[:, :HD] / oe[:, HD:HD + 1])
    o_ref[...] = jnp.concatenate(outs, axis=1).astype(o_ref.dtype)


def _attn(qn, kh, vh, bsz, s_len, ctx=None):
    t = qn.shape[0]
    tq = min(256, s_len)
    nq = s_len // tq
    in_specs = [pl.BlockSpec((tq, 256), lambda b, g, i: (b * nq + i, g)),
                pl.BlockSpec((None, s_len, HD), lambda b, g, i: (g, b, 0)),
                pl.BlockSpec((None, s_len, LANES), lambda b, g, i: (g, b, 0))]
    args = [qn, kh, vh]
    if ctx is not None:
        past = ctx[0].shape[1] // bsz
        in_specs += [pl.BlockSpec((None, past, HD), lambda b, g, i: (g, b, 0)),
                     pl.BlockSpec((None, past, LANES), lambda b, g, i: (g, b, 0))]
        args += list(ctx)
    return pl.pallas_call(
        functools.partial(_attn_kernel, has_ctx=ctx is not None),
        grid=(bsz, KV_B, nq),
        in_specs=in_specs,
        out_specs=pl.BlockSpec((tq, 256), lambda b, g, i: (b * nq + i, g)),
        out_shape=jax.ShapeDtypeStruct((t, W_A), BF16),
        compiler_params=_cp("parallel", "parallel", "parallel"),
        name="attn",
    )(*args)


def _out_kernel(*refs, n_in, router):
    a_refs = refs[:n_in]
    w_refs = refs[n_in:2 * n_in]
    x_ref, g1_ref, sh2_ref, sc2_ref, lng_ref, lnb_ref = refs[2 * n_in:2 * n_in + 6]
    pos = 2 * n_in + 6
    if router:
        rwh_ref, rwl_ref, rb_ref = refs[pos:pos + 3]
        pos += 3
    x1_ref, u2_ref = refs[pos:pos + 2]
    y = _dot(a_refs[0][...], w_refs[0][...])
    for a_ref, w_ref in zip(a_refs[1:], w_refs[1:]):
        y = y + _dot(a_ref[...], w_ref[...])
    x1 = _layer_norm(ALPHA * x_ref[...] + g1_ref[...] * y, lng_ref[...], lnb_ref[...])
    x1_ref[...] = x1
    u2 = x1 * (1.0 + sc2_ref[...]) + sh2_ref[...]
    uh, ul = _split2(u2)
    u2_ref[...] = uh
    if router:
        gate_ref = refs[pos + 2]
        rwh = rwh_ref[...]
        logits = _dot(uh, rwh) + _dot(ul, rwh) + _dot(uh, rwl_ref[...]) + rb_ref[...]
        lane = lax.broadcasted_iota(jnp.int32, logits.shape, 1)
        m1 = jnp.max(logits, axis=1, keepdims=True)
        i1 = jnp.min(jnp.where(logits == m1, lane, LANES), axis=1, keepdims=True)
        rest = jnp.where(lane == i1, 2.0 * NEG, logits)
        m2 = jnp.max(rest, axis=1, keepdims=True)
        i2 = jnp.min(jnp.where(rest == m2, lane, LANES), axis=1, keepdims=True)
        e = jnp.exp(m2 - m1)
        p1 = 1.0 / (1.0 + e)
        gate_ref[...] = jnp.where(lane == i1, p1, jnp.where(lane == i2, e * p1, 0.0))


def _out_proj(acts, ws, x, mods3, row_fn, ln_g, ln_b, router=None):
    t = x.shape[0]
    tm = 512
    n_in = len(acts)
    in_specs = [pl.BlockSpec((tm, a.shape[1]), lambda i: (i, 0)) for a in acts]
    in_specs += [pl.BlockSpec(w.shape, lambda i: (0, 0)) for w in ws]
    in_specs += [pl.BlockSpec((tm, D), lambda i: (i, 0))]
    in_specs += [pl.BlockSpec((None, 1, D), functools.partial(lambda i, k: (row_fn(i, tm), 0, k), k=k))
                 for k in (2, 3, 4)]
    in_specs += [pl.BlockSpec((1, D), lambda i: (0, 0))] * 2
    args = list(acts) + list(ws) + [x, mods3, mods3, mods3, ln_g.reshape(1, D), ln_b.reshape(1, D)]
    out_specs = [pl.BlockSpec((tm, D), lambda i: (i, 0)), pl.BlockSpec((tm, D), lambda i: (i, 0))]
    out_shape = [jax.ShapeDtypeStruct((t, D), F32), jax.ShapeDtypeStruct((t, D), BF16)]
    if router is not None:
        in_specs += [pl.BlockSpec((D, LANES), lambda i: (0, 0))] * 2 + [pl.BlockSpec((1, LANES), lambda i: (0, 0))]
        args += list(router)
        out_specs.append(pl.BlockSpec((tm, LANES), lambda i: (i, 0)))
        out_shape.append(jax.ShapeDtypeStruct((t, LANES), F32))
    return pl.pallas_call(
        functools.partial(_out_kernel, n_in=n_in, router=router is not None),
        grid=(t // tm,),
        in_specs=in_specs,
        out_specs=out_specs,
        out_shape=out_shape,
        compiler_params=_cp("parallel"),
        name="out_proj",
    )(*args)


def _ffn_kernel(u_ref, w1_ref, w3_ref, w2_ref, x1_ref, g2_ref, lng_ref, lnb_ref, o_ref, acc_scr):
    j = pl.program_id(1)

    @pl.when(j == 0)
    def _():
        acc_scr[...] = jnp.zeros(acc_scr.shape, F32)

    u = u_ref[...]
    h = _silu(_dot(u, w1_ref[...])) * _dot(u, w3_ref[...])
    acc_scr[...] += _dot(h.astype(BF16), w2_ref[...])

    @pl.when(j == pl.num_programs(1) - 1)
    def _():
        o_ref[...] = _layer_norm(ALPHA * x1_ref[...] + g2_ref[...] * acc_scr[...], lng_ref[...], lnb_ref[...])


def _ffn(u2, w1, w3, w2, x1, mods3, row_fn, ln_g, ln_b, tf):
    t = u2.shape[0]
    tm = 512
    f = w1.shape[1]
    return pl.pallas_call(
        _ffn_kernel,
        grid=(t // tm, f // tf),
        in_specs=[pl.BlockSpec((tm, D), lambda i, j: (i, 0)),
                  pl.BlockSpec((D, tf), lambda i, j: (0, j)),
                  pl.BlockSpec((D, tf), lambda i, j: (0, j)),
                  pl.BlockSpec((tf, D), lambda i, j: (j, 0)),
                  pl.BlockSpec((tm, D), lambda i, j: (i, 0)),
                  pl.BlockSpec((None, 1, D), lambda i, j: (row_fn(i, tm), 0, 5)),
                  pl.BlockSpec((1, D), lambda i, j: (0, 0)),
                  pl.BlockSpec((1, D), lambda i, j: (0, 0))],
        out_specs=pl.BlockSpec((tm, D), lambda i, j: (i, 0)),
        out_shape=jax.ShapeDtypeStruct((t, D), F32),
        scratch_shapes=[pltpu.VMEM((tm, D), F32)],
        compiler_params=_cp("parallel", "arbitrary"),
        name="ffn",
    )(u2, w1, w3, w2, x1, mods3, ln_g.reshape(1, D), ln_b.reshape(1, D))


MOE_CAP = 256


def _moe_kernel(u_ref, w1_ref, w3_ref, w2_ref, gate_ref, tril_ref, x1_ref, g2_ref, lng_ref, lnb_ref, o_ref,
                acc_scr, rank_scr, rank_t_scr, gate_t_scr):
    e = pl.program_id(1)
    tm = u_ref.shape[0]

    @pl.when(e == 0)
    def _():
        acc_scr[...] = jnp.zeros(acc_scr.shape, F32)
        gate = gate_ref[...]
        routed = gate > 0.0
        rank = jnp.where(routed, _dot(tril_ref[...], jnp.where(routed, 1.0, 0.0).astype(BF16)), -1.0)
        rank_scr[...] = rank
        rank_t_scr[...] = rank.T
        gate_t_scr[...] = gate.T

    lane = lax.broadcasted_iota(jnp.int32, (tm, LANES), 1)
    rank_col = jnp.sum(jnp.where(lane == e, rank_scr[...], 0.0), axis=1, keepdims=True)
    rank_row = rank_t_scr[pl.ds(e, 1), :]
    gate_row = gate_t_scr[pl.ds(e, 1), :]
    count = jnp.sum(jnp.where(rank_row >= 0.0, 1, 0))
    u = u_ref[...]

    def one_pass(first_slot, cap):
        off = first_slot.astype(F32)
        slot_c = lax.broadcasted_iota(jnp.int32, (cap, tm), 0).astype(F32) + off
        sel = jnp.where(rank_row == slot_c, 1.0, 0.0)
        weight = jnp.sum(sel * gate_row, axis=1, keepdims=True)
        xe = _dot(sel.astype(BF16), u).astype(BF16)
        h = _silu(_dot(xe, w1_ref[...])) * _dot(xe, w3_ref[...])
        y = _dot(h.astype(BF16), w2_ref[...]) * weight
        slot_r = lax.broadcasted_iota(jnp.int32, (tm, cap), 1).astype(F32) + off
        sel_t = jnp.where(rank_col == slot_r, 1.0, 0.0).astype(BF16)
        acc_scr[...] += _dot(sel_t, y.astype(BF16))

    n_full = count // MOE_CAP
    left = count - n_full * MOE_CAP
    n_full = n_full + jnp.where(left > MOE_CAP // 2, 1, 0)

    def full_pass(blk, carry):
        one_pass(blk * MOE_CAP, MOE_CAP)
        return carry

    lax.fori_loop(0, n_full, full_pass, 0)

    @pl.when(jnp.logical_and(left > 0, left <= MOE_CAP // 2))
    def _():
        one_pass(n_full * MOE_CAP, MOE_CAP // 2)

    @pl.when(e == pl.num_programs(1) - 1)
    def _():
        o_ref[...] = _layer_norm(ALPHA * x1_ref[...] + g2_ref[...] * acc_scr[...], lng_ref[...], lnb_ref[...])


def _moe(u2, w1, w3, w2, gate, x1, mods3, row_fn, ln_g, ln_b, tm):
    t = u2.shape[0]
    n_e, _, f = w1.shape
    tril = jnp.tril(jnp.ones((tm, tm), F32), -1).astype(BF16)
    once = pl.Buffered(1)
    in_specs = [pl.BlockSpec((tm, D), lambda i, e: (i, 0), pipeline_mode=once),
                pl.BlockSpec((None, D, f), lambda i, e: (e, 0, 0)),
                pl.BlockSpec((None, D, f), lambda i, e: (e, 0, 0)),
                pl.BlockSpec((None, f, D), lambda i, e: (e, 0, 0)),
                pl.BlockSpec((tm, LANES), lambda i, e: (i, 0), pipeline_mode=once),
                pl.BlockSpec((tm, tm), lambda i, e: (0, 0), pipeline_mode=once),
                pl.BlockSpec((tm, D), lambda i, e: (i, 0), pipeline_mode=once),
                pl.BlockSpec((None, 1, D), lambda i, e: (row_fn(i, tm), 0, 5)),
                pl.BlockSpec((1, D), lambda i, e: (0, 0)),
                pl.BlockSpec((1, D), lambda i, e: (0, 0))]
    return pl.pallas_call(
        _moe_kernel,
        grid=(t // tm, n_e),
        in_specs=in_specs,
        out_specs=pl.BlockSpec((tm, D), lambda i, e: (i, 0)),
        out_shape=jax.ShapeDtypeStruct((t, D), F32),
        scratch_shapes=[pltpu.VMEM((tm, D), F32), pltpu.VMEM((tm, LANES), F32),
                        pltpu.VMEM((LANES, tm), F32), pltpu.VMEM((LANES, tm), F32)],
        compiler_params=_cp("parallel", "arbitrary"),
        name="moe",
    )(u2, w1, w3, w2, gate, tril, x1, mods3, ln_g.reshape(1, D), ln_b.reshape(1, D))


def _conv_kernel(x_ref, w_ref, b_ref, o_ref, *, s_len, transpose):
    x = x_ref[...].astype(F32)
    rows = x.shape[0]
    t = lax.broadcasted_iota(jnp.int32, x.shape, 0) % s_len
    prev = jnp.where(t == 0, 0.0, pltpu.roll(x, 1, axis=0))
    nxt = jnp.where(t == s_len - 1, 0.0, pltpu.roll(x, rows - 1, axis=0))
    y = _silu(w_ref[0:1, :] * prev + w_ref[1:2, :] * x + w_ref[2:3, :] * nxt + b_ref[...])
    if transpose:
        for q in range(rows // s_len):
            o_ref[q] = y[q * s_len:(q + 1) * s_len, :].T.astype(o_ref.dtype)
    else:
        o_ref[...] = y.astype(o_ref.dtype)


def _conv(pm, conv_w, conv_b, bsz, s_len, col_off, transpose):
    t = pm.shape[0]
    tc = 512
    n_ch = conv_w.shape[1]
    off = col_off // tc
    n_seq = max(1, min(bsz, 2048 // s_len))
    rows = n_seq * s_len
    if transpose:
        out_spec = pl.BlockSpec((n_seq, tc, s_len), lambda b, c: (b, c, 0))
        out_shape = jax.ShapeDtypeStruct((bsz, n_ch, s_len), BF16)
    else:
        out_spec = pl.BlockSpec((rows, tc), lambda b, c: (b, c))
        out_shape = jax.ShapeDtypeStruct((t, n_ch), BF16)
    return pl.pallas_call(
        functools.partial(_conv_kernel, s_len=s_len, transpose=transpose),
        grid=(bsz // n_seq, n_ch // tc),
        in_specs=[pl.BlockSpec((rows, tc), lambda b, c: (b, off + c)),
                  pl.BlockSpec((3, tc), lambda b, c: (0, c)),
                  pl.BlockSpec((1, tc), lambda b, c: (0, c))],
        out_specs=out_spec,
        out_shape=out_shape,
        compiler_params=_cp("parallel", "parallel"),
        name="ssd_conv",
    )(pm, conv_w, conv_b.reshape(1, n_ch))


def _ssd_kernel(*refs, nc, has_init, out_state):
    xt_ref, b_ref, c_ref, dt_ref, dtn_ref, z_ref, dtb_ref, alog_ref, dsk_ref, ng_ref = refs[:10]
    pos = 10
    if has_init:
        s0_ref = refs[pos]
        pos += 1
    yn_ref = refs[pos]
    pos += 1
    if out_state:
        sst_ref = refs[pos]
        pos += 1
    s_scr, yb_scr, yt_scr, xw_scr, cum_scr, rows_scr = refs[pos:pos + 6]

    j = pl.program_id(1)
    is_fwd = j >= nc
    ci = jnp.where(is_fwd, j - nc, nc - 1 - j)
    sgn = jnp.where(is_fwd, 1, -1)

    def _init(d):
        if has_init:
            s_scr[...] = s0_ref[d]
        else:
            s_scr[...] = jnp.zeros(s_scr.shape, F32)

    pl.when(j == 0)(lambda: _init(1))
    pl.when(j == nc)(lambda: _init(0))

    def _prep(dtraw_ref, sg, slot):
        row = lax.broadcasted_iota(jnp.int32, (CH, CH), 0)
        col = lax.broadcasted_iota(jnp.int32, (CH, CH), 1)
        tri = jnp.where((col - row) * sg <= 0, 1.0, 0.0).astype(BF16)
        f_row = (jnp.zeros((1, LANES), jnp.int32) + sg) > 0
        f_mat = (jnp.zeros((CH, LANES), jnp.int32) + sg) > 0
        f_hc = (jnp.zeros((H_C, LANES), jnp.int32) + sg) > 0
        raw = dtraw_ref[...] + dtb_ref[...]
        raw = jnp.where(f_mat, raw, pltpu.roll(raw, LANES - H_C, axis=1))
        alog = alog_ref[...]
        alog = jnp.where(f_row, alog, pltpu.roll(alog, LANES - H_C, axis=1))
        dtv = _softplus(raw)
        cumv = _tri_dot(tri, dtv * (-jnp.exp(alog)))
        cumv_t = cumv.T[0:H_C, :]
        dtv_t = dtv.T[0:H_C, :]
        last = jnp.where(f_hc, jnp.broadcast_to(cumv_t[:, CH - 1:CH], (H_C, LANES)),
                         jnp.broadcast_to(cumv_t[:, 0:1], (H_C, LANES)))
        cum_scr[slot] = cumv * LOG2E
        rows_scr[slot, 0] = cumv_t * LOG2E
        rows_scr[slot, 1] = dtv_t
        rows_scr[slot, 2] = jnp.exp(cumv_t)
        rows_scr[slot, 3] = jnp.exp(last - cumv_t) * dtv_t
        rows_scr[slot, 4] = jnp.exp(last)

    pl.when(j == 0)(lambda: _prep(dt_ref, sgn, 0))

    slot = j % 2
    row = lax.broadcasted_iota(jnp.int32, (CH, CH), 0)
    col = lax.broadcasted_iota(jnp.int32, (CH, CH), 1)
    causal_t = (row - col) * sgn <= 0
    cum = cum_scr[slot]
    cum_t = rows_scr[slot, 0]
    dt_t = rows_scr[slot, 1]
    ecum_t = rows_scr[slot, 2]
    ws_t = rows_scr[slot, 3]
    dec = rows_scr[slot, 4]
    _prep(dtn_ref, jnp.where(j + 1 >= nc, 1, -1), (j + 1) % 2)

    for gi in range(NG_C):
        gsl = slice(gi * D_STATE, (gi + 1) * D_STATE)
        bg = b_ref[:, gsl]
        cg = c_ref[:, gsl]
        cb_t = _dot_nt(bg, cg)
        ys_t = _dot_nt(s_scr[gi].astype(BF16), cg)
        for rr in range(HPG_C):
            r = gi * HPG_C + rr
            rsl = slice(r * HD, (r + 1) * HD)
            seg_t = jnp.where(causal_t, cum_t[r:r + 1, :] - cum[:, r:r + 1], NEG)
            w_t = (cb_t * jnp.exp2(seg_t)).astype(BF16)
            xr_t = xt_ref[rsl, :].astype(F32)
            xd_t = (xr_t * dt_t[r:r + 1, :]).astype(BF16)
            yt_scr[rsl, :] = _dot(xd_t, w_t) + ys_t[rr * HD:(rr + 1) * HD, :] * ecum_t[r:r + 1, :]
            xw_scr[rr * HD:(rr + 1) * HD, :] = (xr_t * ws_t[r:r + 1, :]).astype(BF16)
        upd = _dot(xw_scr[...], bg)
        for rr in range(HPG_C):
            r = gi * HPG_C + rr
            hsl = slice(rr * HD, (rr + 1) * HD)
            s_scr[gi, hsl, :] = s_scr[gi, hsl, :] * dec[r:r + 1, :] + upd[hsl, :]

    @pl.when(j < nc)
    def _():
        yb_scr[ci] = yt_scr[...]

    @pl.when(j >= nc)
    def _():
        yt = yt_scr[...] + yb_scr[ci] + dsk_ref[...] * xt_ref[...].astype(F32)
        y = yt.T * _silu(z_ref[...].astype(F32))
        outs = []
        gw = D_INNER // NG_C
        for gi in range(NG_C):
            yg = y[:, gi * gw:(gi + 1) * gw]
            outs.append(yg * lax.rsqrt(jnp.mean(yg * yg, axis=1, keepdims=True) + EPS))
        yn_ref[...] = (jnp.concatenate(outs, axis=1) * ng_ref[...]).astype(yn_ref.dtype)

    if out_state:
        @pl.when(j == nc - 1)
        def _():
            sst_ref[1] = s_scr[...]

        @pl.when(j == 2 * nc - 1)
        def _():
            sst_ref[0] = s_scr[...]


def _ssd(xt, bc, pa, pm, dtb, alog, dsk, norm_g, bsz, s_len, init=None, out_state=False):
    nc = s_len // CH
    t = bsz * s_len
    gw = NG_C * D_STATE

    def chunk_i(j):
        return jnp.where(j >= nc, j - nc, nc - 1 - j)

    def chunk(b, j):
        return b * nc + chunk_i(j)

    def chunk_next(b, j):
        return chunk(b, jnp.minimum(j + 1, 2 * nc - 1))

    def chunk_fwd(b, j):
        return b * nc + jnp.maximum(j - nc, 0)

    in_specs = [pl.BlockSpec((None, D_INNER, CH), lambda b, j: (b, 0, chunk_i(j))),
                pl.BlockSpec((CH, gw), lambda b, j: (chunk(b, j), 0)),
                pl.BlockSpec((CH, gw), lambda b, j: (chunk(b, j), 1)),
                pl.BlockSpec((CH, LANES), lambda b, j: (chunk(b, j), 0)),
                pl.BlockSpec((CH, LANES), lambda b, j: (chunk_next(b, j), 0)),
                pl.BlockSpec((CH, D_INNER), lambda b, j: (chunk_fwd(b, j), 0)),
                pl.BlockSpec((1, LANES), lambda b, j: (0, 0)),
                pl.BlockSpec((1, LANES), lambda b, j: (0, 0)),
                pl.BlockSpec((D_INNER, LANES), lambda b, j: (0, 0)),
                pl.BlockSpec((1, D_INNER), lambda b, j: (0, 0))]
    args = [xt, bc, bc, pa, pa, pm, dtb, alog, dsk, norm_g]
    state_spec = pl.BlockSpec((None, 2, NG_C, HPG_C * HD, D_STATE), lambda b, j: (b, 0, 0, 0, 0))
    if init is not None:
        in_specs.append(state_spec)
        args.append(init)
    out_specs = [pl.BlockSpec((CH, D_INNER), lambda b, j: (chunk_fwd(b, j), 0))]
    out_shape = [jax.ShapeDtypeStruct((t, D_INNER), BF16)]
    if out_state:
        out_specs.append(state_spec)
        out_shape.append(jax.ShapeDtypeStruct((bsz, 2, NG_C, HPG_C * HD, D_STATE), F32))
    return pl.pallas_call(
        functools.partial(_ssd_kernel, nc=nc, has_init=init is not None, out_state=out_state),
        grid=(bsz, 2 * nc),
        in_specs=in_specs,
        out_specs=out_specs,
        out_shape=out_shape,
        scratch_shapes=[pltpu.VMEM((NG_C, HPG_C * HD, D_STATE), F32),
                        pltpu.VMEM((nc, D_INNER, CH), F32),
                        pltpu.VMEM((D_INNER, CH), F32),
                        pltpu.VMEM((HPG_C * HD, CH), BF16),
                        pltpu.VMEM((2, CH, LANES), F32), pltpu.VMEM((2, 5, H_C, LANES), F32)],
        compiler_params=_cp("parallel", "arbitrary"),
        name="ssd_scan",
    )(*args)


def _rope_tables(s_len):
    nf = HD // 4
    inv = ROPE_THETA ** (-jnp.arange(nf, dtype=F32) / nf)
    tpos = jnp.arange(s_len)
    ang_r = (tpos // GRID_W).astype(F32)[:, None] * inv
    ang_c = (tpos % GRID_W).astype(F32)[:, None] * inv
    cos = jnp.concatenate([jnp.cos(ang_r)] * 2 + [jnp.cos(ang_c)] * 2, axis=1)
    sin = jnp.concatenate([-jnp.sin(ang_r), jnp.sin(ang_r), -jnp.sin(ang_c), jnp.sin(ang_c)], axis=1)
    return jnp.tile(cos, (1, 2)), jnp.tile(sin, (1, 2))


def _block_avg(n):
    idx = np.arange(n) // HD
    return jnp.asarray((idx[:, None] == idx[None, :]).astype(np.float32) / HD, dtype=BF16)


def _pad_cols(w, n):
    return jnp.pad(w, ((0, 0), (0, n - w.shape[1])))


def _row_prompt(i, tm):
    return 0


def _make_row_sample(s_len):
    def row(i, tm):
        return 1 + (i * tm) // s_len
    return row


def kernel(x_prompt, x_sample, cache_k_l0, cache_v_l0, state_mlstm_c_l0, state_mlstm_n_l0, state_mlstm_m_l0, state_ssd_l1, c, c_ctx, ada_w_l0, ada_b_l0, mix_w_in_l0, mlstm_gate_b_l0, mlstm_norm_g_l0, q_norm_g_l0, k_norm_g_l0, mix_w_out_l0, ln1_g_l0, ln1_b_l0, ln2_g_l0, ln2_b_l0, ffn_w1_l0, ffn_w3_l0, ffn_w2_l0, ada_w_l1, ada_b_l1, ssd_w_in_l1, ssd_conv_w_l1, ssd_conv_b_l1, ssd_dt_bias_l1, ssd_a_log_l1, ssd_d_l1, ssd_norm_g_l1, ssd_w_out_l1, ln1_g_l1, ln1_b_l1, ln2_g_l1, ln2_b_l1, router_w_l1, router_b_l1, moe_w1_l1, moe_w3_l1, moe_w2_l1):
    bp, sp, _ = x_prompt.shape
    bs, ss, _ = x_sample.shape
    past = cache_k_l0.shape[1]
    groups = [
        dict(x=x_prompt.reshape(bp * sp, D), b=bp, s=sp, row=_row_prompt, prompt=True),
        dict(x=x_sample.reshape(bs * ss, D), b=bs, s=ss, row=_make_row_sample(ss), prompt=False),
    ]

    cvec = jnp.zeros((16, D), F32).at[0].set(c_ctx).at[1:1 + bs].set(c)
    mods0 = _ada(cvec, ada_w_l0, ada_b_l0).reshape(16, 1, 6 * D)
    mods1 = _ada(cvec, ada_w_l1, ada_b_l1).reshape(16, 1, 6 * D)

    w0_main = jnp.concatenate([mix_w_in_l0[:, :2048], mix_w_in_l0[:, 2080:]], axis=1).astype(BF16)
    w0_aux_hi, w0_aux_lo = _split2(_pad_cols(mix_w_in_l0[:, 2048:2080], LANES))
    gbias = _pad_cols(mlstm_gate_b_l0.reshape(1, 4 * H_A), LANES)
    avg128 = _block_avg(LANES)
    q_g = jnp.tile(q_norm_g_l0, W_A // HD).reshape(1, W_A)
    k_g = jnp.tile(k_norm_g_l0, KV_B).reshape(1, LANES)
    w_out_h = mix_w_out_l0[:W_A].astype(BF16)
    w_out_o = mix_w_out_l0[W_A:].astype(BF16)
    ffn_w1 = ffn_w1_l0.astype(BF16)
    ffn_w3 = ffn_w3_l0.astype(BF16)
    ffn_w2 = ffn_w2_l0.astype(BF16)
    tables = _rope_tables(ss)
    kc = cache_k_l0.transpose(2, 0, 1, 3).reshape(KV_B, bs * past, HD).astype(BF16)
    vc = cache_v_l0.transpose(2, 0, 1, 3).reshape(KV_B, bs * past, HD)
    vc = jnp.concatenate([vc, jnp.ones_like(vc)], axis=-1).astype(BF16)
    c0 = jnp.concatenate([jnp.swapaxes(state_mlstm_c_l0, -1, -2),
                          jnp.broadcast_to(state_mlstm_n_l0[..., None, :], state_mlstm_c_l0.shape)], axis=-2)
    m0 = jnp.broadcast_to(state_mlstm_m_l0[..., None], state_mlstm_m_l0.shape + (LANES,))

    n_main1 = 2 * D_INNER + 2 * NG_C * D_STATE
    w1_main = ssd_w_in_l1[:, :n_main1].astype(BF16)
    w1_aux_hi, w1_aux_lo = _split2(_pad_cols(ssd_w_in_l1[:, n_main1:], LANES))
    dtb = _pad_cols(ssd_dt_bias_l1.reshape(1, 2 * H_C), LANES)
    alog = _pad_cols(ssd_a_log_l1.reshape(1, 2 * H_C), LANES)
    dsk = jnp.broadcast_to(jnp.repeat(ssd_d_l1, HD)[:, None], (D_INNER, LANES))
    ssd_ng = ssd_norm_g_l1.reshape(1, D_INNER)
    w_out1 = ssd_w_out_l1.astype(BF16)
    rw_hi, rw_lo = _split2(_pad_cols(router_w_l1, LANES))
    rb = jnp.full((1, LANES), NEG, F32).at[0, :N_EXPERTS].set(router_b_l1)
    moe_w1 = moe_w1_l1.astype(BF16)
    moe_w3 = moe_w3_l1.astype(BF16)
    moe_w2 = moe_w2_l1.astype(BF16)
    s0 = state_ssd_l1.reshape(bs, 2, NG_C, HPG_C * HD, D_STATE)

    outs = {}
    for grp in groups:
        x, b, s, row, prompt = grp["x"], grp["b"], grp["s"], grp["row"], grp["prompt"]
        pm, pa = _mm_in(x, mods0, row, w0_main, w0_aux_hi, w0_aux_lo, tm=512, n_chunks=2)
        if prompt:
            hn, cst, mst = _mlstm(pm, pa, gbias, mlstm_norm_g_l0.reshape(1, W_A), b, s, out_state=True)
            qn, kh, vh, kn = _qkprep(pm, q_g, k_g, avg128, s, emit_kn=True)
            o = _attn(qn, kh, vh, b, s)
            outs["k"] = kn.reshape(b, s, KV_B, HD)
            outs["v"] = pm[:, 2688:2816].astype(F32).reshape(b, s, KV_B, HD)
            outs["c"] = jnp.swapaxes(cst[..., :HD, :], -1, -2)
            outs["n"] = cst[..., HD, :]
            outs["m"] = mst[..., 0]
        else:
            hn = _mlstm(pm, pa, gbias, mlstm_norm_g_l0.reshape(1, W_A), b, s, init=(c0, m0))[0]
            qn, kh, vh = _qkprep(pm, q_g, k_g, avg128, s, tables=tables)
            o = _attn(qn, kh, vh, b, s, ctx=(kc, vc))
        x1, u2 = _out_proj([hn, o], [w_out_h, w_out_o], x, mods0, row, ln1_g_l0, ln1_b_l0)
        x2 = _ffn(u2, ffn_w1, ffn_w3, ffn_w2, x1, mods0, row, ln2_g_l0, ln2_b_l0, tf=D_FF // 2)
        pm, pa = _mm_in(x2, mods1, row, w1_main, w1_aux_hi, w1_aux_lo, tm=512, n_chunks=4)
        xt = _conv(pm, ssd_conv_w_l1[:, :D_INNER], ssd_conv_b_l1[:D_INNER], b, s, D_INNER, True)
        bc = _conv(pm, ssd_conv_w_l1[:, D_INNER:], ssd_conv_b_l1[D_INNER:], b, s, 2 * D_INNER, False)
        if prompt:
            yn, sst = _ssd(xt, bc, pa, pm, dtb, alog, dsk, ssd_ng, b, s, out_state=True)
            outs["s"] = sst.reshape(b, 2, H_C, HD, D_STATE)
        else:
            yn = _ssd(xt, bc, pa, pm, dtb, alog, dsk, ssd_ng, b, s, init=s0)[0]
        x1, u2, gate = _out_proj([yn], [w_out1], x2, mods1, row, ln1_g_l1, ln1_b_l1, router=(rw_hi, rw_lo, rb))
        x3 = _moe(u2, moe_w1, moe_w3, moe_w2, gate, x1, mods1, row, ln2_g_l1, ln2_b_l1,
                  tm=min(1024, b * s if prompt else s))
        outs["y_p" if prompt else "y_s"] = x3.reshape(b, s, D)

    return (outs["y_p"], outs["y_s"], outs["k"], outs["v"], outs["c"], outs["n"], outs["m"], outs["s"])
```

```python
import functools

import jax
import jax.numpy as jnp
import numpy as np
from jax import lax
from jax.experimental import pallas as pl
from jax.experimental.pallas import tpu as pltpu

F32 = jnp.float32
BF16 = jnp.bfloat16

D = 1024
CH = 128
LANES = 128
EPS = 1e-6
DEPTH = 2
ALPHA = (2 * DEPTH) ** 0.25
GRID_W = 64
ROPE_THETA = 10000.0
H_A = 8
HD = 64
W_A = 512
KV_B = 2
H_C = 32
NG_C = 4
HPG_C = 8
D_STATE = 128
D_INNER = 2048
D_FF = 2816
N_EXPERTS = 8
D_FF_E = 1408
NEG = -1e30
LOG2E = 1.4426950408889634
VMEM_LIMIT = 56 * 1024 * 1024


def _cp(*sem):
    return pltpu.CompilerParams(dimension_semantics=sem, vmem_limit_bytes=VMEM_LIMIT)


def _sigmoid(x):
    return 1.0 / (1.0 + jnp.exp(-x))


def _silu(x):
    return x * _sigmoid(x)


def _softplus(x):
    return jnp.maximum(x, 0.0) + jnp.log(1.0 + jnp.exp(-jnp.abs(x)))


def _split2(x):
    hi = x.astype(BF16)
    lo = (x - hi.astype(F32)).astype(BF16)
    return hi, lo


def _split3(x):
    h1 = x.astype(BF16)
    r = x - h1.astype(F32)
    h2 = r.astype(BF16)
    h3 = (r - h2.astype(F32)).astype(BF16)
    return h1, h2, h3


def _dot(a, b):
    return jnp.dot(a, b, preferred_element_type=F32)


def _dot_nt(a, b):
    return lax.dot_general(a, b, (((1,), (1,)), ((), ())), preferred_element_type=F32)


def _tri_dot(tri, x):
    h1, h2, h3 = _split3(x)
    return _dot(tri, h1) + _dot(tri, h2) + _dot(tri, h3)


def _layer_norm(r, g, b):
    mu = jnp.mean(r, axis=-1, keepdims=True)
    d = r - mu
    var = jnp.mean(d * d, axis=-1, keepdims=True)
    return d * lax.rsqrt(var + EPS) * g + b


def _ada_kernel(c_ref, w_ref, b_ref, o_ref):
    s = _silu(c_ref[...])
    sh, sl = _split2(s)
    wh, wl = _split2(w_ref[...])
    o_ref[...] = _dot(sh, wh) + _dot(sl, wh) + _dot(sh, wl) + b_ref[...]


def _ada(cvec, w, b):
    n = w.shape[1]
    tn = 768
    return pl.pallas_call(
        _ada_kernel,
        grid=(n // tn,),
        in_specs=[pl.BlockSpec((16, D), lambda j: (0, 0)),
                  pl.BlockSpec((D, tn), lambda j: (0, j)),
                  pl.BlockSpec((1, tn), lambda j: (0, j))],
        out_specs=pl.BlockSpec((16, tn), lambda j: (0, j)),
        out_shape=jax.ShapeDtypeStruct((16, n), F32),
        compiler_params=_cp("parallel"),
        name="ada_mods",
    )(cvec, w, b.reshape(1, n))


def _mm_in_kernel(x_ref, sh_ref, sc_ref, wm_ref, wah_ref, wal_ref, om_ref, oa_ref, *, n_chunks):
    u = x_ref[...] * (1.0 + sc_ref[...]) + sh_ref[...]
    uh, ul = _split2(u)
    wah = wah_ref[...]
    oa_ref[...] = _dot(uh, wah) + _dot(ul, wah) + _dot(uh, wal_ref[...])
    tn = wm_ref.shape[1] // n_chunks
    for c in range(n_chunks):
        om_ref[:, c * tn:(c + 1) * tn] = _dot(uh, wm_ref[:, c * tn:(c + 1) * tn]).astype(om_ref.dtype)


def _mm_in(x, mods3, row_fn, wm, wa_hi, wa_lo, tm, n_chunks):
    t, nm = x.shape[0], wm.shape[1]
    once = pl.Buffered(1)
    return pl.pallas_call(
        functools.partial(_mm_in_kernel, n_chunks=n_chunks),
        grid=(t // tm,),
        in_specs=[pl.BlockSpec((tm, D), lambda i: (i, 0)),
                  pl.BlockSpec((None, 1, D), lambda i: (row_fn(i, tm), 0, 0)),
                  pl.BlockSpec((None, 1, D), lambda i: (row_fn(i, tm), 0, 1)),
                  pl.BlockSpec((D, nm), lambda i: (0, 0), pipeline_mode=once),
                  pl.BlockSpec((D, LANES), lambda i: (0, 0), pipeline_mode=once),
                  pl.BlockSpec((D, LANES), lambda i: (0, 0), pipeline_mode=once)],
        out_specs=[pl.BlockSpec((tm, nm), lambda i: (i, 0)),
                   pl.BlockSpec((tm, LANES), lambda i: (i, 0))],
        out_shape=[jax.ShapeDtypeStruct((t, nm), BF16), jax.ShapeDtypeStruct((t, LANES), F32)],
        compiler_params=_cp("parallel"),
        name="mm_in",
    )(x, mods3, mods3, wm, wa_hi, wa_lo)


def _mlstm_kernel(*refs, nc, has_init, out_state):
    q_ref, k_ref, v_ref, om_ref, g_ref, gn_ref, gb_ref, ng_ref = refs[:8]
    pos = 8
    if has_init:
        c0_ref, m0_ref = refs[pos:pos + 2]
        pos += 2
    hn_ref = refs[pos]
    pos += 1
    if out_state:
        cst_ref, mst_ref = refs[pos:pos + 2]
        pos += 2
    c_scr, m_scr, hb_scr, r_scr, rows_scr = refs[pos:pos + 5]

    j = pl.program_id(1)
    is_fwd = j >= nc
    ci = jnp.where(is_fwd, j - nc, nc - 1 - j)
    sgn = jnp.where(is_fwd, 1, -1)

    def _init(d):
        if has_init:
            c_scr[...] = c0_ref[d]
            m_scr[...] = m0_ref[d]
        else:
            c_scr[...] = jnp.zeros(c_scr.shape, F32)
            m_scr[...] = jnp.zeros(m_scr.shape, F32)

    pl.when(j == 0)(lambda: _init(1))
    pl.when(j == nc)(lambda: _init(0))

    def _prep(graw_ref, sg, slot):
        row = lax.broadcasted_iota(jnp.int32, (CH, CH), 0)
        col = lax.broadcasted_iota(jnp.int32, (CH, CH), 1)
        tri = jnp.where((col - row) * sg <= 0, 1.0, 0.0).astype(BF16)
        f_mat = (jnp.zeros((CH, LANES), jnp.int32) + sg) > 0
        g = graw_ref[...] + gb_ref[...]
        li = jnp.where(f_mat, g, pltpu.roll(g, LANES - 16, axis=1))
        lfr = jnp.where(f_mat, pltpu.roll(g, LANES - 8, axis=1), pltpu.roll(g, LANES - 24, axis=1))
        lf = jnp.minimum(lfr, 0.0) - jnp.log(1.0 + jnp.exp(-jnp.abs(lfr)))
        bc = _tri_dot(tri, lf)
        rr = li - bc
        cmx = rr
        for dlt in (1, 2, 4, 8, 16, 32, 64):
            before = jnp.where(row >= dlt, pltpu.roll(cmx, dlt, axis=0), NEG)
            after = jnp.where(row < CH - dlt, pltpu.roll(cmx, CH - dlt, axis=0), NEG)
            cmx = jnp.maximum(cmx, jnp.where(f_mat, before, after))
        r_scr[slot] = rr * LOG2E
        rows_scr[slot, 0] = rr.T[0:H_A, :]
        rows_scr[slot, 1] = bc.T[0:H_A, :]
        rows_scr[slot, 2] = cmx.T[0:H_A, :]

    pl.when(j == 0)(lambda: _prep(g_ref, sgn, 0))

    slot = j % 2
    row = lax.broadcasted_iota(jnp.int32, (CH, CH), 0)
    col = lax.broadcasted_iota(jnp.int32, (CH, CH), 1)
    causal_t = (row - col) * sgn <= 0
    fwd8 = (jnp.zeros((H_A, LANES), jnp.int32) + sgn) > 0
    r = r_scr[slot]
    r_t = rows_scr[slot, 0]
    b_t = rows_scr[slot, 1]
    cm = rows_scr[slot, 2]
    _prep(gn_ref, jnp.where(j + 1 >= nc, 1, -1), (j + 1) % 2)

    m_prev = m_scr[...]
    g_t = jnp.maximum(m_prev, cm)
    mt_t = b_t + g_t
    wst_t = jnp.exp(m_prev - g_t)
    emt_t = jnp.exp(-mt_t)
    fwd_c = fwd8[:, 0:1]
    m_new = jnp.where(fwd_c, mt_t[:, LANES - 1:LANES], mt_t[:, 0:1])
    b_last = jnp.where(fwd_c, b_t[:, LANES - 1:LANES], b_t[:, 0:1])
    ws_t = jnp.exp(r_t + (b_last - m_new))
    decay = jnp.exp(b_last + m_prev[:, 0:1] - m_new)
    m_scr[...] = jnp.broadcast_to(m_new, (H_A, LANES))

    q_t = q_ref[...].astype(F32).T.astype(BF16)
    v_t = v_ref[...].astype(F32).T
    ones = jnp.ones((HD, CH), F32)

    g2_t = g_t * LOG2E
    heads = range(H_A)
    sls = [slice(h * HD, (h + 1) * HD) for h in heads]
    khs = [k_ref[:, sl] for sl in sls]
    qhs = [q_t[sl, :] for sl in sls]
    vexts = [jnp.concatenate([v_t[sl, :], ones], axis=0) for sl in sls]
    cexts = [c_scr[h] for h in heads]
    scs = [_dot(khs[h], qhs[h]) for h in heads]
    w_ts = [jnp.exp2(jnp.where(causal_t, r[:, h:h + 1] - g2_t[h:h + 1, :], NEG)) for h in heads]
    sc_ts = [(scs[h] * 0.125 * w_ts[h]).astype(BF16) for h in heads]
    cqs = [_dot(cexts[h].astype(BF16), qhs[h]) for h in heads]
    nds = [_dot(vexts[h].astype(BF16), sc_ts[h]) + (cqs[h] * 0.125) * wst_t[h:h + 1, :] for h in heads]
    hts = [nds[h][0:HD, :] / jnp.maximum(jnp.abs(nds[h][HD:HD + 1, :]), emt_t[h:h + 1, :]) for h in heads]
    for h in heads:
        c_scr[h] = decay[h:h + 1, :] * cexts[h] + _dot((vexts[h] * ws_t[h:h + 1, :]).astype(BF16), khs[h])
    hcat_t = jnp.concatenate(hts, axis=0)

    @pl.when(j < nc)
    def _():
        hb_scr[ci] = hcat_t

    @pl.when(j >= nc)
    def _():
        ht = hcat_t + hb_scr[ci]
        normed = []
        for h in range(H_A):
            hh = ht[h * HD:(h + 1) * HD, :]
            dlt = hh - jnp.mean(hh, axis=0, keepdims=True)
            normed.append(dlt * lax.rsqrt(jnp.mean(dlt * dlt, axis=0, keepdims=True) + EPS))
        hn = jnp.concatenate(normed, axis=0).T * ng_ref[...] * _sigmoid(om_ref[...].astype(F32))
        hn_ref[...] = hn.astype(hn_ref.dtype)

    if out_state:
        @pl.when(j == nc - 1)
        def _():
            cst_ref[1] = c_scr[...]
            mst_ref[1] = m_scr[...]

        @pl.when(j == 2 * nc - 1)
        def _():
            cst_ref[0] = c_scr[...]
            mst_ref[0] = m_scr[...]


def _mlstm(pm, pa, gbias, norm_g, bsz, s_len, init=None, out_state=False):
    nc = s_len // CH
    t = bsz * s_len

    def chunk(b, j):
        return b * nc + jnp.where(j >= nc, j - nc, nc - 1 - j)

    def chunk_next(b, j):
        return chunk(b, jnp.minimum(j + 1, 2 * nc - 1))

    def chunk_fwd(b, j):
        return b * nc + jnp.maximum(j - nc, 0)

    in_specs = [pl.BlockSpec((CH, W_A), lambda b, j: (chunk(b, j), 0)),
                pl.BlockSpec((CH, W_A), lambda b, j: (chunk(b, j), 1)),
                pl.BlockSpec((CH, W_A), lambda b, j: (chunk(b, j), 2)),
                pl.BlockSpec((CH, W_A), lambda b, j: (chunk_fwd(b, j), 3)),
                pl.BlockSpec((CH, LANES), lambda b, j: (chunk(b, j), 0)),
                pl.BlockSpec((CH, LANES), lambda b, j: (chunk_next(b, j), 0)),
                pl.BlockSpec((1, LANES), lambda b, j: (0, 0)),
                pl.BlockSpec((1, W_A), lambda b, j: (0, 0))]
    args = [pm, pm, pm, pm, pa, pa, gbias, norm_g]
    if init is not None:
        in_specs += [pl.BlockSpec((None, 2, H_A, LANES, HD), lambda b, j: (b, 0, 0, 0, 0)),
                     pl.BlockSpec((None, 2, H_A, LANES), lambda b, j: (b, 0, 0, 0))]
        args += list(init)
    out_specs = [pl.BlockSpec((CH, W_A), lambda b, j: (chunk_fwd(b, j), 0))]
    out_shape = [jax.ShapeDtypeStruct((t, W_A), BF16)]
    if out_state:
        out_specs += [pl.BlockSpec((None, 2, H_A, LANES, HD), lambda b, j: (b, 0, 0, 0, 0)),
                      pl.BlockSpec((None, 2, H_A, LANES), lambda b, j: (b, 0, 0, 0))]
        out_shape += [jax.ShapeDtypeStruct((bsz, 2, H_A, LANES, HD), F32),
                      jax.ShapeDtypeStruct((bsz, 2, H_A, LANES), F32)]
    return pl.pallas_call(
        functools.partial(_mlstm_kernel, nc=nc, has_init=init is not None, out_state=out_state),
        grid=(bsz, 2 * nc),
        in_specs=in_specs,
        out_specs=out_specs,
        out_shape=out_shape,
        scratch_shapes=[pltpu.VMEM((H_A, LANES, HD), F32), pltpu.VMEM((H_A, LANES), F32),
                        pltpu.VMEM((nc, W_A, CH), F32),
                        pltpu.VMEM((2, CH, LANES), F32), pltpu.VMEM((2, 3, H_A, LANES), F32)],
        compiler_params=_cp("parallel", "arbitrary"),
        name="mlstm",
    )(*args)


def _group_rms(x, avg, g):
    s1, s2 = _split2(x * x)
    ms = _dot(s1, avg) + _dot(s2, avg)
    return x * lax.rsqrt(ms + EPS) * g


def _rope(x, cos, sin):
    lane = lax.broadcasted_iota(jnp.int32, x.shape, 1)
    swapped = jnp.where((lane % 32) < 16, pltpu.roll(x, LANES - 16, axis=1), pltpu.roll(x, 16, axis=1))
    return x * cos + swapped * sin


def _qkprep_kernel(*refs, rope, emit_kn):
    q_ref, k_ref, v_ref, qg_ref, kg_ref, avg_ref = refs[:6]
    pos = 6
    if rope:
        cos_ref, sin_ref = refs[pos:pos + 2]
        pos += 2
    qn_ref, kh_ref, vh_ref = refs[pos:pos + 3]
    pos += 3
    avg = avg_ref[...]
    qs = []
    for c in range(W_A // LANES):
        sl = slice(c * LANES, (c + 1) * LANES)
        xn = _group_rms(q_ref[:, sl].astype(F32), avg, qg_ref[:, sl])
        if rope:
            xn = _rope(xn, cos_ref[...], sin_ref[...])
        qs.append(xn * 0.125)
    qn_ref[...] = jnp.concatenate(qs, axis=1).astype(qn_ref.dtype)
    kn = _group_rms(k_ref[...].astype(F32), avg, kg_ref[...])
    if emit_kn:
        refs[pos][...] = kn
    if rope:
        kn = _rope(kn, cos_ref[...], sin_ref[...])
    v = v_ref[...].astype(F32)
    ones = jnp.ones((v.shape[0], HD), F32)
    for gi in range(KV_B):
        sl = slice(gi * HD, (gi + 1) * HD)
        kh_ref[gi] = kn[:, sl].astype(kh_ref.dtype)
        vh_ref[gi] = jnp.concatenate([v[:, sl], ones], axis=1).astype(vh_ref.dtype)


def _qkprep(pm, q_g, k_g, avg128, s_len, tables=None, emit_kn=False):
    t = pm.shape[0]
    tm = 256
    nt = s_len // tm
    qoff, koff, voff = 2048 // W_A, 2560 // LANES, 2688 // LANES
    in_specs = [pl.BlockSpec((tm, W_A), lambda i: (i, qoff)),
                pl.BlockSpec((tm, LANES), lambda i: (i, koff)),
                pl.BlockSpec((tm, LANES), lambda i: (i, voff)),
                pl.BlockSpec((1, W_A), lambda i: (0, 0)),
                pl.BlockSpec((1, LANES), lambda i: (0, 0)),
                pl.BlockSpec((LANES, LANES), lambda i: (0, 0))]
    args = [pm, pm, pm, q_g, k_g, avg128]
    if tables is not None:
        in_specs += [pl.BlockSpec((tm, LANES), lambda i: (i % nt, 0))] * 2
        args += list(tables)
    out_specs = [pl.BlockSpec((tm, W_A), lambda i: (i, 0)),
                 pl.BlockSpec((KV_B, tm, HD), lambda i: (0, i, 0)),
                 pl.BlockSpec((KV_B, tm, LANES), lambda i: (0, i, 0))]
    out_shape = [jax.ShapeDtypeStruct((t, W_A), BF16),
                 jax.ShapeDtypeStruct((KV_B, t, HD), BF16),
                 jax.ShapeDtypeStruct((KV_B, t, LANES), BF16)]
    if emit_kn:
        out_specs.append(pl.BlockSpec((tm, LANES), lambda i: (i, 0)))
        out_shape.append(jax.ShapeDtypeStruct((t, LANES), F32))
    return pl.pallas_call(
        functools.partial(_qkprep_kernel, rope=tables is not None, emit_kn=emit_kn),
        grid=(t // tm,),
        in_specs=in_specs,
        out_specs=out_specs,
        out_shape=out_shape,
        compiler_params=_cp("parallel"),
        name="qkprep",
    )(*args)


def _attn_kernel(*refs, has_ctx):
    if has_ctx:
        q_ref, k_ref, v_ref, kc_ref, vc_ref, o_ref = refs
    else:
        q_ref, k_ref, v_ref, o_ref = refs
    k = k_ref[...]
    v = v_ref[...]
    hq = range(4)
    qhs = [q_ref[:, h * HD:(h + 1) * HD] for h in hq]
    ss = [_dot_nt(qh, k) for qh in qhs]
    ms = [jnp.max(s, axis=1, keepdims=True) for s in ss]
    if has_ctx:
        scs = [_dot_nt(qh, kc_ref[...]) for qh in qhs]
        ms = [jnp.maximum(m, jnp.max(sc, axis=1, keepdims=True)) for m, sc in zip(ms, scs)]
    oes = [_dot(jnp.exp(s - m).astype(BF16), v) for s, m in zip(ss, ms)]
    if has_ctx:
        oes = [oe + _dot(jnp.exp(sc - m).astype(BF16), vc_ref[...]) for oe, sc, m in zip(oes, scs, ms)]
    outs = [oe[:, :HD] / oe[:, HD:HD + 1] for oe in oes]
    o_ref[...] = jnp.concatenate(outs, axis=1).astype(o_ref.dtype)


def _attn(qn, kh, vh, bsz, s_len, ctx=None):
    t = qn.shape[0]
    tq = min(256, s_len)
    nq = s_len // tq
    in_specs = [pl.BlockSpec((tq, 256), lambda b, g, i: (b * nq + i, g)),
                pl.BlockSpec((None, s_len, HD), lambda b, g, i: (g, b, 0)),
                pl.BlockSpec((None, s_len, LANES), lambda b, g, i: (g, b, 0))]
    args = [qn, kh, vh]
    if ctx is not None:
        past = ctx[0].shape[1] // bsz
        in_specs += [pl.BlockSpec((None, past, HD), lambda b, g, i: (g, b, 0)),
                     pl.BlockSpec((None, past, LANES), lambda b, g, i: (g, b, 0))]
        args += list(ctx)
    return pl.pallas_call(
        functools.partial(_attn_kernel, has_ctx=ctx is not None),
        grid=(bsz, KV_B, nq),
        in_specs=in_specs,
        out_specs=pl.BlockSpec((tq, 256), lambda b, g, i: (b * nq + i, g)),
        out_shape=jax.ShapeDtypeStruct((t, W_A), BF16),
        compiler_params=_cp("parallel", "parallel", "parallel"),
        name="attn",
    )(*args)


def _out_kernel(*refs, n_in, router):
    a_refs = refs[:n_in]
    w_refs = refs[n_in:2 * n_in]
    x_ref, g1_ref, sh2_ref, sc2_ref, lng_ref, lnb_ref = refs[2 * n_in:2 * n_in + 6]
    pos = 2 * n_in + 6
    if router:
        rwh_ref, rwl_ref, rb_ref = refs[pos:pos + 3]
        pos += 3
    x1_ref, u2_ref = refs[pos:pos + 2]
    y = _dot(a_refs[0][...], w_refs[0][...])
    for a_ref, w_ref in zip(a_refs[1:], w_refs[1:]):
        y = y + _dot(a_ref[...], w_ref[...])
    x1 = _layer_norm(ALPHA * x_ref[...] + g1_ref[...] * y, lng_ref[...], lnb_ref[...])
    x1_ref[...] = x1
    u2 = x1 * (1.0 + sc2_ref[...]) + sh2_ref[...]
    uh, ul = _split2(u2)
    u2_ref[...] = uh
    if router:
        gate_ref = refs[pos + 2]
        rwh = rwh_ref[...]
        logits = _dot(uh, rwh) + _dot(ul, rwh) + _dot(uh, rwl_ref[...]) + rb_ref[...]
        lane = lax.broadcasted_iota(jnp.int32, logits.shape, 1)
        m1 = jnp.max(logits, axis=1, keepdims=True)
        i1 = jnp.min(jnp.where(logits == m1, lane, LANES), axis=1, keepdims=True)
        rest = jnp.where(lane == i1, 2.0 * NEG, logits)
        m2 = jnp.max(rest, axis=1, keepdims=True)
        i2 = jnp.min(jnp.where(rest == m2, lane, LANES), axis=1, keepdims=True)
        e = jnp.exp(m2 - m1)
        p1 = 1.0 / (1.0 + e)
        gate_ref[...] = jnp.where(lane == i1, p1, jnp.where(lane == i2, e * p1, 0.0))


def _out_proj(acts, ws, x, mods3, row_fn, ln_g, ln_b, router=None):
    t = x.shape[0]
    tm = 512
    n_in = len(acts)
    in_specs = [pl.BlockSpec((tm, a.shape[1]), lambda i: (i, 0)) for a in acts]
    in_specs += [pl.BlockSpec(w.shape, lambda i: (0, 0)) for w in ws]
    in_specs += [pl.BlockSpec((tm, D), lambda i: (i, 0))]
    in_specs += [pl.BlockSpec((None, 1, D), functools.partial(lambda i, k: (row_fn(i, tm), 0, k), k=k))
                 for k in (2, 3, 4)]
    in_specs += [pl.BlockSpec((1, D), lambda i: (0, 0))] * 2
    args = list(acts) + list(ws) + [x, mods3, mods3, mods3, ln_g.reshape(1, D), ln_b.reshape(1, D)]
    out_specs = [pl.BlockSpec((tm, D), lambda i: (i, 0)), pl.BlockSpec((tm, D), lambda i: (i, 0))]
    out_shape = [jax.ShapeDtypeStruct((t, D), F32), jax.ShapeDtypeStruct((t, D), BF16)]
    if router is not None:
        in_specs += [pl.BlockSpec((D, LANES), lambda i: (0, 0))] * 2 + [pl.BlockSpec((1, LANES), lambda i: (0, 0))]
        args += list(router)
        out_specs.append(pl.BlockSpec((tm, LANES), lambda i: (i, 0)))
        out_shape.append(jax.ShapeDtypeStruct((t, LANES), F32))
    return pl.pallas_call(
        functools.partial(_out_kernel, n_in=n_in, router=router is not None),
        grid=(t // tm,),
        in_specs=in_specs,
        out_specs=out_specs,
        out_shape=out_shape,
        compiler_params=_cp("parallel"),
        name="out_proj",
    )(*args)


def _ffn_kernel(u_ref, w1_ref, w3_ref, w2_ref, x1_ref, g2_ref, lng_ref, lnb_ref, o_ref, acc_scr):
    j = pl.program_id(1)

    @pl.when(j == 0)
    def _():
        acc_scr[...] = jnp.zeros(acc_scr.shape, F32)

    u = u_ref[...]
    h = _silu(_dot(u, w1_ref[...])) * _dot(u, w3_ref[...])
    acc_scr[...] += _dot(h.astype(BF16), w2_ref[...])

    @pl.when(j == pl.num_programs(1) - 1)
    def _():
        o_ref[...] = _layer_norm(ALPHA * x1_ref[...] + g2_ref[...] * acc_scr[...], lng_ref[...], lnb_ref[...])


def _ffn(u2, w1, w3, w2, x1, mods3, row_fn, ln_g, ln_b, tf):
    t = u2.shape[0]
    tm = 512
    f = w1.shape[1]
    return pl.pallas_call(
        _ffn_kernel,
        grid=(t // tm, f // tf),
        in_specs=[pl.BlockSpec((tm, D), lambda i, j: (i, 0)),
                  pl.BlockSpec((D, tf), lambda i, j: (0, j)),
                  pl.BlockSpec((D, tf), lambda i, j: (0, j)),
                  pl.BlockSpec((tf, D), lambda i, j: (j, 0)),
                  pl.BlockSpec((tm, D), lambda i, j: (i, 0)),
                  pl.BlockSpec((None, 1, D), lambda i, j: (row_fn(i, tm), 0, 5)),
                  pl.BlockSpec((1, D), lambda i, j: (0, 0)),
                  pl.BlockSpec((1, D), lambda i, j: (0, 0))],
        out_specs=pl.BlockSpec((tm, D), lambda i, j: (i, 0)),
        out_shape=jax.ShapeDtypeStruct((t, D), F32),
        scratch_shapes=[pltpu.VMEM((tm, D), F32)],
        compiler_params=_cp("parallel", "arbitrary"),
        name="ffn",
    )(u2, w1, w3, w2, x1, mods3, ln_g.reshape(1, D), ln_b.reshape(1, D))


MOE_CAP = 256


def _moe_kernel(u_ref, w1_ref, w3_ref, w2_ref, gate_ref, tril_ref, x1_ref, g2_ref, lng_ref, lnb_ref, o_ref,
                acc_scr, rank_scr, rank_t_scr, gate_t_scr):
    e = pl.program_id(1)
    tm = u_ref.shape[0]

    @pl.when(e == 0)
    def _():
        acc_scr[...] = jnp.zeros(acc_scr.shape, F32)
        gate = gate_ref[...]
        routed = gate > 0.0
        rank = jnp.where(routed, _dot(tril_ref[...], jnp.where(routed, 1.0, 0.0).astype(BF16)), -1.0)
        rank_scr[...] = rank
        rank_t_scr[...] = rank.T
        gate_t_scr[...] = gate.T

    lane = lax.broadcasted_iota(jnp.int32, (tm, LANES), 1)
    rank_col = jnp.sum(jnp.where(lane == e, rank_scr[...], 0.0), axis=1, keepdims=True)
    rank_row = rank_t_scr[pl.ds(e, 1), :]
    gate_row = gate_t_scr[pl.ds(e, 1), :]
    count = jnp.sum(jnp.where(rank_row >= 0.0, 1, 0))
    u = u_ref[...]

    def one_pass(first_slot, cap):
        off = first_slot.astype(F32)
        slot_c = lax.broadcasted_iota(jnp.int32, (cap, tm), 0).astype(F32) + off
        sel = jnp.where(rank_row == slot_c, 1.0, 0.0)
        weight = jnp.sum(sel * gate_row, axis=1, keepdims=True)
        xe = _dot(sel.astype(BF16), u).astype(BF16)
        h = _silu(_dot(xe, w1_ref[...])) * _dot(xe, w3_ref[...])
        y = _dot(h.astype(BF16), w2_ref[...]) * weight
        slot_r = lax.broadcasted_iota(jnp.int32, (tm, cap), 1).astype(F32) + off
        sel_t = jnp.where(rank_col == slot_r, 1.0, 0.0).astype(BF16)
        acc_scr[...] += _dot(sel_t, y.astype(BF16))

    n_full = count // MOE_CAP
    left = count - n_full * MOE_CAP
    n_full = n_full + jnp.where(left > MOE_CAP // 2, 1, 0)

    def full_pass(blk, carry):
        one_pass(blk * MOE_CAP, MOE_CAP)
        return carry

    lax.fori_loop(0, n_full, full_pass, 0)

    @pl.when(jnp.logical_and(left > 0, left <= MOE_CAP // 2))
    def _():
        one_pass(n_full * MOE_CAP, MOE_CAP // 2)

    @pl.when(e == pl.num_programs(1) - 1)
    def _():
        o_ref[...] = _layer_norm(ALPHA * x1_ref[...] + g2_ref[...] * acc_scr[...], lng_ref[...], lnb_ref[...])


def _moe(u2, w1, w3, w2, gate, x1, mods3, row_fn, ln_g, ln_b, tm):
    t = u2.shape[0]
    n_e, _, f = w1.shape
    tril = jnp.tril(jnp.ones((tm, tm), F32), -1).astype(BF16)
    once = pl.Buffered(1)
    in_specs = [pl.BlockSpec((tm, D), lambda i, e: (i, 0), pipeline_mode=once),
                pl.BlockSpec((None, D, f), lambda i, e: (e, 0, 0)),
                pl.BlockSpec((None, D, f), lambda i, e: (e, 0, 0)),
                pl.BlockSpec((None, f, D), lambda i, e: (e, 0, 0)),
                pl.BlockSpec((tm, LANES), lambda i, e: (i, 0), pipeline_mode=once),
                pl.BlockSpec((tm, tm), lambda i, e: (0, 0), pipeline_mode=once),
                pl.BlockSpec((tm, D), lambda i, e: (i, 0), pipeline_mode=once),
                pl.BlockSpec((None, 1, D), lambda i, e: (row_fn(i, tm), 0, 5)),
                pl.BlockSpec((1, D), lambda i, e: (0, 0)),
                pl.BlockSpec((1, D), lambda i, e: (0, 0))]
    return pl.pallas_call(
        _moe_kernel,
        grid=(t // tm, n_e),
        in_specs=in_specs,
        out_specs=pl.BlockSpec((tm, D), lambda i, e: (i, 0)),
        out_shape=jax.ShapeDtypeStruct((t, D), F32),
        scratch_shapes=[pltpu.VMEM((tm, D), F32), pltpu.VMEM((tm, LANES), F32),
                        pltpu.VMEM((LANES, tm), F32), pltpu.VMEM((LANES, tm), F32)],
        compiler_params=_cp("parallel", "arbitrary"),
        name="moe",
    )(u2, w1, w3, w2, gate, tril, x1, mods3, ln_g.reshape(1, D), ln_b.reshape(1, D))


def _conv_kernel(x_ref, w_ref, b_ref, o_ref, *, s_len, transpose):
    x = x_ref[...].astype(F32)
    rows = x.shape[0]
    t = lax.broadcasted_iota(jnp.int32, x.shape, 0) % s_len
    prev = jnp.where(t == 0, 0.0, pltpu.roll(x, 1, axis=0))
    nxt = jnp.where(t == s_len - 1, 0.0, pltpu.roll(x, rows - 1, axis=0))
    y = _silu(w_ref[0:1, :] * prev + w_ref[1:2, :] * x + w_ref[2:3, :] * nxt + b_ref[...])
    if transpose:
        for q in range(rows // s_len):
            o_ref[q] = y[q * s_len:(q + 1) * s_len, :].T.astype(o_ref.dtype)
    else:
        o_ref[...] = y.astype(o_ref.dtype)


def _conv(pm, conv_w, conv_b, bsz, s_len, col_off, transpose):
    t = pm.shape[0]
    tc = 512
    n_ch = conv_w.shape[1]
    off = col_off // tc
    n_seq = max(1, min(bsz, 2048 // s_len))
    rows = n_seq * s_len
    if transpose:
        out_spec = pl.BlockSpec((n_seq, tc, s_len), lambda b, c: (b, c, 0))
        out_shape = jax.ShapeDtypeStruct((bsz, n_ch, s_len), BF16)
    else:
        out_spec = pl.BlockSpec((rows, tc), lambda b, c: (b, c))
        out_shape = jax.ShapeDtypeStruct((t, n_ch), BF16)
    return pl.pallas_call(
        functools.partial(_conv_kernel, s_len=s_len, transpose=transpose),
        grid=(bsz // n_seq, n_ch // tc),
        in_specs=[pl.BlockSpec((rows, tc), lambda b, c: (b, off + c)),
                  pl.BlockSpec((3, tc), lambda b, c: (0, c)),
                  pl.BlockSpec((1, tc), lambda b, c: (0, c))],
        out_specs=out_spec,
        out_shape=out_shape,
        compiler_params=_cp("parallel", "parallel"),
        name="ssd_conv",
    )(pm, conv_w, conv_b.reshape(1, n_ch))


def _ssd_kernel(*refs, nc, has_init, out_state):
    xt_ref, b_ref, c_ref, dt_ref, dtn_ref, z_ref, dtb_ref, alog_ref, dsk_ref, ng_ref = refs[:10]
    pos = 10
    if has_init:
        s0_ref = refs[pos]
        pos += 1
    yn_ref = refs[pos]
    pos += 1
    if out_state:
        sst_ref = refs[pos]
        pos += 1
    s_scr, yb_scr, yt_scr, xw_scr, cum_scr, rows_scr = refs[pos:pos + 6]

    j = pl.program_id(1)
    is_fwd = j >= nc
    ci = jnp.where(is_fwd, j - nc, nc - 1 - j)
    sgn = jnp.where(is_fwd, 1, -1)

    def _init(d):
        if has_init:
            s_scr[...] = s0_ref[d]
        else:
            s_scr[...] = jnp.zeros(s_scr.shape, F32)

    pl.when(j == 0)(lambda: _init(1))
    pl.when(j == nc)(lambda: _init(0))

    def _prep(dtraw_ref, sg, slot):
        row = lax.broadcasted_iota(jnp.int32, (CH, CH), 0)
        col = lax.broadcasted_iota(jnp.int32, (CH, CH), 1)
        tri = jnp.where((col - row) * sg <= 0, 1.0, 0.0).astype(BF16)
        f_row = (jnp.zeros((1, LANES), jnp.int32) + sg) > 0
        f_mat = (jnp.zeros((CH, LANES), jnp.int32) + sg) > 0
        f_hc = (jnp.zeros((H_C, LANES), jnp.int32) + sg) > 0
        raw = dtraw_ref[...] + dtb_ref[...]
        raw = jnp.where(f_mat, raw, pltpu.roll(raw, LANES - H_C, axis=1))
        alog = alog_ref[...]
        alog = jnp.where(f_row, alog, pltpu.roll(alog, LANES - H_C, axis=1))
        dtv = _softplus(raw)
        cumv = _tri_dot(tri, dtv * (-jnp.exp(alog)))
        cumv_t = cumv.T[0:H_C, :]
        dtv_t = dtv.T[0:H_C, :]
        last = jnp.where(f_hc, jnp.broadcast_to(cumv_t[:, CH - 1:CH], (H_C, LANES)),
                         jnp.broadcast_to(cumv_t[:, 0:1], (H_C, LANES)))
        cum_scr[slot] = cumv * LOG2E
        rows_scr[slot, 0] = cumv_t * LOG2E
        rows_scr[slot, 1] = dtv_t
        rows_scr[slot, 2] = jnp.exp(cumv_t)
        rows_scr[slot, 3] = jnp.exp(last - cumv_t) * dtv_t
        rows_scr[slot, 4] = jnp.exp(last)

    pl.when(j == 0)(lambda: _prep(dt_ref, sgn, 0))

    slot = j % 2
    row = lax.broadcasted_iota(jnp.int32, (CH, CH), 0)
    col = lax.broadcasted_iota(jnp.int32, (CH, CH), 1)
    causal_t = (row - col) * sgn <= 0
    cum = cum_scr[slot]
    cum_t = rows_scr[slot, 0]
    dt_t = rows_scr[slot, 1]
    ecum_t = rows_scr[slot, 2]
    ws_t = rows_scr[slot, 3]
    dec = rows_scr[slot, 4]
    _prep(dtn_ref, jnp.where(j + 1 >= nc, 1, -1), (j + 1) % 2)

    for gi in range(NG_C):
        gsl = slice(gi * D_STATE, (gi + 1) * D_STATE)
        bg = b_ref[:, gsl]
        cg = c_ref[:, gsl]
        cb_t = _dot_nt(bg, cg)
        ys_t = _dot_nt(s_scr[gi].astype(BF16), cg)
        hr = range(HPG_C)
        rs = [gi * HPG_C + rr for rr in hr]
        w_ts = [(cb_t * jnp.exp2(jnp.where(causal_t, cum_t[r:r + 1, :] - cum[:, r:r + 1], NEG))).astype(BF16)
                for r in rs]
        xr_ts = [xt_ref[r * HD:(r + 1) * HD, :].astype(F32) for r in rs]
        xd_ts = [(xr_ts[rr] * dt_t[rs[rr]:rs[rr] + 1, :]).astype(BF16) for rr in hr]
        y_ts = [_dot(xd_ts[rr], w_ts[rr]) + ys_t[rr * HD:(rr + 1) * HD, :] * ecum_t[rs[rr]:rs[rr] + 1, :]
                for rr in hr]
        for rr in hr:
            yt_scr[rs[rr] * HD:(rs[rr] + 1) * HD, :] = y_ts[rr]
            xw_scr[rr * HD:(rr + 1) * HD, :] = (xr_ts[rr] * ws_t[rs[rr]:rs[rr] + 1, :]).astype(BF16)
        upd = _dot(xw_scr[...], bg)
        for rr in range(HPG_C):
            r = gi * HPG_C + rr
            hsl = slice(rr * HD, (rr + 1) * HD)
            s_scr[gi, hsl, :] = s_scr[gi, hsl, :] * dec[r:r + 1, :] + upd[hsl, :]

    @pl.when(j < nc)
    def _():
        yb_scr[ci] = yt_scr[...]

    @pl.when(j >= nc)
    def _():
        yt = yt_scr[...] + yb_scr[ci] + dsk_ref[...] * xt_ref[...].astype(F32)
        y = yt.T * _silu(z_ref[...].astype(F32))
        outs = []
        gw = D_INNER // NG_C
        for gi in range(NG_C):
            yg = y[:, gi * gw:(gi + 1) * gw]
            outs.append(yg * lax.rsqrt(jnp.mean(yg * yg, axis=1, keepdims=True) + EPS))
        yn_ref[...] = (jnp.concatenate(outs, axis=1) * ng_ref[...]).astype(yn_ref.dtype)

    if out_state:
        @pl.when(j == nc - 1)
        def _():
            sst_ref[1] = s_scr[...]

        @pl.when(j == 2 * nc - 1)
        def _():
            sst_ref[0] = s_scr[...]


def _ssd(xt, bc, pa, pm, dtb, alog, dsk, norm_g, bsz, s_len, init=None, out_state=False):
    nc = s_len // CH
    t = bsz * s_len
    gw = NG_C * D_STATE

    def chunk_i(j):
        return jnp.where(j >= nc, j - nc, nc - 1 - j)

    def chunk(b, j):
        return b * nc + chunk_i(j)

    def chunk_next(b, j):
        return chunk(b, jnp.minimum(j + 1, 2 * nc - 1))

    def chunk_fwd(b, j):
        return b * nc + jnp.maximum(j - nc, 0)

    in_specs = [pl.BlockSpec((None, D_INNER, CH), lambda b, j: (b, 0, chunk_i(j))),
                pl.BlockSpec((CH, gw), lambda b, j: (chunk(b, j), 0)),
                pl.BlockSpec((CH, gw), lambda b, j: (chunk(b, j), 1)),
                pl.BlockSpec((CH, LANES), lambda b, j: (chunk(b, j), 0)),
                pl.BlockSpec((CH, LANES), lambda b, j: (chunk_next(b, j), 0)),
                pl.BlockSpec((CH, D_INNER), lambda b, j: (chunk_fwd(b, j), 0)),
                pl.BlockSpec((1, LANES), lambda b, j: (0, 0)),
                pl.BlockSpec((1, LANES), lambda b, j: (0, 0)),
                pl.BlockSpec((D_INNER, LANES), lambda b, j: (0, 0)),
                pl.BlockSpec((1, D_INNER), lambda b, j: (0, 0))]
    args = [xt, bc, bc, pa, pa, pm, dtb, alog, dsk, norm_g]
    state_spec = pl.BlockSpec((None, 2, NG_C, HPG_C * HD, D_STATE), lambda b, j: (b, 0, 0, 0, 0))
    if init is not None:
        in_specs.append(state_spec)
        args.append(init)
    out_specs = [pl.BlockSpec((CH, D_INNER), lambda b, j: (chunk_fwd(b, j), 0))]
    out_shape = [jax.ShapeDtypeStruct((t, D_INNER), BF16)]
    if out_state:
        out_specs.append(state_spec)
        out_shape.append(jax.ShapeDtypeStruct((bsz, 2, NG_C, HPG_C * HD, D_STATE), F32))
    return pl.pallas_call(
        functools.partial(_ssd_kernel, nc=nc, has_init=init is not None, out_state=out_state),
        grid=(bsz, 2 * nc),
        in_specs=in_specs,
        out_specs=out_specs,
        out_shape=out_shape,
        scratch_shapes=[pltpu.VMEM((NG_C, HPG_C * HD, D_STATE), F32),
                        pltpu.VMEM((nc, D_INNER, CH), F32),
                        pltpu.VMEM((D_INNER, CH), F32),
                        pltpu.VMEM((HPG_C * HD, CH), BF16),
                        pltpu.VMEM((2, CH, LANES), F32), pltpu.VMEM((2, 5, H_C, LANES), F32)],
        compiler_params=_cp("parallel", "arbitrary"),
        name="ssd_scan",
    )(*args)


def _rope_tables(s_len):
    nf = HD // 4
    inv = ROPE_THETA ** (-jnp.arange(nf, dtype=F32) / nf)
    tpos = jnp.arange(s_len)
    ang_r = (tpos // GRID_W).astype(F32)[:, None] * inv
    ang_c = (tpos % GRID_W).astype(F32)[:, None] * inv
    cos = jnp.concatenate([jnp.cos(ang_r)] * 2 + [jnp.cos(ang_c)] * 2, axis=1)
    sin = jnp.concatenate([-jnp.sin(ang_r), jnp.sin(ang_r), -jnp.sin(ang_c), jnp.sin(ang_c)], axis=1)
    return jnp.tile(cos, (1, 2)), jnp.tile(sin, (1, 2))


def _block_avg(n):
    idx = np.arange(n) // HD
    return jnp.asarray((idx[:, None] == idx[None, :]).astype(np.float32) / HD, dtype=BF16)


def _pad_cols(w, n):
    return jnp.pad(w, ((0, 0), (0, n - w.shape[1])))


def _row_prompt(i, tm):
    return 0


def _make_row_sample(s_len):
    def row(i, tm):
        return 1 + (i * tm) // s_len
    return row


def kernel(x_prompt, x_sample, cache_k_l0, cache_v_l0, state_mlstm_c_l0, state_mlstm_n_l0, state_mlstm_m_l0, state_ssd_l1, c, c_ctx, ada_w_l0, ada_b_l0, mix_w_in_l0, mlstm_gate_b_l0, mlstm_norm_g_l0, q_norm_g_l0, k_norm_g_l0, mix_w_out_l0, ln1_g_l0, ln1_b_l0, ln2_g_l0, ln2_b_l0, ffn_w1_l0, ffn_w3_l0, ffn_w2_l0, ada_w_l1, ada_b_l1, ssd_w_in_l1, ssd_conv_w_l1, ssd_conv_b_l1, ssd_dt_bias_l1, ssd_a_log_l1, ssd_d_l1, ssd_norm_g_l1, ssd_w_out_l1, ln1_g_l1, ln1_b_l1, ln2_g_l1, ln2_b_l1, router_w_l1, router_b_l1, moe_w1_l1, moe_w3_l1, moe_w2_l1):
    bp, sp, _ = x_prompt.shape
    bs, ss, _ = x_sample.shape
    past = cache_k_l0.shape[1]
    groups = [
        dict(x=x_prompt.reshape(bp * sp, D), b=bp, s=sp, row=_row_prompt, prompt=True),
        dict(x=x_sample.reshape(bs * ss, D), b=bs, s=ss, row=_make_row_sample(ss), prompt=False),
    ]

    cvec = jnp.zeros((16, D), F32).at[0].set(c_ctx).at[1:1 + bs].set(c)
    mods0 = _ada(cvec, ada_w_l0, ada_b_l0).reshape(16, 1, 6 * D)
    mods1 = _ada(cvec, ada_w_l1, ada_b_l1).reshape(16, 1, 6 * D)

    w0_main = jnp.concatenate([mix_w_in_l0[:, :2048], mix_w_in_l0[:, 2080:]], axis=1).astype(BF16)
    w0_aux_hi, w0_aux_lo = _split2(_pad_cols(mix_w_in_l0[:, 2048:2080], LANES))
    gbias = _pad_cols(mlstm_gate_b_l0.reshape(1, 4 * H_A), LANES)
    avg128 = _block_avg(LANES)
    q_g = jnp.tile(q_norm_g_l0, W_A // HD).reshape(1, W_A)
    k_g = jnp.tile(k_norm_g_l0, KV_B).reshape(1, LANES)
    w_out_h = mix_w_out_l0[:W_A].astype(BF16)
    w_out_o = mix_w_out_l0[W_A:].astype(BF16)
    ffn_w1 = ffn_w1_l0.astype(BF16)
    ffn_w3 = ffn_w3_l0.astype(BF16)
    ffn_w2 = ffn_w2_l0.astype(BF16)
    tables = _rope_tables(ss)
    kc = cache_k_l0.transpose(2, 0, 1, 3).reshape(KV_B, bs * past, HD).astype(BF16)
    vc = cache_v_l0.transpose(2, 0, 1, 3).reshape(KV_B, bs * past, HD)
    vc = jnp.concatenate([vc, jnp.ones_like(vc)], axis=-1).astype(BF16)
    c0 = jnp.concatenate([jnp.swapaxes(state_mlstm_c_l0, -1, -2),
                          jnp.broadcast_to(state_mlstm_n_l0[..., None, :], state_mlstm_c_l0.shape)], axis=-2)
    m0 = jnp.broadcast_to(state_mlstm_m_l0[..., None], state_mlstm_m_l0.shape + (LANES,))

    n_main1 = 2 * D_INNER + 2 * NG_C * D_STATE
    w1_main = ssd_w_in_l1[:, :n_main1].astype(BF16)
    w1_aux_hi, w1_aux_lo = _split2(_pad_cols(ssd_w_in_l1[:, n_main1:], LANES))
    dtb = _pad_cols(ssd_dt_bias_l1.reshape(1, 2 * H_C), LANES)
    alog = _pad_cols(ssd_a_log_l1.reshape(1, 2 * H_C), LANES)
    dsk = jnp.broadcast_to(jnp.repeat(ssd_d_l1, HD)[:, None], (D_INNER, LANES))
    ssd_ng = ssd_norm_g_l1.reshape(1, D_INNER)
    w_out1 = ssd_w_out_l1.astype(BF16)
    rw_hi, rw_lo = _split2(_pad_cols(router_w_l1, LANES))
    rb = jnp.full((1, LANES), NEG, F32).at[0, :N_EXPERTS].set(router_b_l1)
    moe_w1 = moe_w1_l1.astype(BF16)
    moe_w3 = moe_w3_l1.astype(BF16)
    moe_w2 = moe_w2_l1.astype(BF16)
    s0 = state_ssd_l1.reshape(bs, 2, NG_C, HPG_C * HD, D_STATE)

    outs = {}
    for grp in groups:
        x, b, s, row, prompt = grp["x"], grp["b"], grp["s"], grp["row"], grp["prompt"]
        pm, pa = _mm_in(x, mods0, row, w0_main, w0_aux_hi, w0_aux_lo, tm=512, n_chunks=2)
        if prompt:
            hn, cst, mst = _mlstm(pm, pa, gbias, mlstm_norm_g_l0.reshape(1, W_A), b, s, out_state=True)
            qn, kh, vh, kn = _qkprep(pm, q_g, k_g, avg128, s, emit_kn=True)
            o = _attn(qn, kh, vh, b, s)
            outs["k"] = kn.reshape(b, s, KV_B, HD)
            outs["v"] = pm[:, 2688:2816].astype(F32).reshape(b, s, KV_B, HD)
            outs["c"] = jnp.swapaxes(cst[..., :HD, :], -1, -2)
            outs["n"] = cst[..., HD, :]
            outs["m"] = mst[..., 0]
        else:
            hn = _mlstm(pm, pa, gbias, mlstm_norm_g_l0.reshape(1, W_A), b, s, init=(c0, m0))[0]
            qn, kh, vh = _qkprep(pm, q_g, k_g, avg128, s, tables=tables)
            o = _attn(qn, kh, vh, b, s, ctx=(kc, vc))
        x1, u2 = _out_proj([hn, o], [w_out_h, w_out_o], x, mods0, row, ln1_g_l0, ln1_b_l0)
        x2 = _ffn(u2, ffn_w1, ffn_w3, ffn_w2, x1, mods0, row, ln2_g_l0, ln2_b_l0, tf=D_FF // 2)
        pm, pa = _mm_in(x2, mods1, row, w1_main, w1_aux_hi, w1_aux_lo, tm=512, n_chunks=4)
        xt = _conv(pm, ssd_conv_w_l1[:, :D_INNER], ssd_conv_b_l1[:D_INNER], b, s, D_INNER, True)
        bc = _conv(pm, ssd_conv_w_l1[:, D_INNER:], ssd_conv_b_l1[D_INNER:], b, s, 2 * D_INNER, False)
        if prompt:
            yn, sst = _ssd(xt, bc, pa, pm, dtb, alog, dsk, ssd_ng, b, s, out_state=True)
            outs["s"] = sst.reshape(b, 2, H_C, HD, D_STATE)
        else:
            yn = _ssd(xt, bc, pa, pm, dtb, alog, dsk, ssd_ng, b, s, init=s0)[0]
        x1, u2, gate = _out_proj([yn], [w_out1], x2, mods1, row, ln1_g_l1, ln1_b_l1, router=(rw_hi, rw_lo, rb))
        x3 = _moe(u2, moe_w1, moe_w3, moe_w2, gate, x1, mods1, row, ln2_g_l1, ln2_b_l1,
                  tm=min(1024, b * s if prompt else s))
        outs["y_p" if prompt else "y_s"] = x3.reshape(b, s, D)

    return (outs["y_p"], outs["y_s"], outs["k"], outs["v"], outs["c"], outs["n"], outs["m"], outs["s"])
```

```python
import functools

import jax
import jax.numpy as jnp
import numpy as np
from jax import lax
from jax.experimental import pallas as pl
from jax.experimental.pallas import tpu as pltpu

F32 = jnp.float32
BF16 = jnp.bfloat16

D = 1024
CH = 128
LANES = 128
EPS = 1e-6
DEPTH = 2
ALPHA = (2 * DEPTH) ** 0.25
GRID_W = 64
ROPE_THETA = 10000.0
H_A = 8
HD = 64
W_A = 512
KV_B = 2
H_C = 32
NG_C = 4
HPG_C = 8
D_STATE = 128
D_INNER = 2048
D_FF = 2816
N_EXPERTS = 8
D_FF_E = 1408
NEG = -1e30
LOG2E = 1.4426950408889634
VMEM_LIMIT = 56 * 1024 * 1024


def _cp(*sem):
    return pltpu.CompilerParams(dimension_semantics=sem, vmem_limit_bytes=VMEM_LIMIT)


def _sigmoid(x):
    return 1.0 / (1.0 + jnp.exp(-x))


def _silu(x):
    return x * _sigmoid(x)


def _softplus(x):
    return jnp.maximum(x, 0.0) + jnp.log(1.0 + jnp.exp(-jnp.abs(x)))


def _split2(x):
    hi = x.astype(BF16)
    lo = (x - hi.astype(F32)).astype(BF16)
    return hi, lo


def _split3(x):
    h1 = x.astype(BF16)
    r = x - h1.astype(F32)
    h2 = r.astype(BF16)
    h3 = (r - h2.astype(F32)).astype(BF16)
    return h1, h2, h3


def _dot(a, b):
    return jnp.dot(a, b, preferred_element_type=F32)


def _dot_nt(a, b):
    return lax.dot_general(a, b, (((1,), (1,)), ((), ())), preferred_element_type=F32)


def _tri_dot(tri, x):
    h1, h2, h3 = _split3(x)
    return _dot(tri, h1) + _dot(tri, h2) + _dot(tri, h3)


def _layer_norm(r, g, b):
    mu = jnp.mean(r, axis=-1, keepdims=True)
    d = r - mu
    var = jnp.mean(d * d, axis=-1, keepdims=True)
    return d * lax.rsqrt(var + EPS) * g + b


def _ada_kernel(c_ref, w_ref, b_ref, o_ref):
    s = _silu(c_ref[...])
    sh, sl = _split2(s)
    wh, wl = _split2(w_ref[...])
    o_ref[...] = _dot(sh, wh) + _dot(sl, wh) + _dot(sh, wl) + b_ref[...]


def _ada(cvec, w, b):
    n = w.shape[1]
    tn = 768
    return pl.pallas_call(
        _ada_kernel,
        grid=(n // tn,),
        in_specs=[pl.BlockSpec((16, D), lambda j: (0, 0)),
                  pl.BlockSpec((D, tn), lambda j: (0, j)),
                  pl.BlockSpec((1, tn), lambda j: (0, j))],
        out_specs=pl.BlockSpec((16, tn), lambda j: (0, j)),
        out_shape=jax.ShapeDtypeStruct((16, n), F32),
        compiler_params=_cp("parallel"),
        name="ada_mods",
    )(cvec, w, b.reshape(1, n))


def _dot_hilo(uh, ul, whl_ref):
    both = _dot(uh, whl_ref[...])
    return both[:, :LANES] + both[:, LANES:] + _dot(ul, whl_ref[:, :LANES])


def _hilo(w):
    return jnp.concatenate(_split2(w), axis=1)


def _mm_in_kernel(x_ref, sh_ref, sc_ref, wm_ref, wa_ref, om_ref, oa_ref, *, n_chunks):
    u = x_ref[...] * (1.0 + sc_ref[...]) + sh_ref[...]
    uh, ul = _split2(u)
    oa_ref[...] = _dot_hilo(uh, ul, wa_ref)
    tn = wm_ref.shape[1] // n_chunks
    for c in range(n_chunks):
        om_ref[:, c * tn:(c + 1) * tn] = _dot(uh, wm_ref[:, c * tn:(c + 1) * tn]).astype(om_ref.dtype)


def _mm_in(x, mods3, row_fn, wm, wa_hilo, tm, n_chunks):
    t, nm = x.shape[0], wm.shape[1]
    once = pl.Buffered(1)
    return pl.pallas_call(
        functools.partial(_mm_in_kernel, n_chunks=n_chunks),
        grid=(t // tm,),
        in_specs=[pl.BlockSpec((tm, D), lambda i: (i, 0)),
                  pl.BlockSpec((None, 1, D), lambda i: (row_fn(i, tm), 0, 0)),
                  pl.BlockSpec((None, 1, D), lambda i: (row_fn(i, tm), 0, 1)),
                  pl.BlockSpec((D, nm), lambda i: (0, 0), pipeline_mode=once),
                  pl.BlockSpec((D, 2 * LANES), lambda i: (0, 0), pipeline_mode=once)],
        out_specs=[pl.BlockSpec((tm, nm), lambda i: (i, 0)),
                   pl.BlockSpec((tm, LANES), lambda i: (i, 0))],
        out_shape=[jax.ShapeDtypeStruct((t, nm), BF16), jax.ShapeDtypeStruct((t, LANES), F32)],
        compiler_params=_cp("parallel"),
        name="mm_in",
    )(x, mods3, mods3, wm, wa_hilo)


def _mlstm_kernel(*refs, nc, has_init, out_state):
    q_ref, k_ref, v_ref, om_ref, g_ref, gn_ref, gb_ref, ng_ref = refs[:8]
    pos = 8
    if has_init:
        c0_ref, m0_ref = refs[pos:pos + 2]
        pos += 2
    hn_ref = refs[pos]
    pos += 1
    if out_state:
        cst_ref, mst_ref = refs[pos:pos + 2]
        pos += 2
    c_scr, m_scr, hb_scr, r_scr, rows_scr = refs[pos:pos + 5]

    j = pl.program_id(1)
    is_fwd = j >= nc
    ci = jnp.where(is_fwd, j - nc, nc - 1 - j)
    sgn = jnp.where(is_fwd, 1, -1)

    def _init(d):
        if has_init:
            c_scr[...] = c0_ref[d]
            m_scr[...] = m0_ref[d]
        else:
            c_scr[...] = jnp.zeros(c_scr.shape, F32)
            m_scr[...] = jnp.zeros(m_scr.shape, F32)

    pl.when(j == 0)(lambda: _init(1))
    pl.when(j == nc)(lambda: _init(0))

    def _prep(graw_ref, sg, slot):
        row = lax.broadcasted_iota(jnp.int32, (CH, CH), 0)
        col = lax.broadcasted_iota(jnp.int32, (CH, CH), 1)
        tri = jnp.where((col - row) * sg <= 0, 1.0, 0.0).astype(BF16)
        f_mat = (jnp.zeros((CH, LANES), jnp.int32) + sg) > 0
        g = graw_ref[...] + gb_ref[...]
        li = jnp.where(f_mat, g, pltpu.roll(g, LANES - 16, axis=1))
        lfr = jnp.where(f_mat, pltpu.roll(g, LANES - 8, axis=1), pltpu.roll(g, LANES - 24, axis=1))
        lf = jnp.minimum(lfr, 0.0) - jnp.log(1.0 + jnp.exp(-jnp.abs(lfr)))
        bc = _tri_dot(tri, lf)
        rr = li - bc
        cmx = rr
        for dlt in (1, 2, 4, 8, 16, 32, 64):
            before = jnp.where(row >= dlt, pltpu.roll(cmx, dlt, axis=0), NEG)
            after = jnp.where(row < CH - dlt, pltpu.roll(cmx, CH - dlt, axis=0), NEG)
            cmx = jnp.maximum(cmx, jnp.where(f_mat, before, after))
        r_scr[slot] = rr * LOG2E
        rows_scr[slot, 0] = rr.T[0:H_A, :]
        rows_scr[slot, 1] = bc.T[0:H_A, :]
        rows_scr[slot, 2] = cmx.T[0:H_A, :]

    pl.when(j == 0)(lambda: _prep(g_ref, sgn, 0))

    slot = j % 2
    row = lax.broadcasted_iota(jnp.int32, (CH, CH), 0)
    col = lax.broadcasted_iota(jnp.int32, (CH, CH), 1)
    causal_t = (row - col) * sgn <= 0
    fwd8 = (jnp.zeros((H_A, LANES), jnp.int32) + sgn) > 0
    r = r_scr[slot]
    r_t = rows_scr[slot, 0]
    b_t = rows_scr[slot, 1]
    cm = rows_scr[slot, 2]
    _prep(gn_ref, jnp.where(j + 1 >= nc, 1, -1), (j + 1) % 2)

    m_prev = m_scr[...]
    g_t = jnp.maximum(m_prev, cm)
    mt_t = b_t + g_t
    wst_t = jnp.exp(m_prev - g_t)
    emt_t = jnp.exp(-mt_t)
    fwd_c = fwd8[:, 0:1]
    m_new = jnp.where(fwd_c, mt_t[:, LANES - 1:LANES], mt_t[:, 0:1])
    b_last = jnp.where(fwd_c, b_t[:, LANES - 1:LANES], b_t[:, 0:1])
    ws_t = jnp.exp(r_t + (b_last - m_new))
    decay = jnp.exp(b_last + m_prev[:, 0:1] - m_new)
    m_scr[...] = jnp.broadcast_to(m_new, (H_A, LANES))

    q_t = q_ref[...].astype(F32).T.astype(BF16)
    v_t = v_ref[...].astype(F32).T
    ones = jnp.ones((HD, CH), F32)

    g2_t = g_t * LOG2E
    heads = range(H_A)
    sls = [slice(h * HD, (h + 1) * HD) for h in heads]
    khs = [k_ref[:, sl] for sl in sls]
    qhs = [q_t[sl, :] for sl in sls]
    vexts = [jnp.concatenate([v_t[sl, :], ones], axis=0) for sl in sls]
    cexts = [c_scr[h] for h in heads]
    scs = [_dot(khs[h], qhs[h]) for h in heads]
    w_ts = [jnp.exp2(jnp.where(causal_t, r[:, h:h + 1] - g2_t[h:h + 1, :], NEG)) for h in heads]
    sc_ts = [(scs[h] * 0.125 * w_ts[h]).astype(BF16) for h in heads]
    cqs = [_dot(cexts[h].astype(BF16), qhs[h]) for h in heads]
    nds = [_dot(vexts[h].astype(BF16), sc_ts[h]) + (cqs[h] * 0.125) * wst_t[h:h + 1, :] for h in heads]
    hts = [nds[h][0:HD, :] / jnp.maximum(jnp.abs(nds[h][HD:HD + 1, :]), emt_t[h:h + 1, :]) for h in heads]
    for h in heads:
        c_scr[h] = decay[h:h + 1, :] * cexts[h] + _dot((vexts[h] * ws_t[h:h + 1, :]).astype(BF16), khs[h])
    hcat_t = jnp.concatenate(hts, axis=0)

    @pl.when(j < nc)
    def _():
        hb_scr[ci] = hcat_t

    @pl.when(j >= nc)
    def _():
        ht = hcat_t + hb_scr[ci]
        normed = []
        for h in range(H_A):
            hh = ht[h * HD:(h + 1) * HD, :]
            dlt = hh - jnp.mean(hh, axis=0, keepdims=True)
            normed.append(dlt * lax.rsqrt(jnp.mean(dlt * dlt, axis=0, keepdims=True) + EPS))
        hn = jnp.concatenate(normed, axis=0).T * ng_ref[...] * _sigmoid(om_ref[...].astype(F32))
        hn_ref[...] = hn.astype(hn_ref.dtype)

    if out_state:
        @pl.when(j == nc - 1)
        def _():
            cst_ref[1] = c_scr[...]
            mst_ref[1] = m_scr[...]

        @pl.when(j == 2 * nc - 1)
        def _():
            cst_ref[0] = c_scr[...]
            mst_ref[0] = m_scr[...]


def _mlstm(pm, pa, gbias, norm_g, bsz, s_len, init=None, out_state=False):
    nc = s_len // CH
    t = bsz * s_len

    def chunk(b, j):
        return b * nc + jnp.where(j >= nc, j - nc, nc - 1 - j)

    def chunk_next(b, j):
        return chunk(b, jnp.minimum(j + 1, 2 * nc - 1))

    def chunk_fwd(b, j):
        return b * nc + jnp.maximum(j - nc, 0)

    in_specs = [pl.BlockSpec((CH, W_A), lambda b, j: (chunk(b, j), 0)),
                pl.BlockSpec((CH, W_A), lambda b, j: (chunk(b, j), 1)),
                pl.BlockSpec((CH, W_A), lambda b, j: (chunk(b, j), 2)),
                pl.BlockSpec((CH, W_A), lambda b, j: (chunk_fwd(b, j), 3)),
                pl.BlockSpec((CH, LANES), lambda b, j: (chunk(b, j), 0)),
                pl.BlockSpec((CH, LANES), lambda b, j: (chunk_next(b, j), 0)),
                pl.BlockSpec((1, LANES), lambda b, j: (0, 0)),
                pl.BlockSpec((1, W_A), lambda b, j: (0, 0))]
    args = [pm, pm, pm, pm, pa, pa, gbias, norm_g]
    if init is not None:
        in_specs += [pl.BlockSpec((None, 2, H_A, LANES, HD), lambda b, j: (b, 0, 0, 0, 0)),
                     pl.BlockSpec((None, 2, H_A, LANES), lambda b, j: (b, 0, 0, 0))]
        args += list(init)
    out_specs = [pl.BlockSpec((CH, W_A), lambda b, j: (chunk_fwd(b, j), 0))]
    out_shape = [jax.ShapeDtypeStruct((t, W_A), BF16)]
    if out_state:
        out_specs += [pl.BlockSpec((None, 2, H_A, LANES, HD), lambda b, j: (b, 0, 0, 0, 0)),
                      pl.BlockSpec((None, 2, H_A, LANES), lambda b, j: (b, 0, 0, 0))]
        out_shape += [jax.ShapeDtypeStruct((bsz, 2, H_A, LANES, HD), F32),
                      jax.ShapeDtypeStruct((bsz, 2, H_A, LANES), F32)]
    return pl.pallas_call(
        functools.partial(_mlstm_kernel, nc=nc, has_init=init is not None, out_state=out_state),
        grid=(bsz, 2 * nc),
        in_specs=in_specs,
        out_specs=out_specs,
        out_shape=out_shape,
        scratch_shapes=[pltpu.VMEM((H_A, LANES, HD), F32), pltpu.VMEM((H_A, LANES), F32),
                        pltpu.VMEM((nc, W_A, CH), F32),
                        pltpu.VMEM((2, CH, LANES), F32), pltpu.VMEM((2, 3, H_A, LANES), F32)],
        compiler_params=_cp("parallel", "arbitrary"),
        name="mlstm",
    )(*args)


def _group_rms(x, avg, g):
    s1, s2 = _split2(x * x)
    ms = _dot(s1, avg) + _dot(s2, avg)
    return x * lax.rsqrt(ms + EPS) * g


def _rope(x, cos, sin):
    lane = lax.broadcasted_iota(jnp.int32, x.shape, 1)
    swapped = jnp.where((lane % 32) < 16, pltpu.roll(x, LANES - 16, axis=1), pltpu.roll(x, 16, axis=1))
    return x * cos + swapped * sin


def _qkprep_kernel(*refs, rope, emit_kn):
    q_ref, k_ref, v_ref, qg_ref, kg_ref, avg_ref = refs[:6]
    pos = 6
    if rope:
        cos_ref, sin_ref = refs[pos:pos + 2]
        pos += 2
    qn_ref, kh_ref, vh_ref = refs[pos:pos + 3]
    pos += 3
    avg = avg_ref[...]
    qs = []
    for c in range(W_A // LANES):
        sl = slice(c * LANES, (c + 1) * LANES)
        xn = _group_rms(q_ref[:, sl].astype(F32), avg, qg_ref[:, sl])
        if rope:
            xn = _rope(xn, cos_ref[...], sin_ref[...])
        qs.append(xn * 0.125)
    qn_ref[...] = jnp.concatenate(qs, axis=1).astype(qn_ref.dtype)
    kn = _group_rms(k_ref[...].astype(F32), avg, kg_ref[...])
    if emit_kn:
        refs[pos][...] = kn
    if rope:
        kn = _rope(kn, cos_ref[...], sin_ref[...])
    v = v_ref[...].astype(F32)
    ones = jnp.ones((v.shape[0], HD), F32)
    for gi in range(KV_B):
        sl = slice(gi * HD, (gi + 1) * HD)
        kh_ref[gi] = kn[:, sl].astype(kh_ref.dtype)
        vh_ref[gi] = jnp.concatenate([v[:, sl], ones], axis=1).astype(vh_ref.dtype)


def _qkprep(pm, q_g, k_g, avg128, s_len, tables=None, emit_kn=False):
    t = pm.shape[0]
    tm = 256
    nt = s_len // tm
    qoff, koff, voff = 2048 // W_A, 2560 // LANES, 2688 // LANES
    in_specs = [pl.BlockSpec((tm, W_A), lambda i: (i, qoff)),
                pl.BlockSpec((tm, LANES), lambda i: (i, koff)),
                pl.BlockSpec((tm, LANES), lambda i: (i, voff)),
                pl.BlockSpec((1, W_A), lambda i: (0, 0)),
                pl.BlockSpec((1, LANES), lambda i: (0, 0)),
                pl.BlockSpec((LANES, LANES), lambda i: (0, 0))]
    args = [pm, pm, pm, q_g, k_g, avg128]
    if tables is not None:
        in_specs += [pl.BlockSpec((tm, LANES), lambda i: (i % nt, 0))] * 2
        args += list(tables)
    out_specs = [pl.BlockSpec((tm, W_A), lambda i: (i, 0)),
                 pl.BlockSpec((KV_B, tm, HD), lambda i: (0, i, 0)),
                 pl.BlockSpec((KV_B, tm, LANES), lambda i: (0, i, 0))]
    out_shape = [jax.ShapeDtypeStruct((t, W_A), BF16),
                 jax.ShapeDtypeStruct((KV_B, t, HD), BF16),
                 jax.ShapeDtypeStruct((KV_B, t, LANES), BF16)]
    if emit_kn:
        out_specs.append(pl.BlockSpec((tm, LANES), lambda i: (i, 0)))
        out_shape.append(jax.ShapeDtypeStruct((t, LANES), F32))
    return pl.pallas_call(
        functools.partial(_qkprep_kernel, rope=tables is not None, emit_kn=emit_kn),
        grid=(t // tm,),
        in_specs=in_specs,
        out_specs=out_specs,
        out_shape=out_shape,
        compiler_params=_cp("parallel"),
        name="qkprep",
    )(*args)


def _attn_kernel(*refs, has_ctx):
    if has_ctx:
        q_ref, k_ref, v_ref, kc_ref, vc_ref, o_ref = refs
    else:
        q_ref, k_ref, v_ref, o_ref = refs
    k = k_ref[...]
    v = v_ref[...]
    hq = range(4)
    qhs = [q_ref[:, h * HD:(h + 1) * HD] for h in hq]
    ss = [_dot_nt(qh, k) for qh in qhs]
    ms = [jnp.max(s, axis=1, keepdims=True) for s in ss]
    if has_ctx:
        scs = [_dot_nt(qh, kc_ref[...]) for qh in qhs]
        ms = [jnp.maximum(m, jnp.max(sc, axis=1, keepdims=True)) for m, sc in zip(ms, scs)]
    oes = [_dot(jnp.exp(s - m).astype(BF16), v) for s, m in zip(ss, ms)]
    if has_ctx:
        oes = [oe + _dot(jnp.exp(sc - m).astype(BF16), vc_ref[...]) for oe, sc, m in zip(oes, scs, ms)]
    outs = [oe[:, :HD] / oe[:, HD:HD + 1] for oe in oes]
    o_ref[...] = jnp.concatenate(outs, axis=1).astype(o_ref.dtype)


def _attn(qn, kh, vh, bsz, s_len, ctx=None):
    t = qn.shape[0]
    tq = min(256, s_len)
    nq = s_len // tq
    in_specs = [pl.BlockSpec((tq, 256), lambda b, g, i: (b * nq + i, g)),
                pl.BlockSpec((None, s_len, HD), lambda b, g, i: (g, b, 0)),
                pl.BlockSpec((None, s_len, LANES), lambda b, g, i: (g, b, 0))]
    args = [qn, kh, vh]
    if ctx is not None:
        past = ctx[0].shape[1] // bsz
        in_specs += [pl.BlockSpec((None, past, HD), lambda b, g, i: (g, b, 0)),
                     pl.BlockSpec((None, past, LANES), lambda b, g, i: (g, b, 0))]
        args += list(ctx)
    return pl.pallas_call(
        functools.partial(_attn_kernel, has_ctx=ctx is not None),
        grid=(bsz, KV_B, nq),
        in_specs=in_specs,
        out_specs=pl.BlockSpec((tq, 256), lambda b, g, i: (b * nq + i, g)),
        out_shape=jax.ShapeDtypeStruct((t, W_A), BF16),
        compiler_params=_cp("parallel", "parallel", "parallel"),
        name="attn",
    )(*args)


def _out_kernel(*refs, n_in, router):
    a_refs = refs[:n_in]
    w_refs = refs[n_in:2 * n_in]
    x_ref, g1_ref, sh2_ref, sc2_ref, lng_ref, lnb_ref = refs[2 * n_in:2 * n_in + 6]
    pos = 2 * n_in + 6
    if router:
        rw_ref, rb_ref = refs[pos:pos + 2]
        pos += 2
    x1_ref, u2_ref = refs[pos:pos + 2]
    y = _dot(a_refs[0][...], w_refs[0][...])
    for a_ref, w_ref in zip(a_refs[1:], w_refs[1:]):
        y = y + _dot(a_ref[...], w_ref[...])
    x1 = _layer_norm(ALPHA * x_ref[...] + g1_ref[...] * y, lng_ref[...], lnb_ref[...])
    x1_ref[...] = x1
    u2 = x1 * (1.0 + sc2_ref[...]) + sh2_ref[...]
    uh, ul = _split2(u2)
    u2_ref[...] = uh
    if router:
        gate_ref = refs[pos + 2]
        logits = _dot_hilo(uh, ul, rw_ref) + rb_ref[...]
        lane = lax.broadcasted_iota(jnp.int32, logits.shape, 1)
        m1 = jnp.max(logits, axis=1, keepdims=True)
        i1 = jnp.min(jnp.where(logits == m1, lane, LANES), axis=1, keepdims=True)
        rest = jnp.where(lane == i1, 2.0 * NEG, logits)
        m2 = jnp.max(rest, axis=1, keepdims=True)
        i2 = jnp.min(jnp.where(rest == m2, lane, LANES), axis=1, keepdims=True)
        e = jnp.exp(m2 - m1)
        p1 = 1.0 / (1.0 + e)
        gate_ref[...] = jnp.where(lane == i1, p1, jnp.where(lane == i2, e * p1, 0.0))


def _out_proj(acts, ws, x, mods3, row_fn, ln_g, ln_b, router=None):
    t = x.shape[0]
    tm = 512
    n_in = len(acts)
    in_specs = [pl.BlockSpec((tm, a.shape[1]), lambda i: (i, 0)) for a in acts]
    in_specs += [pl.BlockSpec(w.shape, lambda i: (0, 0)) for w in ws]
    in_specs += [pl.BlockSpec((tm, D), lambda i: (i, 0))]
    in_specs += [pl.BlockSpec((None, 1, D), functools.partial(lambda i, k: (row_fn(i, tm), 0, k), k=k))
                 for k in (2, 3, 4)]
    in_specs += [pl.BlockSpec((1, D), lambda i: (0, 0))] * 2
    args = list(acts) + list(ws) + [x, mods3, mods3, mods3, ln_g.reshape(1, D), ln_b.reshape(1, D)]
    out_specs = [pl.BlockSpec((tm, D), lambda i: (i, 0)), pl.BlockSpec((tm, D), lambda i: (i, 0))]
    out_shape = [jax.ShapeDtypeStruct((t, D), F32), jax.ShapeDtypeStruct((t, D), BF16)]
    if router is not None:
        in_specs += [pl.BlockSpec((D, 2 * LANES), lambda i: (0, 0)), pl.BlockSpec((1, LANES), lambda i: (0, 0))]
        args += list(router)
        out_specs.append(pl.BlockSpec((tm, LANES), lambda i: (i, 0)))
        out_shape.append(jax.ShapeDtypeStruct((t, LANES), F32))
    return pl.pallas_call(
        functools.partial(_out_kernel, n_in=n_in, router=router is not None),
        grid=(t // tm,),
        in_specs=in_specs,
        out_specs=out_specs,
        out_shape=out_shape,
        compiler_params=_cp("parallel"),
        name="out_proj",
    )(*args)


def _ffn_kernel(u_ref, w1_ref, w3_ref, w2_ref, x1_ref, g2_ref, lng_ref, lnb_ref, o_ref, *, tf):
    u = u_ref[...]
    acc = None
    for c in range(w1_ref.shape[1] // tf):
        cs = slice(c * tf, (c + 1) * tf)
        h = _silu(_dot(u, w1_ref[:, cs])) * _dot(u, w3_ref[:, cs])
        part = _dot(h.astype(BF16), w2_ref[cs, :])
        acc = part if acc is None else acc + part
    o_ref[...] = _layer_norm(ALPHA * x1_ref[...] + g2_ref[...] * acc, lng_ref[...], lnb_ref[...])


def _ffn(u2, w1, w3, w2, x1, mods3, row_fn, ln_g, ln_b, tf):
    t = u2.shape[0]
    tm = 512
    f = w1.shape[1]
    once = pl.Buffered(1)
    return pl.pallas_call(
        functools.partial(_ffn_kernel, tf=tf),
        grid=(t // tm,),
        in_specs=[pl.BlockSpec((tm, D), lambda i: (i, 0)),
                  pl.BlockSpec((D, f), lambda i: (0, 0), pipeline_mode=once),
                  pl.BlockSpec((D, f), lambda i: (0, 0), pipeline_mode=once),
                  pl.BlockSpec((f, D), lambda i: (0, 0), pipeline_mode=once),
                  pl.BlockSpec((tm, D), lambda i: (i, 0)),
                  pl.BlockSpec((None, 1, D), lambda i: (row_fn(i, tm), 0, 5)),
                  pl.BlockSpec((1, D), lambda i: (0, 0)),
                  pl.BlockSpec((1, D), lambda i: (0, 0))],
        out_specs=pl.BlockSpec((tm, D), lambda i: (i, 0)),
        out_shape=jax.ShapeDtypeStruct((t, D), F32),
        compiler_params=_cp("parallel"),
        name="ffn",
    )(u2, w1, w3, w2, x1, mods3, ln_g.reshape(1, D), ln_b.reshape(1, D))


MOE_CAP = 256


def _moe_kernel(u_ref, w1_ref, w3_ref, w2_ref, gate_ref, tril_ref, x1_ref, g2_ref, lng_ref, lnb_ref, o_ref,
                acc_scr, rank_scr, rank_t_scr, gate_t_scr):
    e = pl.program_id(1)
    tm = u_ref.shape[0]

    @pl.when(e == 0)
    def _():
        acc_scr[...] = jnp.zeros(acc_scr.shape, F32)
        gate = gate_ref[...]
        routed = gate > 0.0
        rank = jnp.where(routed, _dot(tril_ref[...], jnp.where(routed, 1.0, 0.0).astype(BF16)), -1.0)
        rank_scr[...] = rank
        rank_t_scr[...] = rank.T
        gate_t_scr[...] = gate.T

    lane = lax.broadcasted_iota(jnp.int32, (tm, LANES), 1)
    rank_col = jnp.sum(jnp.where(lane == e, rank_scr[...], 0.0), axis=1, keepdims=True)
    rank_row = rank_t_scr[pl.ds(e, 1), :]
    gate_row = gate_t_scr[pl.ds(e, 1), :]
    count = jnp.sum(jnp.where(rank_row >= 0.0, 1, 0))
    u = u_ref[...]

    def one_pass(first_slot, cap):
        off = first_slot.astype(F32)
        slot_c = lax.broadcasted_iota(jnp.int32, (cap, tm), 0).astype(F32) + off
        sel = jnp.where(rank_row == slot_c, 1.0, 0.0)
        weight = jnp.sum(sel * gate_row, axis=1, keepdims=True)
        xe = _dot(sel.astype(BF16), u).astype(BF16)
        h = _silu(_dot(xe, w1_ref[...])) * _dot(xe, w3_ref[...])
        y = _dot(h.astype(BF16), w2_ref[...]) * weight
        slot_r = lax.broadcasted_iota(jnp.int32, (tm, cap), 1).astype(F32) + off
        sel_t = jnp.where(rank_col == slot_r, 1.0, 0.0).astype(BF16)
        acc_scr[...] += _dot(sel_t, y.astype(BF16))

    n_full = count // MOE_CAP
    left = count - n_full * MOE_CAP
    n_full = n_full + jnp.where(left > MOE_CAP // 2, 1, 0)

    def full_pass(blk, carry):
        one_pass(blk * MOE_CAP, MOE_CAP)
        return carry

    lax.fori_loop(0, n_full, full_pass, 0)

    @pl.when(jnp.logical_and(left > 0, left <= MOE_CAP // 2))
    def _():
        one_pass(n_full * MOE_CAP, MOE_CAP // 2)

    @pl.when(e == pl.num_programs(1) - 1)
    def _():
        o_ref[...] = _layer_norm(ALPHA * x1_ref[...] + g2_ref[...] * acc_scr[...], lng_ref[...], lnb_ref[...])


def _moe(u2, w1, w3, w2, gate, x1, mods3, row_fn, ln_g, ln_b, tm):
    t = u2.shape[0]
    n_e, _, f = w1.shape
    tril = jnp.tril(jnp.ones((tm, tm), F32), -1).astype(BF16)
    once = pl.Buffered(1)
    in_specs = [pl.BlockSpec((tm, D), lambda i, e: (i, 0)),
                pl.BlockSpec((None, D, f), lambda i, e: (e, 0, 0)),
                pl.BlockSpec((None, D, f), lambda i, e: (e, 0, 0)),
                pl.BlockSpec((None, f, D), lambda i, e: (e, 0, 0)),
                pl.BlockSpec((tm, LANES), lambda i, e: (i, 0)),
                pl.BlockSpec((tm, tm), lambda i, e: (0, 0), pipeline_mode=once),
                pl.BlockSpec((tm, D), lambda i, e: (i, 0), pipeline_mode=once),
                pl.BlockSpec((None, 1, D), lambda i, e: (row_fn(i, tm), 0, 5)),
                pl.BlockSpec((1, D), lambda i, e: (0, 0)),
                pl.BlockSpec((1, D), lambda i, e: (0, 0))]
    return pl.pallas_call(
        _moe_kernel,
        grid=(t // tm, n_e),
        in_specs=in_specs,
        out_specs=pl.BlockSpec((tm, D), lambda i, e: (i, 0)),
        out_shape=jax.ShapeDtypeStruct((t, D), F32),
        scratch_shapes=[pltpu.VMEM((tm, D), F32), pltpu.VMEM((tm, LANES), F32),
                        pltpu.VMEM((LANES, tm), F32), pltpu.VMEM((LANES, tm), F32)],
        compiler_params=_cp("parallel", "arbitrary"),
        name="moe",
    )(u2, w1, w3, w2, gate, tril, x1, mods3, ln_g.reshape(1, D), ln_b.reshape(1, D))


def _conv_kernel(x_ref, w_ref, b_ref, o_ref, *, s_len, transpose):
    x = x_ref[...].astype(F32)
    rows = x.shape[0]
    t = lax.broadcasted_iota(jnp.int32, x.shape, 0) % s_len
    prev = jnp.where(t == 0, 0.0, pltpu.roll(x, 1, axis=0))
    nxt = jnp.where(t == s_len - 1, 0.0, pltpu.roll(x, rows - 1, axis=0))
    y = _silu(w_ref[0:1, :] * prev + w_ref[1:2, :] * x + w_ref[2:3, :] * nxt + b_ref[...])
    if transpose:
        for q in range(rows // s_len):
            o_ref[q] = y[q * s_len:(q + 1) * s_len, :].T.astype(o_ref.dtype)
    else:
        o_ref[...] = y.astype(o_ref.dtype)


def _conv(pm, conv_w, conv_b, bsz, s_len, col_off, transpose):
    t = pm.shape[0]
    tc = 512
    n_ch = conv_w.shape[1]
    off = col_off // tc
    n_seq = max(1, min(bsz, 2048 // s_len))
    rows = n_seq * s_len
    if transpose:
        out_spec = pl.BlockSpec((n_seq, tc, s_len), lambda b, c: (b, c, 0))
        out_shape = jax.ShapeDtypeStruct((bsz, n_ch, s_len), BF16)
    else:
        out_spec = pl.BlockSpec((rows, tc), lambda b, c: (b, c))
        out_shape = jax.ShapeDtypeStruct((t, n_ch), BF16)
    return pl.pallas_call(
        functools.partial(_conv_kernel, s_len=s_len, transpose=transpose),
        grid=(bsz // n_seq, n_ch // tc),
        in_specs=[pl.BlockSpec((rows, tc), lambda b, c: (b, off + c)),
                  pl.BlockSpec((3, tc), lambda b, c: (0, c)),
                  pl.BlockSpec((1, tc), lambda b, c: (0, c))],
        out_specs=out_spec,
        out_shape=out_shape,
        compiler_params=_cp("parallel", "parallel"),
        name="ssd_conv",
    )(pm, conv_w, conv_b.reshape(1, n_ch))


def _ssd_kernel(*refs, nc, has_init, out_state):
    xt_ref, b_ref, c_ref, dt_ref, dtn_ref, z_ref, dtb_ref, alog_ref, dsk_ref, ng_ref = refs[:10]
    pos = 10
    if has_init:
        s0_ref = refs[pos]
        pos += 1
    yn_ref = refs[pos]
    pos += 1
    if out_state:
        sst_ref = refs[pos]
        pos += 1
    s_scr, yb_scr, yt_scr, xw_scr, cum_scr, rows_scr = refs[pos:pos + 6]

    j = pl.program_id(1)
    is_fwd = j >= nc
    ci = jnp.where(is_fwd, j - nc, nc - 1 - j)
    sgn = jnp.where(is_fwd, 1, -1)

    def _init(d):
        if has_init:
            s_scr[...] = s0_ref[d]
        else:
            s_scr[...] = jnp.zeros(s_scr.shape, F32)

    pl.when(j == 0)(lambda: _init(1))
    pl.when(j == nc)(lambda: _init(0))

    def _prep(dtraw_ref, sg, slot):
        row = lax.broadcasted_iota(jnp.int32, (CH, CH), 0)
        col = lax.broadcasted_iota(jnp.int32, (CH, CH), 1)
        tri = jnp.where((col - row) * sg <= 0, 1.0, 0.0).astype(BF16)
        f_row = (jnp.zeros((1, LANES), jnp.int32) + sg) > 0
        f_mat = (jnp.zeros((CH, LANES), jnp.int32) + sg) > 0
        f_hc = (jnp.zeros((H_C, LANES), jnp.int32) + sg) > 0
        raw = dtraw_ref[...] + dtb_ref[...]
        raw = jnp.where(f_mat, raw, pltpu.roll(raw, LANES - H_C, axis=1))
        alog = alog_ref[...]
        alog = jnp.where(f_row, alog, pltpu.roll(alog, LANES - H_C, axis=1))
        dtv = _softplus(raw)
        cumv = _tri_dot(tri, dtv * (-jnp.exp(alog)))
        cumv_t = cumv.T[0:H_C, :]
        dtv_t = dtv.T[0:H_C, :]
        last = jnp.where(f_hc, jnp.broadcast_to(cumv_t[:, CH - 1:CH], (H_C, LANES)),
                         jnp.broadcast_to(cumv_t[:, 0:1], (H_C, LANES)))
        cum_scr[slot] = cumv * LOG2E
        rows_scr[slot, 0] = cumv_t * LOG2E
        rows_scr[slot, 1] = dtv_t
        rows_scr[slot, 2] = jnp.exp(cumv_t)
        rows_scr[slot, 3] = jnp.exp(last - cumv_t) * dtv_t
        rows_scr[slot, 4] = jnp.exp(last)

    pl.when(j == 0)(lambda: _prep(dt_ref, sgn, 0))

    slot = j % 2
    row = lax.broadcasted_iota(jnp.int32, (CH, CH), 0)
    col = lax.broadcasted_iota(jnp.int32, (CH, CH), 1)
    causal_t = (row - col) * sgn <= 0
    cum = cum_scr[slot]
    cum_t = rows_scr[slot, 0]
    dt_t = rows_scr[slot, 1]
    ecum_t = rows_scr[slot, 2]
    ws_t = rows_scr[slot, 3]
    dec = rows_scr[slot, 4]
    _prep(dtn_ref, jnp.where(j + 1 >= nc, 1, -1), (j + 1) % 2)

    for gi in range(NG_C):
        gsl = slice(gi * D_STATE, (gi + 1) * D_STATE)
        bg = b_ref[:, gsl]
        cg = c_ref[:, gsl]
        cb_t = _dot_nt(bg, cg)
        ys_t = _dot_nt(s_scr[gi].astype(BF16), cg)
        for half in range(2):
            hr = range(half * 4, half * 4 + 4)
            rs = {rr: gi * HPG_C + rr for rr in hr}
            w_ts = {rr: (cb_t * jnp.exp2(jnp.where(causal_t, cum_t[rs[rr]:rs[rr] + 1, :] - cum[:, rs[rr]:rs[rr] + 1],
                                                   NEG))).astype(BF16)
                    for rr in hr}
            xr_ts = {rr: xt_ref[rs[rr] * HD:(rs[rr] + 1) * HD, :].astype(F32) for rr in hr}
            xd_ts = {rr: (xr_ts[rr] * dt_t[rs[rr]:rs[rr] + 1, :]).astype(BF16) for rr in hr}
            y_ts = {rr: _dot(xd_ts[rr], w_ts[rr]) + ys_t[rr * HD:(rr + 1) * HD, :] * ecum_t[rs[rr]:rs[rr] + 1, :]
                    for rr in hr}
            for rr in hr:
                yt_scr[rs[rr] * HD:(rs[rr] + 1) * HD, :] = y_ts[rr]
                xw_scr[rr * HD:(rr + 1) * HD, :] = (xr_ts[rr] * ws_t[rs[rr]:rs[rr] + 1, :]).astype(BF16)
        upd = _dot(xw_scr[...], bg)
        for rr in range(HPG_C):
            r = gi * HPG_C + rr
            hsl = slice(rr * HD, (rr + 1) * HD)
            s_scr[gi, hsl, :] = s_scr[gi, hsl, :] * dec[r:r + 1, :] + upd[hsl, :]

    @pl.when(j < nc)
    def _():
        yb_scr[ci] = yt_scr[...]

    @pl.when(j >= nc)
    def _():
        yt = yt_scr[...] + yb_scr[ci] + dsk_ref[...] * xt_ref[...].astype(F32)
        y = yt.T * _silu(z_ref[...].astype(F32))
        outs = []
        gw = D_INNER // NG_C
        for gi in range(NG_C):
            yg = y[:, gi * gw:(gi + 1) * gw]
            outs.append(yg * lax.rsqrt(jnp.mean(yg * yg, axis=1, keepdims=True) + EPS))
        yn_ref[...] = (jnp.concatenate(outs, axis=1) * ng_ref[...]).astype(yn_ref.dtype)

    if out_state:
        @pl.when(j == nc - 1)
        def _():
            sst_ref[1] = s_scr[...]

        @pl.when(j == 2 * nc - 1)
        def _():
            sst_ref[0] = s_scr[...]


def _ssd(xt, bc, pa, pm, dtb, alog, dsk, norm_g, bsz, s_len, init=None, out_state=False):
    nc = s_len // CH
    t = bsz * s_len
    gw = NG_C * D_STATE

    def chunk_i(j):
        return jnp.where(j >= nc, j - nc, nc - 1 - j)

    def chunk(b, j):
        return b * nc + chunk_i(j)

    def chunk_next(b, j):
        return chunk(b, jnp.minimum(j + 1, 2 * nc - 1))

    def chunk_fwd(b, j):
        return b * nc + jnp.maximum(j - nc, 0)

    in_specs = [pl.BlockSpec((None, D_INNER, CH), lambda b, j: (b, 0, chunk_i(j))),
                pl.BlockSpec((CH, gw), lambda b, j: (chunk(b, j), 0)),
                pl.BlockSpec((CH, gw), lambda b, j: (chunk(b, j), 1)),
                pl.BlockSpec((CH, LANES), lambda b, j: (chunk(b, j), 0)),
                pl.BlockSpec((CH, LANES), lambda b, j: (chunk_next(b, j), 0)),
                pl.BlockSpec((CH, D_INNER), lambda b, j: (chunk_fwd(b, j), 0)),
                pl.BlockSpec((1, LANES), lambda b, j: (0, 0)),
                pl.BlockSpec((1, LANES), lambda b, j: (0, 0)),
                pl.BlockSpec((D_INNER, LANES), lambda b, j: (0, 0)),
                pl.BlockSpec((1, D_INNER), lambda b, j: (0, 0))]
    args = [xt, bc, bc, pa, pa, pm, dtb, alog, dsk, norm_g]
    state_spec = pl.BlockSpec((None, 2, NG_C, HPG_C * HD, D_STATE), lambda b, j: (b, 0, 0, 0, 0))
    if init is not None:
        in_specs.append(state_spec)
        args.append(init)
    out_specs = [pl.BlockSpec((CH, D_INNER), lambda b, j: (chunk_fwd(b, j), 0))]
    out_shape = [jax.ShapeDtypeStruct((t, D_INNER), BF16)]
    if out_state:
        out_specs.append(state_spec)
        out_shape.append(jax.ShapeDtypeStruct((bsz, 2, NG_C, HPG_C * HD, D_STATE), F32))
    return pl.pallas_call(
        functools.partial(_ssd_kernel, nc=nc, has_init=init is not None, out_state=out_state),
        grid=(bsz, 2 * nc),
        in_specs=in_specs,
        out_specs=out_specs,
        out_shape=out_shape,
        scratch_shapes=[pltpu.VMEM((NG_C, HPG_C * HD, D_STATE), F32),
                        pltpu.VMEM((nc, D_INNER, CH), F32),
                        pltpu.VMEM((D_INNER, CH), F32),
                        pltpu.VMEM((HPG_C * HD, CH), BF16),
                        pltpu.VMEM((2, CH, LANES), F32), pltpu.VMEM((2, 5, H_C, LANES), F32)],
        compiler_params=_cp("parallel", "arbitrary"),
        name="ssd_scan",
    )(*args)


def _rope_tables(s_len):
    nf = HD // 4
    inv = ROPE_THETA ** (-jnp.arange(nf, dtype=F32) / nf)
    tpos = jnp.arange(s_len)
    ang_r = (tpos // GRID_W).astype(F32)[:, None] * inv
    ang_c = (tpos % GRID_W).astype(F32)[:, None] * inv
    cos = jnp.concatenate([jnp.cos(ang_r)] * 2 + [jnp.cos(ang_c)] * 2, axis=1)
    sin = jnp.concatenate([-jnp.sin(ang_r), jnp.sin(ang_r), -jnp.sin(ang_c), jnp.sin(ang_c)], axis=1)
    return jnp.tile(cos, (1, 2)), jnp.tile(sin, (1, 2))


def _block_avg(n):
    idx = np.arange(n) // HD
    return jnp.asarray((idx[:, None] == idx[None, :]).astype(np.float32) / HD, dtype=BF16)


def _pad_cols(w, n):
    return jnp.pad(w, ((0, 0), (0, n - w.shape[1])))


def _row_prompt(i, tm):
    return 0


def _make_row_sample(s_len):
    def row(i, tm):
        return 1 + (i * tm) // s_len
    return row


def kernel(x_prompt, x_sample, cache_k_l0, cache_v_l0, state_mlstm_c_l0, state_mlstm_n_l0, state_mlstm_m_l0, state_ssd_l1, c, c_ctx, ada_w_l0, ada_b_l0, mix_w_in_l0, mlstm_gate_b_l0, mlstm_norm_g_l0, q_norm_g_l0, k_norm_g_l0, mix_w_out_l0, ln1_g_l0, ln1_b_l0, ln2_g_l0, ln2_b_l0, ffn_w1_l0, ffn_w3_l0, ffn_w2_l0, ada_w_l1, ada_b_l1, ssd_w_in_l1, ssd_conv_w_l1, ssd_conv_b_l1, ssd_dt_bias_l1, ssd_a_log_l1, ssd_d_l1, ssd_norm_g_l1, ssd_w_out_l1, ln1_g_l1, ln1_b_l1, ln2_g_l1, ln2_b_l1, router_w_l1, router_b_l1, moe_w1_l1, moe_w3_l1, moe_w2_l1):
    bp, sp, _ = x_prompt.shape
    bs, ss, _ = x_sample.shape
    past = cache_k_l0.shape[1]
    groups = [
        dict(x=x_prompt.reshape(bp * sp, D), b=bp, s=sp, row=_row_prompt, prompt=True),
        dict(x=x_sample.reshape(bs * ss, D), b=bs, s=ss, row=_make_row_sample(ss), prompt=False),
    ]

    cvec = jnp.zeros((16, D), F32).at[0].set(c_ctx).at[1:1 + bs].set(c)
    mods0 = _ada(cvec, ada_w_l0, ada_b_l0).reshape(16, 1, 6 * D)
    mods1 = _ada(cvec, ada_w_l1, ada_b_l1).reshape(16, 1, 6 * D)

    w0_main = jnp.concatenate([mix_w_in_l0[:, :2048], mix_w_in_l0[:, 2080:]], axis=1).astype(BF16)
    w0_aux = _hilo(_pad_cols(mix_w_in_l0[:, 2048:2080], LANES))
    gbias = _pad_cols(mlstm_gate_b_l0.reshape(1, 4 * H_A), LANES)
    avg128 = _block_avg(LANES)
    q_g = jnp.tile(q_norm_g_l0, W_A // HD).reshape(1, W_A)
    k_g = jnp.tile(k_norm_g_l0, KV_B).reshape(1, LANES)
    w_out_h = mix_w_out_l0[:W_A].astype(BF16)
    w_out_o = mix_w_out_l0[W_A:].astype(BF16)
    ffn_w1 = ffn_w1_l0.astype(BF16)
    ffn_w3 = ffn_w3_l0.astype(BF16)
    ffn_w2 = ffn_w2_l0.astype(BF16)
    tables = _rope_tables(ss)
    kc = cache_k_l0.transpose(2, 0, 1, 3).reshape(KV_B, bs * past, HD).astype(BF16)
    vc = cache_v_l0.transpose(2, 0, 1, 3).reshape(KV_B, bs * past, HD)
    vc = jnp.concatenate([vc, jnp.ones_like(vc)], axis=-1).astype(BF16)
    c0 = jnp.concatenate([jnp.swapaxes(state_mlstm_c_l0, -1, -2),
                          jnp.broadcast_to(state_mlstm_n_l0[..., None, :], state_mlstm_c_l0.shape)], axis=-2)
    m0 = jnp.broadcast_to(state_mlstm_m_l0[..., None], state_mlstm_m_l0.shape + (LANES,))

    n_main1 = 2 * D_INNER + 2 * NG_C * D_STATE
    w1_main = ssd_w_in_l1[:, :n_main1].astype(BF16)
    w1_aux = _hilo(_pad_cols(ssd_w_in_l1[:, n_main1:], LANES))
    dtb = _pad_cols(ssd_dt_bias_l1.reshape(1, 2 * H_C), LANES)
    alog = _pad_cols(ssd_a_log_l1.reshape(1, 2 * H_C), LANES)
    dsk = jnp.broadcast_to(jnp.repeat(ssd_d_l1, HD)[:, None], (D_INNER, LANES))
    ssd_ng = ssd_norm_g_l1.reshape(1, D_INNER)
    w_out1 = ssd_w_out_l1.astype(BF16)
    rw_hilo = _hilo(_pad_cols(router_w_l1, LANES))
    rb = jnp.full((1, LANES), NEG, F32).at[0, :N_EXPERTS].set(router_b_l1)
    moe_w1 = moe_w1_l1.astype(BF16)
    moe_w3 = moe_w3_l1.astype(BF16)
    moe_w2 = moe_w2_l1.astype(BF16)
    s0 = state_ssd_l1.reshape(bs, 2, NG_C, HPG_C * HD, D_STATE)

    outs = {}
    for grp in groups:
        x, b, s, row, prompt = grp["x"], grp["b"], grp["s"], grp["row"], grp["prompt"]
        pm, pa = _mm_in(x, mods0, row, w0_main, w0_aux, tm=512, n_chunks=2)
        if prompt:
            hn, cst, mst = _mlstm(pm, pa, gbias, mlstm_norm_g_l0.reshape(1, W_A), b, s, out_state=True)
            qn, kh, vh, kn = _qkprep(pm, q_g, k_g, avg128, s, emit_kn=True)
            o = _attn(qn, kh, vh, b, s)
            outs["k"] = kn.reshape(b, s, KV_B, HD)
            outs["v"] = pm[:, 2688:2816].astype(F32).reshape(b, s, KV_B, HD)
            outs["c"] = jnp.swapaxes(cst[..., :HD, :], -1, -2)
            outs["n"] = cst[..., HD, :]
            outs["m"] = mst[..., 0]
        else:
            hn = _mlstm(pm, pa, gbias, mlstm_norm_g_l0.reshape(1, W_A), b, s, init=(c0, m0))[0]
            qn, kh, vh = _qkprep(pm, q_g, k_g, avg128, s, tables=tables)
            o = _attn(qn, kh, vh, b, s, ctx=(kc, vc))
        x1, u2 = _out_proj([hn, o], [w_out_h, w_out_o], x, mods0, row, ln1_g_l0, ln1_b_l0)
        x2 = _ffn(u2, ffn_w1, ffn_w3, ffn_w2, x1, mods0, row, ln2_g_l0, ln2_b_l0, tf=D_FF // 2)
        pm, pa = _mm_in(x2, mods1, row, w1_main, w1_aux, tm=512, n_chunks=4)
        xt = _conv(pm, ssd_conv_w_l1[:, :D_INNER], ssd_conv_b_l1[:D_INNER], b, s, D_INNER, True)
        bc = _conv(pm, ssd_conv_w_l1[:, D_INNER:], ssd_conv_b_l1[D_INNER:], b, s, 2 * D_INNER, False)
        if prompt:
            yn, sst = _ssd(xt, bc, pa, pm, dtb, alog, dsk, ssd_ng, b, s, out_state=True)
            outs["s"] = sst.reshape(b, 2, H_C, HD, D_STATE)
        else:
            yn = _ssd(xt, bc, pa, pm, dtb, alog, dsk, ssd_ng, b, s, init=s0)[0]
        x1, u2, gate = _out_proj([yn], [w_out1], x2, mods1, row, ln1_g_l1, ln1_b_l1, router=(rw_hilo, rb))
        x3 = _moe(u2, moe_w1, moe_w3, moe_w2, gate, x1, mods1, row, ln2_g_l1, ln2_b_l1,
                  tm=min(1024, b * s if prompt else s))
        outs["y_p" if prompt else "y_s"] = x3.reshape(b, s, D)

    return (outs["y_p"], outs["y_s"], outs["k"], outs["v"], outs["c"], outs["n"], outs["m"], outs["s"])
```

```python
import functools

import jax
import jax.numpy as jnp
import numpy as np
from jax import lax
from jax.experimental import pallas as pl
from jax.experimental.pallas import tpu as pltpu

F32 = jnp.float32
BF16 = jnp.bfloat16

D = 1024
CH = 128
LANES = 128
EPS = 1e-6
DEPTH = 2
ALPHA = (2 * DEPTH) ** 0.25
GRID_W = 64
ROPE_THETA = 10000.0
H_A = 8
HD = 64
W_A = 512
KV_B = 2
H_C = 32
NG_C = 4
HPG_C = 8
D_STATE = 128
D_INNER = 2048
D_FF = 2816
N_EXPERTS = 8
D_FF_E = 1408
NEG = -1e30
LOG2E = 1.4426950408889634
VMEM_LIMIT = 56 * 1024 * 1024


def _cp(*sem):
    return pltpu.CompilerParams(dimension_semantics=sem, vmem_limit_bytes=VMEM_LIMIT)


def _sigmoid(x):
    return 1.0 / (1.0 + jnp.exp(-x))


def _silu(x):
    return x * _sigmoid(x)


def _softplus(x):
    return jnp.maximum(x, 0.0) + jnp.log(1.0 + jnp.exp(-jnp.abs(x)))


def _split2(x):
    hi = x.astype(BF16)
    lo = (x - hi.astype(F32)).astype(BF16)
    return hi, lo


def _split3(x):
    h1 = x.astype(BF16)
    r = x - h1.astype(F32)
    h2 = r.astype(BF16)
    h3 = (r - h2.astype(F32)).astype(BF16)
    return h1, h2, h3


def _dot(a, b):
    return jnp.dot(a, b, preferred_element_type=F32)


def _dot_nt(a, b):
    return lax.dot_general(a, b, (((1,), (1,)), ((), ())), preferred_element_type=F32)


def _tri_dot(tri, x):
    h1, h2, h3 = _split3(x)
    return _dot(tri, h1) + _dot(tri, h2) + _dot(tri, h3)


def _layer_norm(r, g, b):
    mu = jnp.mean(r, axis=-1, keepdims=True)
    d = r - mu
    var = jnp.mean(d * d, axis=-1, keepdims=True)
    return d * lax.rsqrt(var + EPS) * g + b


def _ada_kernel(c_ref, w_ref, b_ref, o_ref):
    s = _silu(c_ref[...])
    sh, sl = _split2(s)
    wh, wl = _split2(w_ref[...])
    o_ref[...] = _dot(sh, wh) + _dot(sl, wh) + _dot(sh, wl) + b_ref[...]


def _ada(cvec, w, b):
    n = w.shape[1]
    tn = 768
    return pl.pallas_call(
        _ada_kernel,
        grid=(n // tn,),
        in_specs=[pl.BlockSpec((16, D), lambda j: (0, 0)),
                  pl.BlockSpec((D, tn), lambda j: (0, j)),
                  pl.BlockSpec((1, tn), lambda j: (0, j))],
        out_specs=pl.BlockSpec((16, tn), lambda j: (0, j)),
        out_shape=jax.ShapeDtypeStruct((16, n), F32),
        compiler_params=_cp("parallel"),
        name="ada_mods",
    )(cvec, w, b.reshape(1, n))


def _dot_hilo(uh, ul, whl_ref):
    both = _dot(uh, whl_ref[...])
    return both[:, :LANES] + both[:, LANES:] + _dot(ul, whl_ref[:, :LANES])


def _hilo(w):
    return jnp.concatenate(_split2(w), axis=1)


def _mm_in_kernel(x_ref, sh_ref, sc_ref, wm_ref, wa_ref, om_ref, oa_ref, *, n_chunks):
    u = x_ref[...] * (1.0 + sc_ref[...]) + sh_ref[...]
    uh, ul = _split2(u)
    oa_ref[...] = _dot_hilo(uh, ul, wa_ref)
    tn = wm_ref.shape[1] // n_chunks
    for c in range(n_chunks):
        om_ref[:, c * tn:(c + 1) * tn] = _dot(uh, wm_ref[:, c * tn:(c + 1) * tn]).astype(om_ref.dtype)


def _mm_in(x, mods3, row_fn, wm, wa_hilo, tm, n_chunks):
    t, nm = x.shape[0], wm.shape[1]
    once = pl.Buffered(1)
    return pl.pallas_call(
        functools.partial(_mm_in_kernel, n_chunks=n_chunks),
        grid=(t // tm,),
        in_specs=[pl.BlockSpec((tm, D), lambda i: (i, 0)),
                  pl.BlockSpec((None, 1, D), lambda i: (row_fn(i, tm), 0, 0)),
                  pl.BlockSpec((None, 1, D), lambda i: (row_fn(i, tm), 0, 1)),
                  pl.BlockSpec((D, nm), lambda i: (0, 0), pipeline_mode=once),
                  pl.BlockSpec((D, 2 * LANES), lambda i: (0, 0), pipeline_mode=once)],
        out_specs=[pl.BlockSpec((tm, nm), lambda i: (i, 0)),
                   pl.BlockSpec((tm, LANES), lambda i: (i, 0))],
        out_shape=[jax.ShapeDtypeStruct((t, nm), BF16), jax.ShapeDtypeStruct((t, LANES), F32)],
        compiler_params=_cp("parallel"),
        name="mm_in",
    )(x, mods3, mods3, wm, wa_hilo)


def _mlstm_kernel(*refs, nc, has_init, out_state):
    q_ref, k_ref, v_ref, om_ref, g_ref, gn_ref, gb_ref, ng_ref = refs[:8]
    pos = 8
    if has_init:
        c0_ref, m0_ref = refs[pos:pos + 2]
        pos += 2
    hn_ref = refs[pos]
    pos += 1
    if out_state:
        cst_ref, mst_ref = refs[pos:pos + 2]
        pos += 2
    c_scr, m_scr, hb_scr, r_scr, rows_scr = refs[pos:pos + 5]

    j = pl.program_id(1)
    is_fwd = j >= nc
    ci = jnp.where(is_fwd, j - nc, nc - 1 - j)
    sgn = jnp.where(is_fwd, 1, -1)

    def _init(d):
        if has_init:
            c_scr[...] = c0_ref[d]
            m_scr[...] = m0_ref[d]
        else:
            c_scr[...] = jnp.zeros(c_scr.shape, F32)
            m_scr[...] = jnp.zeros(m_scr.shape, F32)

    pl.when(j == 0)(lambda: _init(1))
    pl.when(j == nc)(lambda: _init(0))

    def _prep(graw_ref, sg, slot):
        row = lax.broadcasted_iota(jnp.int32, (CH, CH), 0)
        col = lax.broadcasted_iota(jnp.int32, (CH, CH), 1)
        tri = jnp.where((col - row) * sg <= 0, 1.0, 0.0).astype(BF16)
        f_mat = (jnp.zeros((CH, LANES), jnp.int32) + sg) > 0
        g = graw_ref[...] + gb_ref[...]
        li = jnp.where(f_mat, g, pltpu.roll(g, LANES - 16, axis=1))
        lfr = jnp.where(f_mat, pltpu.roll(g, LANES - 8, axis=1), pltpu.roll(g, LANES - 24, axis=1))
        lf = jnp.minimum(lfr, 0.0) - jnp.log(1.0 + jnp.exp(-jnp.abs(lfr)))
        bc = _tri_dot(tri, lf)
        rr = li - bc
        cmx = rr
        for dlt in (1, 2, 4, 8, 16, 32, 64):
            before = jnp.where(row >= dlt, pltpu.roll(cmx, dlt, axis=0), NEG)
            after = jnp.where(row < CH - dlt, pltpu.roll(cmx, CH - dlt, axis=0), NEG)
            cmx = jnp.maximum(cmx, jnp.where(f_mat, before, after))
        r_scr[slot] = rr * LOG2E
        rows_scr[slot, 0] = rr.T[0:H_A, :]
        rows_scr[slot, 1] = bc.T[0:H_A, :]
        rows_scr[slot, 2] = cmx.T[0:H_A, :]

    pl.when(j == 0)(lambda: _prep(g_ref, sgn, 0))

    slot = j % 2
    row = lax.broadcasted_iota(jnp.int32, (CH, CH), 0)
    col = lax.broadcasted_iota(jnp.int32, (CH, CH), 1)
    causal_t = (row - col) * sgn <= 0
    fwd8 = (jnp.zeros((H_A, LANES), jnp.int32) + sgn) > 0
    r = r_scr[slot]
    r_t = rows_scr[slot, 0]
    b_t = rows_scr[slot, 1]
    cm = rows_scr[slot, 2]
    _prep(gn_ref, jnp.where(j + 1 >= nc, 1, -1), (j + 1) % 2)

    m_prev = m_scr[...]
    g_t = jnp.maximum(m_prev, cm)
    mt_t = b_t + g_t
    wst_t = jnp.exp(m_prev - g_t)
    emt_t = jnp.exp(-mt_t)
    fwd_c = fwd8[:, 0:1]
    m_new = jnp.where(fwd_c, mt_t[:, LANES - 1:LANES], mt_t[:, 0:1])
    b_last = jnp.where(fwd_c, b_t[:, LANES - 1:LANES], b_t[:, 0:1])
    ws_t = jnp.exp(r_t + (b_last - m_new))
    decay = jnp.exp(b_last + m_prev[:, 0:1] - m_new)
    m_scr[...] = jnp.broadcast_to(m_new, (H_A, LANES))

    q_t = q_ref[...].astype(F32).T.astype(BF16)
    v_t = v_ref[...].astype(F32).T
    ones = jnp.ones((HD, CH), F32)

    g2_t = g_t * LOG2E
    heads = range(H_A)
    sls = [slice(h * HD, (h + 1) * HD) for h in heads]
    khs = [k_ref[:, sl] for sl in sls]
    qhs = [q_t[sl, :] for sl in sls]
    vexts = [jnp.concatenate([v_t[sl, :], ones], axis=0) for sl in sls]
    cexts = [c_scr[h] for h in heads]
    scs = [_dot(khs[h], qhs[h]) for h in heads]
    w_ts = [jnp.exp2(jnp.where(causal_t, r[:, h:h + 1] - g2_t[h:h + 1, :], NEG)) for h in heads]
    sc_ts = [(scs[h] * 0.125 * w_ts[h]).astype(BF16) for h in heads]
    cqs = [_dot(cexts[h].astype(BF16), qhs[h]) for h in heads]
    nds = [_dot(vexts[h].astype(BF16), sc_ts[h]) + (cqs[h] * 0.125) * wst_t[h:h + 1, :] for h in heads]
    hts = [nds[h][0:HD, :] / jnp.maximum(jnp.abs(nds[h][HD:HD + 1, :]), emt_t[h:h + 1, :]) for h in heads]
    for h in heads:
        c_scr[h] = decay[h:h + 1, :] * cexts[h] + _dot((vexts[h] * ws_t[h:h + 1, :]).astype(BF16), khs[h])
    hcat_t = jnp.concatenate(hts, axis=0)

    @pl.when(j < nc)
    def _():
        hb_scr[ci] = hcat_t

    @pl.when(j >= nc)
    def _():
        ht = hcat_t + hb_scr[ci]
        normed = []
        for h in range(H_A):
            hh = ht[h * HD:(h + 1) * HD, :]
            dlt = hh - jnp.mean(hh, axis=0, keepdims=True)
            normed.append(dlt * lax.rsqrt(jnp.mean(dlt * dlt, axis=0, keepdims=True) + EPS))
        hn = jnp.concatenate(normed, axis=0).T * ng_ref[...] * _sigmoid(om_ref[...].astype(F32))
        hn_ref[...] = hn.astype(hn_ref.dtype)

    if out_state:
        @pl.when(j == nc - 1)
        def _():
            cst_ref[1] = c_scr[...]
            mst_ref[1] = m_scr[...]

        @pl.when(j == 2 * nc - 1)
        def _():
            cst_ref[0] = c_scr[...]
            mst_ref[0] = m_scr[...]


def _mlstm(pm, pa, gbias, norm_g, bsz, s_len, init=None, out_state=False):
    nc = s_len // CH
    t = bsz * s_len

    def chunk(b, j):
        return b * nc + jnp.where(j >= nc, j - nc, nc - 1 - j)

    def chunk_next(b, j):
        return chunk(b, jnp.minimum(j + 1, 2 * nc - 1))

    def chunk_fwd(b, j):
        return b * nc + jnp.maximum(j - nc, 0)

    in_specs = [pl.BlockSpec((CH, W_A), lambda b, j: (chunk(b, j), 0)),
                pl.BlockSpec((CH, W_A), lambda b, j: (chunk(b, j), 1)),
                pl.BlockSpec((CH, W_A), lambda b, j: (chunk(b, j), 2)),
                pl.BlockSpec((CH, W_A), lambda b, j: (chunk_fwd(b, j), 3)),
                pl.BlockSpec((CH, LANES), lambda b, j: (chunk(b, j), 0)),
                pl.BlockSpec((CH, LANES), lambda b, j: (chunk_next(b, j), 0)),
                pl.BlockSpec((1, LANES), lambda b, j: (0, 0)),
                pl.BlockSpec((1, W_A), lambda b, j: (0, 0))]
    args = [pm, pm, pm, pm, pa, pa, gbias, norm_g]
    if init is not None:
        in_specs += [pl.BlockSpec((None, 2, H_A, LANES, HD), lambda b, j: (b, 0, 0, 0, 0)),
                     pl.BlockSpec((None, 2, H_A, LANES), lambda b, j: (b, 0, 0, 0))]
        args += list(init)
    out_specs = [pl.BlockSpec((CH, W_A), lambda b, j: (chunk_fwd(b, j), 0))]
    out_shape = [jax.ShapeDtypeStruct((t, W_A), BF16)]
    if out_state:
        out_specs += [pl.BlockSpec((None, 2, H_A, LANES, HD), lambda b, j: (b, 0, 0, 0, 0)),
                      pl.BlockSpec((None, 2, H_A, LANES), lambda b, j: (b, 0, 0, 0))]
        out_shape += [jax.ShapeDtypeStruct((bsz, 2, H_A, LANES, HD), F32),
                      jax.ShapeDtypeStruct((bsz, 2, H_A, LANES), F32)]
    return pl.pallas_call(
        functools.partial(_mlstm_kernel, nc=nc, has_init=init is not None, out_state=out_state),
        grid=(bsz, 2 * nc),
        in_specs=in_specs,
        out_specs=out_specs,
        out_shape=out_shape,
        scratch_shapes=[pltpu.VMEM((H_A, LANES, HD), F32), pltpu.VMEM((H_A, LANES), F32),
                        pltpu.VMEM((nc, W_A, CH), F32),
                        pltpu.VMEM((2, CH, LANES), F32), pltpu.VMEM((2, 3, H_A, LANES), F32)],
        compiler_params=_cp("parallel", "arbitrary"),
        name="mlstm",
    )(*args)


def _group_rms(x, avg, g):
    s1, s2 = _split2(x * x)
    ms = _dot(s1, avg) + _dot(s2, avg)
    return x * lax.rsqrt(ms + EPS) * g


def _rope(x, cos, sin):
    lane = lax.broadcasted_iota(jnp.int32, x.shape, 1)
    swapped = jnp.where((lane % 32) < 16, pltpu.roll(x, LANES - 16, axis=1), pltpu.roll(x, 16, axis=1))
    return x * cos + swapped * sin


def _qkprep_kernel(*refs, rope, emit_kn):
    q_ref, k_ref, v_ref, qg_ref, kg_ref, avg_ref = refs[:6]
    pos = 6
    if rope:
        cos_ref, sin_ref = refs[pos:pos + 2]
        pos += 2
    qn_ref, kh_ref, vh_ref = refs[pos:pos + 3]
    pos += 3
    avg = avg_ref[...]
    qs = []
    for c in range(W_A // LANES):
        sl = slice(c * LANES, (c + 1) * LANES)
        xn = _group_rms(q_ref[:, sl].astype(F32), avg, qg_ref[:, sl])
        if rope:
            xn = _rope(xn, cos_ref[...], sin_ref[...])
        qs.append(xn * 0.125)
    qn_ref[...] = jnp.concatenate(qs, axis=1).astype(qn_ref.dtype)
    kn = _group_rms(k_ref[...].astype(F32), avg, kg_ref[...])
    if emit_kn:
        refs[pos][...] = kn
    if rope:
        kn = _rope(kn, cos_ref[...], sin_ref[...])
    v = v_ref[...].astype(F32)
    ones = jnp.ones((v.shape[0], HD), F32)
    for gi in range(KV_B):
        sl = slice(gi * HD, (gi + 1) * HD)
        kh_ref[gi] = kn[:, sl].astype(kh_ref.dtype)
        vh_ref[gi] = jnp.concatenate([v[:, sl], ones], axis=1).astype(vh_ref.dtype)


def _qkprep(pm, q_g, k_g, avg128, s_len, tables=None, emit_kn=False):
    t = pm.shape[0]
    tm = 256
    nt = s_len // tm
    qoff, koff, voff = 2048 // W_A, 2560 // LANES, 2688 // LANES
    in_specs = [pl.BlockSpec((tm, W_A), lambda i: (i, qoff)),
                pl.BlockSpec((tm, LANES), lambda i: (i, koff)),
                pl.BlockSpec((tm, LANES), lambda i: (i, voff)),
                pl.BlockSpec((1, W_A), lambda i: (0, 0)),
                pl.BlockSpec((1, LANES), lambda i: (0, 0)),
                pl.BlockSpec((LANES, LANES), lambda i: (0, 0))]
    args = [pm, pm, pm, q_g, k_g, avg128]
    if tables is not None:
        in_specs += [pl.BlockSpec((tm, LANES), lambda i: (i % nt, 0))] * 2
        args += list(tables)
    out_specs = [pl.BlockSpec((tm, W_A), lambda i: (i, 0)),
                 pl.BlockSpec((KV_B, tm, HD), lambda i: (0, i, 0)),
                 pl.BlockSpec((KV_B, tm, LANES), lambda i: (0, i, 0))]
    out_shape = [jax.ShapeDtypeStruct((t, W_A), BF16),
                 jax.ShapeDtypeStruct((KV_B, t, HD), BF16),
                 jax.ShapeDtypeStruct((KV_B, t, LANES), BF16)]
    if emit_kn:
        out_specs.append(pl.BlockSpec((tm, LANES), lambda i: (i, 0)))
        out_shape.append(jax.ShapeDtypeStruct((t, LANES), F32))
    return pl.pallas_call(
        functools.partial(_qkprep_kernel, rope=tables is not None, emit_kn=emit_kn),
        grid=(t // tm,),
        in_specs=in_specs,
        out_specs=out_specs,
        out_shape=out_shape,
        compiler_params=_cp("parallel"),
        name="qkprep",
    )(*args)


def _attn_kernel(*refs, has_ctx):
    if has_ctx:
        q_ref, k_ref, v_ref, kc_ref, vc_ref, o_ref = refs
    else:
        q_ref, k_ref, v_ref, o_ref = refs
    k = k_ref[...]
    v = v_ref[...]
    hq = range(4)
    qhs = [q_ref[:, h * HD:(h + 1) * HD] for h in hq]
    ss = [_dot_nt(qh, k) for qh in qhs]
    ms = [jnp.max(s, axis=1, keepdims=True) for s in ss]
    if has_ctx:
        scs = [_dot_nt(qh, kc_ref[...]) for qh in qhs]
        ms = [jnp.maximum(m, jnp.max(sc, axis=1, keepdims=True)) for m, sc in zip(ms, scs)]
    oes = [_dot(jnp.exp(s - m).astype(BF16), v) for s, m in zip(ss, ms)]
    if has_ctx:
        oes = [oe + _dot(jnp.exp(sc - m).astype(BF16), vc_ref[...]) for oe, sc, m in zip(oes, scs, ms)]
    outs = [oe[:, :HD] / oe[:, HD:HD + 1] for oe in oes]
    o_ref[...] = jnp.concatenate(outs, axis=1).astype(o_ref.dtype)


def _attn(qn, kh, vh, bsz, s_len, ctx=None):
    t = qn.shape[0]
    tq = min(512, s_len)
    nq = s_len // tq
    in_specs = [pl.BlockSpec((tq, 256), lambda b, g, i: (b * nq + i, g)),
                pl.BlockSpec((None, s_len, HD), lambda b, g, i: (g, b, 0)),
                pl.BlockSpec((None, s_len, LANES), lambda b, g, i: (g, b, 0))]
    args = [qn, kh, vh]
    if ctx is not None:
        past = ctx[0].shape[1] // bsz
        in_specs += [pl.BlockSpec((None, past, HD), lambda b, g, i: (g, b, 0)),
                     pl.BlockSpec((None, past, LANES), lambda b, g, i: (g, b, 0))]
        args += list(ctx)
    return pl.pallas_call(
        functools.partial(_attn_kernel, has_ctx=ctx is not None),
        grid=(bsz, KV_B, nq),
        in_specs=in_specs,
        out_specs=pl.BlockSpec((tq, 256), lambda b, g, i: (b * nq + i, g)),
        out_shape=jax.ShapeDtypeStruct((t, W_A), BF16),
        compiler_params=_cp("parallel", "parallel", "parallel"),
        name="attn",
    )(*args)


def _out_kernel(*refs, n_in, router):
    a_refs = refs[:n_in]
    w_refs = refs[n_in:2 * n_in]
    x_ref, g1_ref, sh2_ref, sc2_ref, lng_ref, lnb_ref = refs[2 * n_in:2 * n_in + 6]
    pos = 2 * n_in + 6
    if router:
        rw_ref, rb_ref = refs[pos:pos + 2]
        pos += 2
    x1_ref, u2_ref = refs[pos:pos + 2]
    y = _dot(a_refs[0][...], w_refs[0][...])
    for a_ref, w_ref in zip(a_refs[1:], w_refs[1:]):
        y = y + _dot(a_ref[...], w_ref[...])
    x1 = _layer_norm(ALPHA * x_ref[...] + g1_ref[...] * y, lng_ref[...], lnb_ref[...])
    x1_ref[...] = x1
    u2 = x1 * (1.0 + sc2_ref[...]) + sh2_ref[...]
    uh, ul = _split2(u2)
    u2_ref[...] = uh
    if router:
        gate_ref = refs[pos + 2]
        logits = _dot_hilo(uh, ul, rw_ref) + rb_ref[...]
        lane = lax.broadcasted_iota(jnp.int32, logits.shape, 1)
        m1 = jnp.max(logits, axis=1, keepdims=True)
        i1 = jnp.min(jnp.where(logits == m1, lane, LANES), axis=1, keepdims=True)
        rest = jnp.where(lane == i1, 2.0 * NEG, logits)
        m2 = jnp.max(rest, axis=1, keepdims=True)
        i2 = jnp.min(jnp.where(rest == m2, lane, LANES), axis=1, keepdims=True)
        e = jnp.exp(m2 - m1)
        p1 = 1.0 / (1.0 + e)
        gate_ref[...] = jnp.where(lane == i1, p1, jnp.where(lane == i2, e * p1, 0.0))


def _out_proj(acts, ws, x, mods3, row_fn, ln_g, ln_b, router=None):
    t = x.shape[0]
    tm = 512
    n_in = len(acts)
    in_specs = [pl.BlockSpec((tm, a.shape[1]), lambda i: (i, 0)) for a in acts]
    in_specs += [pl.BlockSpec(w.shape, lambda i: (0, 0)) for w in ws]
    in_specs += [pl.BlockSpec((tm, D), lambda i: (i, 0))]
    in_specs += [pl.BlockSpec((None, 1, D), functools.partial(lambda i, k: (row_fn(i, tm), 0, k), k=k))
                 for k in (2, 3, 4)]
    in_specs += [pl.BlockSpec((1, D), lambda i: (0, 0))] * 2
    args = list(acts) + list(ws) + [x, mods3, mods3, mods3, ln_g.reshape(1, D), ln_b.reshape(1, D)]
    out_specs = [pl.BlockSpec((tm, D), lambda i: (i, 0)), pl.BlockSpec((tm, D), lambda i: (i, 0))]
    out_shape = [jax.ShapeDtypeStruct((t, D), F32), jax.ShapeDtypeStruct((t, D), BF16)]
    if router is not None:
        in_specs += [pl.BlockSpec((D, 2 * LANES), lambda i: (0, 0)), pl.BlockSpec((1, LANES), lambda i: (0, 0))]
        args += list(router)
        out_specs.append(pl.BlockSpec((tm, LANES), lambda i: (i, 0)))
        out_shape.append(jax.ShapeDtypeStruct((t, LANES), F32))
    return pl.pallas_call(
        functools.partial(_out_kernel, n_in=n_in, router=router is not None),
        grid=(t // tm,),
        in_specs=in_specs,
        out_specs=out_specs,
        out_shape=out_shape,
        compiler_params=_cp("parallel"),
        name="out_proj",
    )(*args)


def _ffn_kernel(u_ref, w1_ref, w3_ref, w2_ref, x1_ref, g2_ref, lng_ref, lnb_ref, o_ref, *, tf):
    u = u_ref[...]
    acc = None
    for c in range(w1_ref.shape[1] // tf):
        cs = slice(c * tf, (c + 1) * tf)
        h = _silu(_dot(u, w1_ref[:, cs])) * _dot(u, w3_ref[:, cs])
        part = _dot(h.astype(BF16), w2_ref[cs, :])
        acc = part if acc is None else acc + part
    o_ref[...] = _layer_norm(ALPHA * x1_ref[...] + g2_ref[...] * acc, lng_ref[...], lnb_ref[...])


def _ffn(u2, w1, w3, w2, x1, mods3, row_fn, ln_g, ln_b, tf):
    t = u2.shape[0]
    tm = 512
    f = w1.shape[1]
    once = pl.Buffered(1)
    return pl.pallas_call(
        functools.partial(_ffn_kernel, tf=tf),
        grid=(t // tm,),
        in_specs=[pl.BlockSpec((tm, D), lambda i: (i, 0)),
                  pl.BlockSpec((D, f), lambda i: (0, 0), pipeline_mode=once),
                  pl.BlockSpec((D, f), lambda i: (0, 0), pipeline_mode=once),
                  pl.BlockSpec((f, D), lambda i: (0, 0), pipeline_mode=once),
                  pl.BlockSpec((tm, D), lambda i: (i, 0)),
                  pl.BlockSpec((None, 1, D), lambda i: (row_fn(i, tm), 0, 5)),
                  pl.BlockSpec((1, D), lambda i: (0, 0)),
                  pl.BlockSpec((1, D), lambda i: (0, 0))],
        out_specs=pl.BlockSpec((tm, D), lambda i: (i, 0)),
        out_shape=jax.ShapeDtypeStruct((t, D), F32),
        compiler_params=_cp("parallel"),
        name="ffn",
    )(u2, w1, w3, w2, x1, mods3, ln_g.reshape(1, D), ln_b.reshape(1, D))


MOE_CAP = 256


def _moe_kernel(u_ref, w1_ref, w3_ref, w2_ref, gate_ref, tril_ref, x1_ref, g2_ref, lng_ref, lnb_ref, o_ref,
                acc_scr, rank_scr, rank_t_scr, gate_t_scr):
    e = pl.program_id(1)
    tm = u_ref.shape[0]

    @pl.when(e == 0)
    def _():
        acc_scr[...] = jnp.zeros(acc_scr.shape, F32)
        gate = gate_ref[...]
        routed = gate > 0.0
        rank = jnp.where(routed, _dot(tril_ref[...], jnp.where(routed, 1.0, 0.0).astype(BF16)), -1.0)
        rank_scr[...] = rank
        rank_t_scr[...] = rank.T
        gate_t_scr[...] = gate.T

    lane = lax.broadcasted_iota(jnp.int32, (tm, LANES), 1)
    rank_col = jnp.sum(jnp.where(lane == e, rank_scr[...], 0.0), axis=1, keepdims=True)
    rank_row = rank_t_scr[pl.ds(e, 1), :]
    gate_row = gate_t_scr[pl.ds(e, 1), :]
    count = jnp.sum(jnp.where(rank_row >= 0.0, 1, 0))
    u = u_ref[...]

    def one_pass(first_slot, cap):
        off = first_slot.astype(F32)
        slot_c = lax.broadcasted_iota(jnp.int32, (cap, tm), 0).astype(F32) + off
        sel = jnp.where(rank_row == slot_c, 1.0, 0.0)
        weight = jnp.sum(sel * gate_row, axis=1, keepdims=True)
        xe = _dot(sel.astype(BF16), u).astype(BF16)
        h = _silu(_dot(xe, w1_ref[...])) * _dot(xe, w3_ref[...])
        y = _dot(h.astype(BF16), w2_ref[...]) * weight
        slot_r = lax.broadcasted_iota(jnp.int32, (tm, cap), 1).astype(F32) + off
        sel_t = jnp.where(rank_col == slot_r, 1.0, 0.0).astype(BF16)
        acc_scr[...] += _dot(sel_t, y.astype(BF16))

    n_full = count // MOE_CAP
    left = count - n_full * MOE_CAP
    n_full = n_full + jnp.where(left > MOE_CAP // 2, 1, 0)

    def full_pass(blk, carry):
        one_pass(blk * MOE_CAP, MOE_CAP)
        return carry

    lax.fori_loop(0, n_full, full_pass, 0)

    @pl.when(jnp.logical_and(left > 0, left <= MOE_CAP // 2))
    def _():
        one_pass(n_full * MOE_CAP, MOE_CAP // 2)

    @pl.when(e == pl.num_programs(1) - 1)
    def _():
        o_ref[...] = _layer_norm(ALPHA * x1_ref[...] + g2_ref[...] * acc_scr[...], lng_ref[...], lnb_ref[...])


def _moe(u2, w1, w3, w2, gate, x1, mods3, row_fn, ln_g, ln_b, tm):
    t = u2.shape[0]
    n_e, _, f = w1.shape
    tril = jnp.tril(jnp.ones((tm, tm), F32), -1).astype(BF16)
    once = pl.Buffered(1)
    in_specs = [pl.BlockSpec((tm, D), lambda i, e: (i, 0)),
                pl.BlockSpec((None, D, f), lambda i, e: (e, 0, 0)),
                pl.BlockSpec((None, D, f), lambda i, e: (e, 0, 0)),
                pl.BlockSpec((None, f, D), lambda i, e: (e, 0, 0)),
                pl.BlockSpec((tm, LANES), lambda i, e: (i, 0)),
                pl.BlockSpec((tm, tm), lambda i, e: (0, 0), pipeline_mode=once),
                pl.BlockSpec((tm, D), lambda i, e: (i, 0), pipeline_mode=once),
                pl.BlockSpec((None, 1, D), lambda i, e: (row_fn(i, tm), 0, 5)),
                pl.BlockSpec((1, D), lambda i, e: (0, 0)),
                pl.BlockSpec((1, D), lambda i, e: (0, 0))]
    return pl.pallas_call(
        _moe_kernel,
        grid=(t // tm, n_e),
        in_specs=in_specs,
        out_specs=pl.BlockSpec((tm, D), lambda i, e: (i, 0)),
        out_shape=jax.ShapeDtypeStruct((t, D), F32),
        scratch_shapes=[pltpu.VMEM((tm, D), F32), pltpu.VMEM((tm, LANES), F32),
                        pltpu.VMEM((LANES, tm), F32), pltpu.VMEM((LANES, tm), F32)],
        compiler_params=_cp("parallel", "arbitrary"),
        name="moe",
    )(u2, w1, w3, w2, gate, tril, x1, mods3, ln_g.reshape(1, D), ln_b.reshape(1, D))


def _conv_kernel(x_ref, w_ref, b_ref, o_ref, *, s_len, transpose):
    x = x_ref[...].astype(F32)
    rows = x.shape[0]
    t = lax.broadcasted_iota(jnp.int32, x.shape, 0) % s_len
    prev = jnp.where(t == 0, 0.0, pltpu.roll(x, 1, axis=0))
    nxt = jnp.where(t == s_len - 1, 0.0, pltpu.roll(x, rows - 1, axis=0))
    y = _silu(w_ref[0:1, :] * prev + w_ref[1:2, :] * x + w_ref[2:3, :] * nxt + b_ref[...])
    if transpose:
        for q in range(rows // s_len):
            o_ref[q] = y[q * s_len:(q + 1) * s_len, :].T.astype(o_ref.dtype)
    else:
        o_ref[...] = y.astype(o_ref.dtype)


def _conv(pm, conv_w, conv_b, bsz, s_len, col_off, transpose):
    t = pm.shape[0]
    tc = 512
    n_ch = conv_w.shape[1]
    off = col_off // tc
    n_seq = max(1, min(bsz, 2048 // s_len))
    rows = n_seq * s_len
    if transpose:
        out_spec = pl.BlockSpec((n_seq, tc, s_len), lambda b, c: (b, c, 0))
        out_shape = jax.ShapeDtypeStruct((bsz, n_ch, s_len), BF16)
    else:
        out_spec = pl.BlockSpec((rows, tc), lambda b, c: (b, c))
        out_shape = jax.ShapeDtypeStruct((t, n_ch), BF16)
    return pl.pallas_call(
        functools.partial(_conv_kernel, s_len=s_len, transpose=transpose),
        grid=(bsz // n_seq, n_ch // tc),
        in_specs=[pl.BlockSpec((rows, tc), lambda b, c: (b, off + c)),
                  pl.BlockSpec((3, tc), lambda b, c: (0, c)),
                  pl.BlockSpec((1, tc), lambda b, c: (0, c))],
        out_specs=out_spec,
        out_shape=out_shape,
        compiler_params=_cp("parallel", "parallel"),
        name="ssd_conv",
    )(pm, conv_w, conv_b.reshape(1, n_ch))


def _ssd_kernel(*refs, nc, has_init, out_state):
    xt_ref, b_ref, c_ref, dt_ref, dtn_ref, z_ref, dtb_ref, alog_ref, dsk_ref, ng_ref = refs[:10]
    pos = 10
    if has_init:
        s0_ref = refs[pos]
        pos += 1
    yn_ref = refs[pos]
    pos += 1
    if out_state:
        sst_ref = refs[pos]
        pos += 1
    s_scr, yb_scr, yt_scr, xw_scr, cum_scr, rows_scr = refs[pos:pos + 6]

    j = pl.program_id(1)
    is_fwd = j >= nc
    ci = jnp.where(is_fwd, j - nc, nc - 1 - j)
    sgn = jnp.where(is_fwd, 1, -1)

    def _init(d):
        if has_init:
            s_scr[...] = s0_ref[d]
        else:
            s_scr[...] = jnp.zeros(s_scr.shape, F32)

    pl.when(j == 0)(lambda: _init(1))
    pl.when(j == nc)(lambda: _init(0))

    def _prep(dtraw_ref, sg, slot):
        row = lax.broadcasted_iota(jnp.int32, (CH, CH), 0)
        col = lax.broadcasted_iota(jnp.int32, (CH, CH), 1)
        tri = jnp.where((col - row) * sg <= 0, 1.0, 0.0).astype(BF16)
        f_row = (jnp.zeros((1, LANES), jnp.int32) + sg) > 0
        f_mat = (jnp.zeros((CH, LANES), jnp.int32) + sg) > 0
        f_hc = (jnp.zeros((H_C, LANES), jnp.int32) + sg) > 0
        raw = dtraw_ref[...] + dtb_ref[...]
        raw = jnp.where(f_mat, raw, pltpu.roll(raw, LANES - H_C, axis=1))
        alog = alog_ref[...]
        alog = jnp.where(f_row, alog, pltpu.roll(alog, LANES - H_C, axis=1))
        dtv = _softplus(raw)
        cumv = _tri_dot(tri, dtv * (-jnp.exp(alog)))
        cumv_t = cumv.T[0:H_C, :]
        dtv_t = dtv.T[0:H_C, :]
        last = jnp.where(f_hc, jnp.broadcast_to(cumv_t[:, CH - 1:CH], (H_C, LANES)),
                         jnp.broadcast_to(cumv_t[:, 0:1], (H_C, LANES)))
        cum_scr[slot] = cumv * LOG2E
        rows_scr[slot, 0] = cumv_t * LOG2E
        rows_scr[slot, 1] = dtv_t
        rows_scr[slot, 2] = jnp.exp(cumv_t)
        rows_scr[slot, 3] = jnp.exp(last - cumv_t) * dtv_t
        rows_scr[slot, 4] = jnp.exp(last)

    pl.when(j == 0)(lambda: _prep(dt_ref, sgn, 0))

    slot = j % 2
    row = lax.broadcasted_iota(jnp.int32, (CH, CH), 0)
    col = lax.broadcasted_iota(jnp.int32, (CH, CH), 1)
    causal_t = (row - col) * sgn <= 0
    cum = cum_scr[slot]
    cum_t = rows_scr[slot, 0]
    dt_t = rows_scr[slot, 1]
    ecum_t = rows_scr[slot, 2]
    ws_t = rows_scr[slot, 3]
    dec = rows_scr[slot, 4]
    _prep(dtn_ref, jnp.where(j + 1 >= nc, 1, -1), (j + 1) % 2)

    for gi in range(NG_C):
        gsl = slice(gi * D_STATE, (gi + 1) * D_STATE)
        bg = b_ref[:, gsl]
        cg = c_ref[:, gsl]
        cb_t = _dot_nt(bg, cg)
        ys_t = _dot_nt(s_scr[gi].astype(BF16), cg)
        for half in range(2):
            hr = range(half * 4, half * 4 + 4)
            rs = {rr: gi * HPG_C + rr for rr in hr}
            w_ts = {rr: (cb_t * jnp.exp2(jnp.where(causal_t, cum_t[rs[rr]:rs[rr] + 1, :] - cum[:, rs[rr]:rs[rr] + 1],
                                                   NEG))).astype(BF16)
                    for rr in hr}
            xr_ts = {rr: xt_ref[rs[rr] * HD:(rs[rr] + 1) * HD, :].astype(F32) for rr in hr}
            xd_ts = {rr: (xr_ts[rr] * dt_t[rs[rr]:rs[rr] + 1, :]).astype(BF16) for rr in hr}
            y_ts = {rr: _dot(xd_ts[rr], w_ts[rr]) + ys_t[rr * HD:(rr + 1) * HD, :] * ecum_t[rs[rr]:rs[rr] + 1, :]
                    for rr in hr}
            for rr in hr:
                yt_scr[rs[rr] * HD:(rs[rr] + 1) * HD, :] = y_ts[rr]
                xw_scr[rr * HD:(rr + 1) * HD, :] = (xr_ts[rr] * ws_t[rs[rr]:rs[rr] + 1, :]).astype(BF16)
        upd = _dot(xw_scr[...], bg)
        for rr in range(HPG_C):
            r = gi * HPG_C + rr
            hsl = slice(rr * HD, (rr + 1) * HD)
            s_scr[gi, hsl, :] = s_scr[gi, hsl, :] * dec[r:r + 1, :] + upd[hsl, :]

    @pl.when(j < nc)
    def _():
        yb_scr[ci] = yt_scr[...]

    @pl.when(j >= nc)
    def _():
        yt = yt_scr[...] + yb_scr[ci] + dsk_ref[...] * xt_ref[...].astype(F32)
        y = yt.T * _silu(z_ref[...].astype(F32))
        outs = []
        gw = D_INNER // NG_C
        for gi in range(NG_C):
            yg = y[:, gi * gw:(gi + 1) * gw]
            outs.append(yg * lax.rsqrt(jnp.mean(yg * yg, axis=1, keepdims=True) + EPS))
        yn_ref[...] = (jnp.concatenate(outs, axis=1) * ng_ref[...]).astype(yn_ref.dtype)

    if out_state:
        @pl.when(j == nc - 1)
        def _():
            sst_ref[1] = s_scr[...]

        @pl.when(j == 2 * nc - 1)
        def _():
            sst_ref[0] = s_scr[...]


def _ssd(xt, bc, pa, pm, dtb, alog, dsk, norm_g, bsz, s_len, init=None, out_state=False):
    nc = s_len // CH
    t = bsz * s_len
    gw = NG_C * D_STATE

    def chunk_i(j):
        return jnp.where(j >= nc, j - nc, nc - 1 - j)

    def chunk(b, j):
        return b * nc + chunk_i(j)

    def chunk_next(b, j):
        return chunk(b, jnp.minimum(j + 1, 2 * nc - 1))

    def chunk_fwd(b, j):
        return b * nc + jnp.maximum(j - nc, 0)

    in_specs = [pl.BlockSpec((None, D_INNER, CH), lambda b, j: (b, 0, chunk_i(j))),
                pl.BlockSpec((CH, gw), lambda b, j: (chunk(b, j), 0)),
                pl.BlockSpec((CH, gw), lambda b, j: (chunk(b, j), 1)),
                pl.BlockSpec((CH, LANES), lambda b, j: (chunk(b, j), 0)),
                pl.BlockSpec((CH, LANES), lambda b, j: (chunk_next(b, j), 0)),
                pl.BlockSpec((CH, D_INNER), lambda b, j: (chunk_fwd(b, j), 0)),
                pl.BlockSpec((1, LANES), lambda b, j: (0, 0)),
                pl.BlockSpec((1, LANES), lambda b, j: (0, 0)),
                pl.BlockSpec((D_INNER, LANES), lambda b, j: (0, 0)),
                pl.BlockSpec((1, D_INNER), lambda b, j: (0, 0))]
    args = [xt, bc, bc, pa, pa, pm, dtb, alog, dsk, norm_g]
    state_spec = pl.BlockSpec((None, 2, NG_C, HPG_C * HD, D_STATE), lambda b, j: (b, 0, 0, 0, 0))
    if init is not None:
        in_specs.append(state_spec)
        args.append(init)
    out_specs = [pl.BlockSpec((CH, D_INNER), lambda b, j: (chunk_fwd(b, j), 0))]
    out_shape = [jax.ShapeDtypeStruct((t, D_INNER), BF16)]
    if out_state:
        out_specs.append(state_spec)
        out_shape.append(jax.ShapeDtypeStruct((bsz, 2, NG_C, HPG_C * HD, D_STATE), F32))
    return pl.pallas_call(
        functools.partial(_ssd_kernel, nc=nc, has_init=init is not None, out_state=out_state),
        grid=(bsz, 2 * nc),
        in_specs=in_specs,
        out_specs=out_specs,
        out_shape=out_shape,
        scratch_shapes=[pltpu.VMEM((NG_C, HPG_C * HD, D_STATE), F32),
                        pltpu.VMEM((nc, D_INNER, CH), F32),
                        pltpu.VMEM((D_INNER, CH), F32),
                        pltpu.VMEM((HPG_C * HD, CH), BF16),
                        pltpu.VMEM((2, CH, LANES), F32), pltpu.VMEM((2, 5, H_C, LANES), F32)],
        compiler_params=_cp("parallel", "arbitrary"),
        name="ssd_scan",
    )(*args)


def _rope_tables(s_len):
    nf = HD // 4
    inv = ROPE_THETA ** (-jnp.arange(nf, dtype=F32) / nf)
    tpos = jnp.arange(s_len)
    ang_r = (tpos // GRID_W).astype(F32)[:, None] * inv
    ang_c = (tpos % GRID_W).astype(F32)[:, None] * inv
    cos = jnp.concatenate([jnp.cos(ang_r)] * 2 + [jnp.cos(ang_c)] * 2, axis=1)
    sin = jnp.concatenate([-jnp.sin(ang_r), jnp.sin(ang_r), -jnp.sin(ang_c), jnp.sin(ang_c)], axis=1)
    return jnp.tile(cos, (1, 2)), jnp.tile(sin, (1, 2))


def _block_avg(n):
    idx = np.arange(n) // HD
    return jnp.asarray((idx[:, None] == idx[None, :]).astype(np.float32) / HD, dtype=BF16)


def _pad_cols(w, n):
    return jnp.pad(w, ((0, 0), (0, n - w.shape[1])))


def _row_prompt(i, tm):
    return 0


def _make_row_sample(s_len):
    def row(i, tm):
        return 1 + (i * tm) // s_len
    return row


def kernel(x_prompt, x_sample, cache_k_l0, cache_v_l0, state_mlstm_c_l0, state_mlstm_n_l0, state_mlstm_m_l0, state_ssd_l1, c, c_ctx, ada_w_l0, ada_b_l0, mix_w_in_l0, mlstm_gate_b_l0, mlstm_norm_g_l0, q_norm_g_l0, k_norm_g_l0, mix_w_out_l0, ln1_g_l0, ln1_b_l0, ln2_g_l0, ln2_b_l0, ffn_w1_l0, ffn_w3_l0, ffn_w2_l0, ada_w_l1, ada_b_l1, ssd_w_in_l1, ssd_conv_w_l1, ssd_conv_b_l1, ssd_dt_bias_l1, ssd_a_log_l1, ssd_d_l1, ssd_norm_g_l1, ssd_w_out_l1, ln1_g_l1, ln1_b_l1, ln2_g_l1, ln2_b_l1, router_w_l1, router_b_l1, moe_w1_l1, moe_w3_l1, moe_w2_l1):
    bp, sp, _ = x_prompt.shape
    bs, ss, _ = x_sample.shape
    past = cache_k_l0.shape[1]
    groups = [
        dict(x=x_prompt.reshape(bp * sp, D), b=bp, s=sp, row=_row_prompt, prompt=True),
        dict(x=x_sample.reshape(bs * ss, D), b=bs, s=ss, row=_make_row_sample(ss), prompt=False),
    ]

    cvec = jnp.zeros((16, D), F32).at[0].set(c_ctx).at[1:1 + bs].set(c)
    mods0 = _ada(cvec, ada_w_l0, ada_b_l0).reshape(16, 1, 6 * D)
    mods1 = _ada(cvec, ada_w_l1, ada_b_l1).reshape(16, 1, 6 * D)

    w0_main = jnp.concatenate([mix_w_in_l0[:, :2048], mix_w_in_l0[:, 2080:]], axis=1).astype(BF16)
    w0_aux = _hilo(_pad_cols(mix_w_in_l0[:, 2048:2080], LANES))
    gbias = _pad_cols(mlstm_gate_b_l0.reshape(1, 4 * H_A), LANES)
    avg128 = _block_avg(LANES)
    q_g = jnp.tile(q_norm_g_l0, W_A // HD).reshape(1, W_A)
    k_g = jnp.tile(k_norm_g_l0, KV_B).reshape(1, LANES)
    w_out_h = mix_w_out_l0[:W_A].astype(BF16)
    w_out_o = mix_w_out_l0[W_A:].astype(BF16)
    ffn_w1 = ffn_w1_l0.astype(BF16)
    ffn_w3 = ffn_w3_l0.astype(BF16)
    ffn_w2 = ffn_w2_l0.astype(BF16)
    tables = _rope_tables(ss)
    kc = cache_k_l0.transpose(2, 0, 1, 3).reshape(KV_B, bs * past, HD).astype(BF16)
    vc = cache_v_l0.transpose(2, 0, 1, 3).reshape(KV_B, bs * past, HD)
    vc = jnp.concatenate([vc, jnp.ones_like(vc)], axis=-1).astype(BF16)
    c0 = jnp.concatenate([jnp.swapaxes(state_mlstm_c_l0, -1, -2),
                          jnp.broadcast_to(state_mlstm_n_l0[..., None, :], state_mlstm_c_l0.shape)], axis=-2)
    m0 = jnp.broadcast_to(state_mlstm_m_l0[..., None], state_mlstm_m_l0.shape + (LANES,))

    n_main1 = 2 * D_INNER + 2 * NG_C * D_STATE
    w1_main = ssd_w_in_l1[:, :n_main1].astype(BF16)
    w1_aux = _hilo(_pad_cols(ssd_w_in_l1[:, n_main1:], LANES))
    dtb = _pad_cols(ssd_dt_bias_l1.reshape(1, 2 * H_C), LANES)
    alog = _pad_cols(ssd_a_log_l1.reshape(1, 2 * H_C), LANES)
    dsk = jnp.broadcast_to(jnp.repeat(ssd_d_l1, HD)[:, None], (D_INNER, LANES))
    ssd_ng = ssd_norm_g_l1.reshape(1, D_INNER)
    w_out1 = ssd_w_out_l1.astype(BF16)
    rw_hilo = _hilo(_pad_cols(router_w_l1, LANES))
    rb = jnp.full((1, LANES), NEG, F32).at[0, :N_EXPERTS].set(router_b_l1)
    moe_w1 = moe_w1_l1.astype(BF16)
    moe_w3 = moe_w3_l1.astype(BF16)
    moe_w2 = moe_w2_l1.astype(BF16)
    s0 = state_ssd_l1.reshape(bs, 2, NG_C, HPG_C * HD, D_STATE)

    outs = {}
    for grp in groups:
        x, b, s, row, prompt = grp["x"], grp["b"], grp["s"], grp["row"], grp["prompt"]
        pm, pa = _mm_in(x, mods0, row, w0_main, w0_aux, tm=512, n_chunks=2)
        if prompt:
            hn, cst, mst = _mlstm(pm, pa, gbias, mlstm_norm_g_l0.reshape(1, W_A), b, s, out_state=True)
            qn, kh, vh, kn = _qkprep(pm, q_g, k_g, avg128, s, emit_kn=True)
            o = _attn(qn, kh, vh, b, s)
            outs["k"] = kn.reshape(b, s, KV_B, HD)
            outs["v"] = pm[:, 2688:2816].astype(F32).reshape(b, s, KV_B, HD)
            outs["c"] = jnp.swapaxes(cst[..., :HD, :], -1, -2)
            outs["n"] = cst[..., HD, :]
            outs["m"] = mst[..., 0]
        else:
            hn = _mlstm(pm, pa, gbias, mlstm_norm_g_l0.reshape(1, W_A), b, s, init=(c0, m0))[0]
            qn, kh, vh = _qkprep(pm, q_g, k_g, avg128, s, tables=tables)
            o = _attn(qn, kh, vh, b, s, ctx=(kc, vc))
        x1, u2 = _out_proj([hn, o], [w_out_h, w_out_o], x, mods0, row, ln1_g_l0, ln1_b_l0)
        x2 = _ffn(u2, ffn_w1, ffn_w3, ffn_w2, x1, mods0, row, ln2_g_l0, ln2_b_l0, tf=D_FF // 2)
        pm, pa = _mm_in(x2, mods1, row, w1_main, w1_aux, tm=512, n_chunks=4)
        xt = _conv(pm, ssd_conv_w_l1[:, :D_INNER], ssd_conv_b_l1[:D_INNER], b, s, D_INNER, True)
        bc = _conv(pm, ssd_conv_w_l1[:, D_INNER:], ssd_conv_b_l1[D_INNER:], b, s, 2 * D_INNER, False)
        if prompt:
            yn, sst = _ssd(xt, bc, pa, pm, dtb, alog, dsk, ssd_ng, b, s, out_state=True)
            outs["s"] = sst.reshape(b, 2, H_C, HD, D_STATE)
        else:
            yn = _ssd(xt, bc, pa, pm, dtb, alog, dsk, ssd_ng, b, s, init=s0)[0]
        x1, u2, gate = _out_proj([yn], [w_out1], x2, mods1, row, ln1_g_l1, ln1_b_l1, router=(rw_hilo, rb))
        x3 = _moe(u2, moe_w1, moe_w3, moe_w2, gate, x1, mods1, row, ln2_g_l1, ln2_b_l1,
                  tm=min(1024, b * s if prompt else s))
        outs["y_p" if prompt else "y_s"] = x3.reshape(b, s, D)

    return (outs["y_p"], outs["y_s"], outs["k"], outs["v"], outs["c"], outs["n"], outs["m"], outs["s"])
```

```python
import functools

import jax
import jax.numpy as jnp
import numpy as np
from jax import lax
from jax.experimental import pallas as pl
from jax.experimental.pallas import tpu as pltpu

F32 = jnp.float32
BF16 = jnp.bfloat16

D = 1024
CH = 128
LANES = 128
EPS = 1e-6
DEPTH = 2
ALPHA = (2 * DEPTH) ** 0.25
GRID_W = 64
ROPE_THETA = 10000.0
H_A = 8
HD = 64
W_A = 512
KV_B = 2
H_C = 32
NG_C = 4
HPG_C = 8
D_STATE = 128
D_INNER = 2048
D_FF = 2816
N_EXPERTS = 8
D_FF_E = 1408
NEG = -1e30
LOG2E = 1.4426950408889634
VMEM_LIMIT = 56 * 1024 * 1024


def _cp(*sem):
    return pltpu.CompilerParams(dimension_semantics=sem, vmem_limit_bytes=VMEM_LIMIT)


def _sigmoid(x):
    return 1.0 / (1.0 + jnp.exp(-x))


def _silu(x):
    return x * _sigmoid(x)


def _softplus(x):
    return jnp.maximum(x, 0.0) + jnp.log(1.0 + jnp.exp(-jnp.abs(x)))


def _split2(x):
    hi = x.astype(BF16)
    lo = (x - hi.astype(F32)).astype(BF16)
    return hi, lo


def _split3(x):
    h1 = x.astype(BF16)
    r = x - h1.astype(F32)
    h2 = r.astype(BF16)
    h3 = (r - h2.astype(F32)).astype(BF16)
    return h1, h2, h3


def _dot(a, b):
    return jnp.dot(a, b, preferred_element_type=F32)


def _dot_nt(a, b):
    return lax.dot_general(a, b, (((1,), (1,)), ((), ())), preferred_element_type=F32)


def _tri_dot(tri, x):
    h1, h2, h3 = _split3(x)
    return _dot(tri, h1) + _dot(tri, h2) + _dot(tri, h3)


def _layer_norm(r, g, b):
    mu = jnp.mean(r, axis=-1, keepdims=True)
    d = r - mu
    var = jnp.mean(d * d, axis=-1, keepdims=True)
    return d * lax.rsqrt(var + EPS) * g + b


def _ada_kernel(c_ref, w_ref, b_ref, o_ref):
    s = _silu(c_ref[...])
    sh, sl = _split2(s)
    wh, wl = _split2(w_ref[...])
    o_ref[...] = _dot(sh, wh) + _dot(sl, wh) + _dot(sh, wl) + b_ref[...]


def _ada(cvec, w, b):
    n = w.shape[1]
    tn = 768
    return pl.pallas_call(
        _ada_kernel,
        grid=(n // tn,),
        in_specs=[pl.BlockSpec((16, D), lambda j: (0, 0)),
                  pl.BlockSpec((D, tn), lambda j: (0, j)),
                  pl.BlockSpec((1, tn), lambda j: (0, j))],
        out_specs=pl.BlockSpec((16, tn), lambda j: (0, j)),
        out_shape=jax.ShapeDtypeStruct((16, n), F32),
        compiler_params=_cp("parallel"),
        name="ada_mods",
    )(cvec, w, b.reshape(1, n))


def _dot_hilo(uh, ul, whl_ref):
    both = _dot(uh, whl_ref[...])
    return both[:, :LANES] + both[:, LANES:] + _dot(ul, whl_ref[:, :LANES])


def _hilo(w):
    return jnp.concatenate(_split2(w), axis=1)


def _mm_in_kernel(x_ref, sh_ref, sc_ref, wm_ref, wa_ref, om_ref, oa_ref, *, n_chunks):
    u = x_ref[...] * (1.0 + sc_ref[...]) + sh_ref[...]
    uh, ul = _split2(u)
    oa_ref[...] = _dot_hilo(uh, ul, wa_ref)
    tn = wm_ref.shape[1] // n_chunks
    for c in range(n_chunks):
        om_ref[:, c * tn:(c + 1) * tn] = _dot(uh, wm_ref[:, c * tn:(c + 1) * tn]).astype(om_ref.dtype)


def _mm_in(x, mods3, row_fn, wm, wa_hilo, tm, n_chunks):
    t, nm = x.shape[0], wm.shape[1]
    once = pl.Buffered(1)
    return pl.pallas_call(
        functools.partial(_mm_in_kernel, n_chunks=n_chunks),
        grid=(t // tm,),
        in_specs=[pl.BlockSpec((tm, D), lambda i: (i, 0)),
                  pl.BlockSpec((None, 1, D), lambda i: (row_fn(i, tm), 0, 0)),
                  pl.BlockSpec((None, 1, D), lambda i: (row_fn(i, tm), 0, 1)),
                  pl.BlockSpec((D, nm), lambda i: (0, 0), pipeline_mode=once),
                  pl.BlockSpec((D, 2 * LANES), lambda i: (0, 0), pipeline_mode=once)],
        out_specs=[pl.BlockSpec((tm, nm), lambda i: (i, 0)),
                   pl.BlockSpec((tm, LANES), lambda i: (i, 0))],
        out_shape=[jax.ShapeDtypeStruct((t, nm), BF16), jax.ShapeDtypeStruct((t, LANES), F32)],
        compiler_params=_cp("parallel"),
        name="mm_in",
    )(x, mods3, mods3, wm, wa_hilo)


def _mlstm_kernel(*refs, nc, has_init, out_state):
    q_ref, k_ref, v_ref, om_ref, g_ref, gn_ref, gb_ref, ng_ref = refs[:8]
    pos = 8
    if has_init:
        c0_ref, m0_ref = refs[pos:pos + 2]
        pos += 2
    hn_ref = refs[pos]
    pos += 1
    if out_state:
        cst_ref, mst_ref = refs[pos:pos + 2]
        pos += 2
    c_scr, m_scr, hb_scr, r_scr, rows_scr = refs[pos:pos + 5]

    j = pl.program_id(1)
    is_fwd = j >= nc
    ci = jnp.where(is_fwd, j - nc, nc - 1 - j)
    sgn = jnp.where(is_fwd, 1, -1)

    def _init(d):
        if has_init:
            c_scr[...] = c0_ref[d]
            m_scr[...] = m0_ref[d]
        else:
            c_scr[...] = jnp.zeros(c_scr.shape, F32)
            m_scr[...] = jnp.zeros(m_scr.shape, F32)

    pl.when(j == 0)(lambda: _init(1))
    pl.when(j == nc)(lambda: _init(0))

    def _prep(graw_ref, sg, slot):
        row = lax.broadcasted_iota(jnp.int32, (CH, CH), 0)
        col = lax.broadcasted_iota(jnp.int32, (CH, CH), 1)
        tri = jnp.where((col - row) * sg <= 0, 1.0, 0.0).astype(BF16)
        f_mat = (jnp.zeros((CH, LANES), jnp.int32) + sg) > 0
        g = graw_ref[...] + gb_ref[...]
        li = jnp.where(f_mat, g, pltpu.roll(g, LANES - 16, axis=1))
        lfr = jnp.where(f_mat, pltpu.roll(g, LANES - 8, axis=1), pltpu.roll(g, LANES - 24, axis=1))
        lf = jnp.minimum(lfr, 0.0) - jnp.log(1.0 + jnp.exp(-jnp.abs(lfr)))
        bc = _tri_dot(tri, lf)
        rr = li - bc
        cmx = rr
        for dlt in (1, 2, 4, 8, 16, 32, 64):
            before = jnp.where(row >= dlt, pltpu.roll(cmx, dlt, axis=0), NEG)
            after = jnp.where(row < CH - dlt, pltpu.roll(cmx, CH - dlt, axis=0), NEG)
            cmx = jnp.maximum(cmx, jnp.where(f_mat, before, after))
        r_scr[slot] = rr * LOG2E
        rows_scr[slot, 0] = rr.T[0:H_A, :]
        rows_scr[slot, 1] = bc.T[0:H_A, :]
        rows_scr[slot, 2] = cmx.T[0:H_A, :]

    pl.when(j == 0)(lambda: _prep(g_ref, sgn, 0))

    slot = j % 2
    row = lax.broadcasted_iota(jnp.int32, (CH, CH), 0)
    col = lax.broadcasted_iota(jnp.int32, (CH, CH), 1)
    causal_t = (row - col) * sgn <= 0
    fwd8 = (jnp.zeros((H_A, LANES), jnp.int32) + sgn) > 0
    r = r_scr[slot]
    r_t = rows_scr[slot, 0]
    b_t = rows_scr[slot, 1]
    cm = rows_scr[slot, 2]
    _prep(gn_ref, jnp.where(j + 1 >= nc, 1, -1), (j + 1) % 2)

    m_prev = m_scr[...]
    g_t = jnp.maximum(m_prev, cm)
    mt_t = b_t + g_t
    wst_t = jnp.exp(m_prev - g_t)
    emt_t = jnp.exp(-mt_t)
    fwd_c = fwd8[:, 0:1]
    m_new = jnp.where(fwd_c, mt_t[:, LANES - 1:LANES], mt_t[:, 0:1])
    b_last = jnp.where(fwd_c, b_t[:, LANES - 1:LANES], b_t[:, 0:1])
    ws_t = jnp.exp(r_t + (b_last - m_new))
    decay = jnp.exp(b_last + m_prev[:, 0:1] - m_new)
    m_scr[...] = jnp.broadcast_to(m_new, (H_A, LANES))

    q_t = q_ref[...].astype(F32).T.astype(BF16)
    v_t = v_ref[...].astype(F32).T
    ones = jnp.ones((HD, CH), F32)

    g2_t = g_t * LOG2E
    heads = range(H_A)
    sls = [slice(h * HD, (h + 1) * HD) for h in heads]
    khs = [k_ref[:, sl] for sl in sls]
    qhs = [q_t[sl, :] for sl in sls]
    vexts = [jnp.concatenate([v_t[sl, :], ones], axis=0) for sl in sls]
    cexts = [c_scr[h] for h in heads]
    scs = [_dot(khs[h], qhs[h]) for h in heads]
    w_ts = [jnp.exp2(jnp.where(causal_t, r[:, h:h + 1] - g2_t[h:h + 1, :], NEG)) for h in heads]
    sc_ts = [(scs[h] * 0.125 * w_ts[h]).astype(BF16) for h in heads]
    cqs = [_dot(cexts[h].astype(BF16), qhs[h]) for h in heads]
    nds = [_dot(vexts[h].astype(BF16), sc_ts[h]) + (cqs[h] * 0.125) * wst_t[h:h + 1, :] for h in heads]
    hts = [nds[h][0:HD, :] / jnp.maximum(jnp.abs(nds[h][HD:HD + 1, :]), emt_t[h:h + 1, :]) for h in heads]
    for h in heads:
        c_scr[h] = decay[h:h + 1, :] * cexts[h] + _dot((vexts[h] * ws_t[h:h + 1, :]).astype(BF16), khs[h])
    hcat_t = jnp.concatenate(hts, axis=0)

    @pl.when(j < nc)
    def _():
        hb_scr[ci] = hcat_t

    @pl.when(j >= nc)
    def _():
        ht = hcat_t + hb_scr[ci]
        normed = []
        for h in range(H_A):
            hh = ht[h * HD:(h + 1) * HD, :]
            dlt = hh - jnp.mean(hh, axis=0, keepdims=True)
            normed.append(dlt * lax.rsqrt(jnp.mean(dlt * dlt, axis=0, keepdims=True) + EPS))
        hn = jnp.concatenate(normed, axis=0).T * ng_ref[...] * _sigmoid(om_ref[...].astype(F32))
        hn_ref[...] = hn.astype(hn_ref.dtype)

    if out_state:
        @pl.when(j == nc - 1)
        def _():
            cst_ref[1] = c_scr[...]
            mst_ref[1] = m_scr[...]

        @pl.when(j == 2 * nc - 1)
        def _():
            cst_ref[0] = c_scr[...]
            mst_ref[0] = m_scr[...]


def _mlstm(pm, pa, gbias, norm_g, bsz, s_len, init=None, out_state=False):
    nc = s_len // CH
    t = bsz * s_len

    def chunk(b, j):
        return b * nc + jnp.where(j >= nc, j - nc, nc - 1 - j)

    def chunk_next(b, j):
        return chunk(b, jnp.minimum(j + 1, 2 * nc - 1))

    def chunk_fwd(b, j):
        return b * nc + jnp.maximum(j - nc, 0)

    in_specs = [pl.BlockSpec((CH, W_A), lambda b, j: (chunk(b, j), 0)),
                pl.BlockSpec((CH, W_A), lambda b, j: (chunk(b, j), 1)),
                pl.BlockSpec((CH, W_A), lambda b, j: (chunk(b, j), 2)),
                pl.BlockSpec((CH, W_A), lambda b, j: (chunk_fwd(b, j), 3)),
                pl.BlockSpec((CH, LANES), lambda b, j: (chunk(b, j), 0)),
                pl.BlockSpec((CH, LANES), lambda b, j: (chunk_next(b, j), 0)),
                pl.BlockSpec((1, LANES), lambda b, j: (0, 0)),
                pl.BlockSpec((1, W_A), lambda b, j: (0, 0))]
    args = [pm, pm, pm, pm, pa, pa, gbias, norm_g]
    if init is not None:
        in_specs += [pl.BlockSpec((None, 2, H_A, LANES, HD), lambda b, j: (b, 0, 0, 0, 0)),
                     pl.BlockSpec((None, 2, H_A, LANES), lambda b, j: (b, 0, 0, 0))]
        args += list(init)
    out_specs = [pl.BlockSpec((CH, W_A), lambda b, j: (chunk_fwd(b, j), 0))]
    out_shape = [jax.ShapeDtypeStruct((t, W_A), BF16)]
    if out_state:
        out_specs += [pl.BlockSpec((None, 2, H_A, LANES, HD), lambda b, j: (b, 0, 0, 0, 0)),
                      pl.BlockSpec((None, 2, H_A, LANES), lambda b, j: (b, 0, 0, 0))]
        out_shape += [jax.ShapeDtypeStruct((bsz, 2, H_A, LANES, HD), F32),
                      jax.ShapeDtypeStruct((bsz, 2, H_A, LANES), F32)]
    return pl.pallas_call(
        functools.partial(_mlstm_kernel, nc=nc, has_init=init is not None, out_state=out_state),
        grid=(bsz, 2 * nc),
        in_specs=in_specs,
        out_specs=out_specs,
        out_shape=out_shape,
        scratch_shapes=[pltpu.VMEM((H_A, LANES, HD), F32), pltpu.VMEM((H_A, LANES), F32),
                        pltpu.VMEM((nc, W_A, CH), F32),
                        pltpu.VMEM((2, CH, LANES), F32), pltpu.VMEM((2, 3, H_A, LANES), F32)],
        compiler_params=_cp("parallel", "arbitrary"),
        name="mlstm",
    )(*args)


def _group_rms(x, avg, g):
    s1, s2 = _split2(x * x)
    ms = _dot(s1, avg) + _dot(s2, avg)
    return x * lax.rsqrt(ms + EPS) * g


def _rope(x, cos, sin):
    lane = lax.broadcasted_iota(jnp.int32, x.shape, 1)
    swapped = jnp.where((lane % 32) < 16, pltpu.roll(x, LANES - 16, axis=1), pltpu.roll(x, 16, axis=1))
    return x * cos + swapped * sin


def _qkprep_kernel(*refs, rope, emit_kn):
    q_ref, k_ref, v_ref, qg_ref, kg_ref, avg_ref = refs[:6]
    pos = 6
    if rope:
        cos_ref, sin_ref = refs[pos:pos + 2]
        pos += 2
    qn_ref, kh_ref, vh_ref = refs[pos:pos + 3]
    pos += 3
    avg = avg_ref[...]
    qs = []
    for c in range(W_A // LANES):
        sl = slice(c * LANES, (c + 1) * LANES)
        xn = _group_rms(q_ref[:, sl].astype(F32), avg, qg_ref[:, sl])
        if rope:
            xn = _rope(xn, cos_ref[...], sin_ref[...])
        qs.append(xn * 0.125)
    qn_ref[...] = jnp.concatenate(qs, axis=1).astype(qn_ref.dtype)
    kn = _group_rms(k_ref[...].astype(F32), avg, kg_ref[...])
    if emit_kn:
        refs[pos][...] = kn
    if rope:
        kn = _rope(kn, cos_ref[...], sin_ref[...])
    v = v_ref[...].astype(F32)
    ones = jnp.ones((v.shape[0], HD), F32)
    for gi in range(KV_B):
        sl = slice(gi * HD, (gi + 1) * HD)
        kh_ref[gi] = kn[:, sl].astype(kh_ref.dtype)
        vh_ref[gi] = jnp.concatenate([v[:, sl], ones], axis=1).astype(vh_ref.dtype)


def _qkprep(pm, q_g, k_g, avg128, s_len, tables=None, emit_kn=False):
    t = pm.shape[0]
    tm = 256
    nt = s_len // tm
    qoff, koff, voff = 2048 // W_A, 2560 // LANES, 2688 // LANES
    in_specs = [pl.BlockSpec((tm, W_A), lambda i: (i, qoff)),
                pl.BlockSpec((tm, LANES), lambda i: (i, koff)),
                pl.BlockSpec((tm, LANES), lambda i: (i, voff)),
                pl.BlockSpec((1, W_A), lambda i: (0, 0)),
                pl.BlockSpec((1, LANES), lambda i: (0, 0)),
                pl.BlockSpec((LANES, LANES), lambda i: (0, 0))]
    args = [pm, pm, pm, q_g, k_g, avg128]
    if tables is not None:
        in_specs += [pl.BlockSpec((tm, LANES), lambda i: (i % nt, 0))] * 2
        args += list(tables)
    out_specs = [pl.BlockSpec((tm, W_A), lambda i: (i, 0)),
                 pl.BlockSpec((KV_B, tm, HD), lambda i: (0, i, 0)),
                 pl.BlockSpec((KV_B, tm, LANES), lambda i: (0, i, 0))]
    out_shape = [jax.ShapeDtypeStruct((t, W_A), BF16),
                 jax.ShapeDtypeStruct((KV_B, t, HD), BF16),
                 jax.ShapeDtypeStruct((KV_B, t, LANES), BF16)]
    if emit_kn:
        out_specs.append(pl.BlockSpec((tm, LANES), lambda i: (i, 0)))
        out_shape.append(jax.ShapeDtypeStruct((t, LANES), F32))
    return pl.pallas_call(
        functools.partial(_qkprep_kernel, rope=tables is not None, emit_kn=emit_kn),
        grid=(t // tm,),
        in_specs=in_specs,
        out_specs=out_specs,
        out_shape=out_shape,
        compiler_params=_cp("parallel"),
        name="qkprep",
    )(*args)


def _attn_kernel(*refs, has_ctx):
    if has_ctx:
        q_ref, k_ref, v_ref, kc_ref, vc_ref, o_ref = refs
    else:
        q_ref, k_ref, v_ref, o_ref = refs
    k = k_ref[...]
    v = v_ref[...]
    hq = range(4)
    qhs = [q_ref[:, h * HD:(h + 1) * HD] for h in hq]
    ss = [_dot_nt(qh, k) for qh in qhs]
    ms = [jnp.max(s, axis=1, keepdims=True) for s in ss]
    if has_ctx:
        scs = [_dot_nt(qh, kc_ref[...]) for qh in qhs]
        ms = [jnp.maximum(m, jnp.max(sc, axis=1, keepdims=True)) for m, sc in zip(ms, scs)]
    oes = [_dot(jnp.exp(s - m).astype(BF16), v) for s, m in zip(ss, ms)]
    if has_ctx:
        oes = [oe + _dot(jnp.exp(sc - m).astype(BF16), vc_ref[...]) for oe, sc, m in zip(oes, scs, ms)]
    outs = [oe[:, :HD] / oe[:, HD:HD + 1] for oe in oes]
    o_ref[...] = jnp.concatenate(outs, axis=1).astype(o_ref.dtype)


def _attn(qn, kh, vh, bsz, s_len, ctx=None):
    t = qn.shape[0]
    tq = min(512, s_len)
    nq = s_len // tq
    in_specs = [pl.BlockSpec((tq, 256), lambda b, g, i: (b * nq + i, g)),
                pl.BlockSpec((None, s_len, HD), lambda b, g, i: (g, b, 0)),
                pl.BlockSpec((None, s_len, LANES), lambda b, g, i: (g, b, 0))]
    args = [qn, kh, vh]
    if ctx is not None:
        past = ctx[0].shape[1] // bsz
        in_specs += [pl.BlockSpec((None, past, HD), lambda b, g, i: (g, b, 0)),
                     pl.BlockSpec((None, past, LANES), lambda b, g, i: (g, b, 0))]
        args += list(ctx)
    return pl.pallas_call(
        functools.partial(_attn_kernel, has_ctx=ctx is not None),
        grid=(bsz, KV_B, nq),
        in_specs=in_specs,
        out_specs=pl.BlockSpec((tq, 256), lambda b, g, i: (b * nq + i, g)),
        out_shape=jax.ShapeDtypeStruct((t, W_A), BF16),
        compiler_params=_cp("parallel", "parallel", "parallel"),
        name="attn",
    )(*args)


def _out_kernel(*refs, n_in, router):
    a_refs = refs[:n_in]
    w_refs = refs[n_in:2 * n_in]
    x_ref, g1_ref, sh2_ref, sc2_ref, lng_ref, lnb_ref = refs[2 * n_in:2 * n_in + 6]
    pos = 2 * n_in + 6
    if router:
        rw_ref, rb_ref = refs[pos:pos + 2]
        pos += 2
    x1_ref, u2_ref = refs[pos:pos + 2]
    y = _dot(a_refs[0][...], w_refs[0][...])
    for a_ref, w_ref in zip(a_refs[1:], w_refs[1:]):
        y = y + _dot(a_ref[...], w_ref[...])
    x1 = _layer_norm(ALPHA * x_ref[...] + g1_ref[...] * y, lng_ref[...], lnb_ref[...])
    x1_ref[...] = x1
    u2 = x1 * (1.0 + sc2_ref[...]) + sh2_ref[...]
    uh, ul = _split2(u2)
    u2_ref[...] = uh
    if router:
        gate_ref = refs[pos + 2]
        logits = _dot_hilo(uh, ul, rw_ref) + rb_ref[...]
        lane = lax.broadcasted_iota(jnp.int32, logits.shape, 1)
        m1 = jnp.max(logits, axis=1, keepdims=True)
        i1 = jnp.min(jnp.where(logits == m1, lane, LANES), axis=1, keepdims=True)
        rest = jnp.where(lane == i1, 2.0 * NEG, logits)
        m2 = jnp.max(rest, axis=1, keepdims=True)
        i2 = jnp.min(jnp.where(rest == m2, lane, LANES), axis=1, keepdims=True)
        e = jnp.exp(m2 - m1)
        p1 = 1.0 / (1.0 + e)
        gate_ref[...] = jnp.where(lane == i1, p1, jnp.where(lane == i2, e * p1, 0.0))


def _out_proj(acts, ws, x, mods3, row_fn, ln_g, ln_b, router=None):
    t = x.shape[0]
    tm = 512
    n_in = len(acts)
    in_specs = [pl.BlockSpec((tm, a.shape[1]), lambda i: (i, 0)) for a in acts]
    in_specs += [pl.BlockSpec(w.shape, lambda i: (0, 0)) for w in ws]
    in_specs += [pl.BlockSpec((tm, D), lambda i: (i, 0))]
    in_specs += [pl.BlockSpec((None, 1, D), functools.partial(lambda i, k: (row_fn(i, tm), 0, k), k=k))
                 for k in (2, 3, 4)]
    in_specs += [pl.BlockSpec((1, D), lambda i: (0, 0))] * 2
    args = list(acts) + list(ws) + [x, mods3, mods3, mods3, ln_g.reshape(1, D), ln_b.reshape(1, D)]
    out_specs = [pl.BlockSpec((tm, D), lambda i: (i, 0)), pl.BlockSpec((tm, D), lambda i: (i, 0))]
    out_shape = [jax.ShapeDtypeStruct((t, D), F32), jax.ShapeDtypeStruct((t, D), BF16)]
    if router is not None:
        in_specs += [pl.BlockSpec((D, 2 * LANES), lambda i: (0, 0)), pl.BlockSpec((1, LANES), lambda i: (0, 0))]
        args += list(router)
        out_specs.append(pl.BlockSpec((tm, LANES), lambda i: (i, 0)))
        out_shape.append(jax.ShapeDtypeStruct((t, LANES), F32))
    return pl.pallas_call(
        functools.partial(_out_kernel, n_in=n_in, router=router is not None),
        grid=(t // tm,),
        in_specs=in_specs,
        out_specs=out_specs,
        out_shape=out_shape,
        compiler_params=_cp("parallel"),
        name="out_proj",
    )(*args)


def _ffn_kernel(u_ref, w1_ref, w3_ref, w2_ref, x1_ref, g2_ref, lng_ref, lnb_ref, o_ref, *, tf):
    u = u_ref[...]
    acc = None
    for c in range(w1_ref.shape[1] // tf):
        cs = slice(c * tf, (c + 1) * tf)
        h = _silu(_dot(u, w1_ref[:, cs])) * _dot(u, w3_ref[:, cs])
        part = _dot(h.astype(BF16), w2_ref[cs, :])
        acc = part if acc is None else acc + part
    o_ref[...] = _layer_norm(ALPHA * x1_ref[...] + g2_ref[...] * acc, lng_ref[...], lnb_ref[...])


def _ffn(u2, w1, w3, w2, x1, mods3, row_fn, ln_g, ln_b, tf):
    t = u2.shape[0]
    tm = 512
    f = w1.shape[1]
    once = pl.Buffered(1)
    return pl.pallas_call(
        functools.partial(_ffn_kernel, tf=tf),
        grid=(t // tm,),
        in_specs=[pl.BlockSpec((tm, D), lambda i: (i, 0)),
                  pl.BlockSpec((D, f), lambda i: (0, 0), pipeline_mode=once),
                  pl.BlockSpec((D, f), lambda i: (0, 0), pipeline_mode=once),
                  pl.BlockSpec((f, D), lambda i: (0, 0), pipeline_mode=once),
                  pl.BlockSpec((tm, D), lambda i: (i, 0)),
                  pl.BlockSpec((None, 1, D), lambda i: (row_fn(i, tm), 0, 5)),
                  pl.BlockSpec((1, D), lambda i: (0, 0)),
                  pl.BlockSpec((1, D), lambda i: (0, 0))],
        out_specs=pl.BlockSpec((tm, D), lambda i: (i, 0)),
        out_shape=jax.ShapeDtypeStruct((t, D), F32),
        compiler_params=_cp("parallel"),
        name="ffn",
    )(u2, w1, w3, w2, x1, mods3, ln_g.reshape(1, D), ln_b.reshape(1, D))


MOE_CAP = 256


def _moe_kernel(u_ref, w1_ref, w3_ref, w2_ref, gate_ref, tril_ref, x1_ref, g2_ref, lng_ref, lnb_ref, o_ref,
                acc_scr, rank_scr, rank_t_scr, gate_t_scr):
    e = pl.program_id(1)
    tm = u_ref.shape[0]

    @pl.when(e == 0)
    def _():
        acc_scr[...] = jnp.zeros(acc_scr.shape, F32)
        gate = gate_ref[...]
        routed = gate > 0.0
        rank = jnp.where(routed, _dot(tril_ref[...], jnp.where(routed, 1.0, 0.0).astype(BF16)), -1.0)
        rank_scr[...] = rank
        rank_t_scr[...] = rank.T
        gate_t_scr[...] = gate.T

    lane = lax.broadcasted_iota(jnp.int32, (tm, LANES), 1)
    rank_col = jnp.sum(jnp.where(lane == e, rank_scr[...], 0.0), axis=1, keepdims=True)
    rank_row = rank_t_scr[pl.ds(e, 1), :]
    gate_row = gate_t_scr[pl.ds(e, 1), :]
    count = jnp.sum(jnp.where(rank_row >= 0.0, 1, 0))
    u = u_ref[...]

    def one_pass(first_slot, cap):
        off = first_slot.astype(F32)
        slot_c = lax.broadcasted_iota(jnp.int32, (cap, tm), 0).astype(F32) + off
        sel = jnp.where(rank_row == slot_c, 1.0, 0.0)
        weight = jnp.sum(sel * gate_row, axis=1, keepdims=True)
        xe = _dot(sel.astype(BF16), u).astype(BF16)
        h = _silu(_dot(xe, w1_ref[...])) * _dot(xe, w3_ref[...])
        y = _dot(h.astype(BF16), w2_ref[...]) * weight
        slot_r = lax.broadcasted_iota(jnp.int32, (tm, cap), 1).astype(F32) + off
        sel_t = jnp.where(rank_col == slot_r, 1.0, 0.0).astype(BF16)
        acc_scr[...] += _dot(sel_t, y.astype(BF16))

    n_full = count // MOE_CAP
    left = count - n_full * MOE_CAP
    n_full = n_full + jnp.where(left > MOE_CAP // 2, 1, 0)

    def full_pass(blk, carry):
        one_pass(blk * MOE_CAP, MOE_CAP)
        return carry

    lax.fori_loop(0, n_full, full_pass, 0)

    @pl.when(jnp.logical_and(left > 0, left <= MOE_CAP // 2))
    def _():
        one_pass(n_full * MOE_CAP, MOE_CAP // 2)

    @pl.when(e == pl.num_programs(1) - 1)
    def _():
        o_ref[...] = _layer_norm(ALPHA * x1_ref[...] + g2_ref[...] * acc_scr[...], lng_ref[...], lnb_ref[...])


def _moe(u2, w1, w3, w2, gate, x1, mods3, row_fn, ln_g, ln_b, tm):
    t = u2.shape[0]
    n_e, _, f = w1.shape
    tril = jnp.tril(jnp.ones((tm, tm), F32), -1).astype(BF16)
    once = pl.Buffered(1)
    in_specs = [pl.BlockSpec((tm, D), lambda i, e: (i, 0)),
                pl.BlockSpec((None, D, f), lambda i, e: (e, 0, 0)),
                pl.BlockSpec((None, D, f), lambda i, e: (e, 0, 0)),
                pl.BlockSpec((None, f, D), lambda i, e: (e, 0, 0)),
                pl.BlockSpec((tm, LANES), lambda i, e: (i, 0)),
                pl.BlockSpec((tm, tm), lambda i, e: (0, 0), pipeline_mode=once),
                pl.BlockSpec((tm, D), lambda i, e: (i, 0), pipeline_mode=once),
                pl.BlockSpec((None, 1, D), lambda i, e: (row_fn(i, tm), 0, 5)),
                pl.BlockSpec((1, D), lambda i, e: (0, 0)),
                pl.BlockSpec((1, D), lambda i, e: (0, 0))]
    return pl.pallas_call(
        _moe_kernel,
        grid=(t // tm, n_e),
        in_specs=in_specs,
        out_specs=pl.BlockSpec((tm, D), lambda i, e: (i, 0)),
        out_shape=jax.ShapeDtypeStruct((t, D), F32),
        scratch_shapes=[pltpu.VMEM((tm, D), F32), pltpu.VMEM((tm, LANES), F32),
                        pltpu.VMEM((LANES, tm), F32), pltpu.VMEM((LANES, tm), F32)],
        compiler_params=_cp("parallel", "arbitrary"),
        name="moe",
    )(u2, w1, w3, w2, gate, tril, x1, mods3, ln_g.reshape(1, D), ln_b.reshape(1, D))


def _conv_kernel(x_ref, w_ref, b_ref, o_ref, *, s_len, transpose):
    rows = x_ref.shape[0]
    bps = s_len // CH
    r = lax.broadcasted_iota(jnp.int32, (CH, 2 * CH), 0)
    c = lax.broadcasted_iota(jnp.int32, (CH, 2 * CH), 1)
    prev_win = jnp.where(c == r + CH - 1, 1.0, 0.0).astype(BF16)
    next_win = jnp.where(c == r + 1, 1.0, 0.0).astype(BF16)
    r1 = lax.broadcasted_iota(jnp.int32, (CH, CH), 0)
    c1 = lax.broadcasted_iota(jnp.int32, (CH, CH), 1)
    prev_in = jnp.where(c1 == r1 - 1, 1.0, 0.0).astype(BF16)
    next_in = jnp.where(c1 == r1 + 1, 1.0, 0.0).astype(BF16)
    w0, w1, w2, bias = w_ref[0:1, :], w_ref[1:2, :], w_ref[2:3, :], b_ref[...]
    seq_blocks = []
    for bi in range(rows // CH):
        lo, hi = bi * CH, (bi + 1) * CH
        cur = x_ref[lo:hi, :]
        if bi % bps == 0:
            prev = _dot(prev_in, cur)
        else:
            prev = _dot(prev_win, x_ref[lo - CH:hi, :])
        if bi % bps == bps - 1:
            nxt = _dot(next_in, cur)
        else:
            nxt = _dot(next_win, x_ref[lo:hi + CH, :])
        y = _silu(w0 * prev + w1 * cur.astype(F32) + w2 * nxt + bias)
        if transpose:
            seq_blocks.append(y.T)
            if bi % bps == bps - 1:
                o_ref[bi // bps] = jnp.concatenate(seq_blocks, axis=1).astype(o_ref.dtype)
                seq_blocks = []
        else:
            o_ref[lo:hi, :] = y.astype(o_ref.dtype)


def _conv(pm, conv_w, conv_b, bsz, s_len, col_off, transpose):
    t = pm.shape[0]
    tc = 512
    n_ch = conv_w.shape[1]
    off = col_off // tc
    n_seq = max(1, min(bsz, 2048 // s_len))
    rows = n_seq * s_len
    if transpose:
        out_spec = pl.BlockSpec((n_seq, tc, s_len), lambda b, c: (b, c, 0))
        out_shape = jax.ShapeDtypeStruct((bsz, n_ch, s_len), BF16)
    else:
        out_spec = pl.BlockSpec((rows, tc), lambda b, c: (b, c))
        out_shape = jax.ShapeDtypeStruct((t, n_ch), BF16)
    return pl.pallas_call(
        functools.partial(_conv_kernel, s_len=s_len, transpose=transpose),
        grid=(bsz // n_seq, n_ch // tc),
        in_specs=[pl.BlockSpec((rows, tc), lambda b, c: (b, off + c)),
                  pl.BlockSpec((3, tc), lambda b, c: (0, c)),
                  pl.BlockSpec((1, tc), lambda b, c: (0, c))],
        out_specs=out_spec,
        out_shape=out_shape,
        compiler_params=_cp("parallel", "parallel"),
        name="ssd_conv",
    )(pm, conv_w, conv_b.reshape(1, n_ch))


def _ssd_kernel(*refs, nc, has_init, out_state):
    xt_ref, b_ref, c_ref, dt_ref, dtn_ref, z_ref, dtb_ref, alog_ref, dsk_ref, ng_ref = refs[:10]
    pos = 10
    if has_init:
        s0_ref = refs[pos]
        pos += 1
    yn_ref = refs[pos]
    pos += 1
    if out_state:
        sst_ref = refs[pos]
        pos += 1
    s_scr, yb_scr, yt_scr, xw_scr, cum_scr, rows_scr = refs[pos:pos + 6]

    j = pl.program_id(1)
    is_fwd = j >= nc
    ci = jnp.where(is_fwd, j - nc, nc - 1 - j)
    sgn = jnp.where(is_fwd, 1, -1)

    def _init(d):
        if has_init:
            s_scr[...] = s0_ref[d]
        else:
            s_scr[...] = jnp.zeros(s_scr.shape, F32)

    pl.when(j == 0)(lambda: _init(1))
    pl.when(j == nc)(lambda: _init(0))

    def _prep(dtraw_ref, sg, slot):
        row = lax.broadcasted_iota(jnp.int32, (CH, CH), 0)
        col = lax.broadcasted_iota(jnp.int32, (CH, CH), 1)
        tri = jnp.where((col - row) * sg <= 0, 1.0, 0.0).astype(BF16)
        f_row = (jnp.zeros((1, LANES), jnp.int32) + sg) > 0
        f_mat = (jnp.zeros((CH, LANES), jnp.int32) + sg) > 0
        f_hc = (jnp.zeros((H_C, LANES), jnp.int32) + sg) > 0
        raw = dtraw_ref[...] + dtb_ref[...]
        raw = jnp.where(f_mat, raw, pltpu.roll(raw, LANES - H_C, axis=1))
        alog = alog_ref[...]
        alog = jnp.where(f_row, alog, pltpu.roll(alog, LANES - H_C, axis=1))
        dtv = _softplus(raw)
        cumv = _tri_dot(tri, dtv * (-jnp.exp(alog)))
        cumv_t = cumv.T[0:H_C, :]
        dtv_t = dtv.T[0:H_C, :]
        last = jnp.where(f_hc, jnp.broadcast_to(cumv_t[:, CH - 1:CH], (H_C, LANES)),
                         jnp.broadcast_to(cumv_t[:, 0:1], (H_C, LANES)))
        cum_scr[slot] = cumv * LOG2E
        rows_scr[slot, 0] = cumv_t * LOG2E
        rows_scr[slot, 1] = dtv_t
        rows_scr[slot, 2] = jnp.exp(cumv_t)
        rows_scr[slot, 3] = jnp.exp(last - cumv_t) * dtv_t
        rows_scr[slot, 4] = jnp.exp(last)

    pl.when(j == 0)(lambda: _prep(dt_ref, sgn, 0))

    slot = j % 2
    row = lax.broadcasted_iota(jnp.int32, (CH, CH), 0)
    col = lax.broadcasted_iota(jnp.int32, (CH, CH), 1)
    causal_t = (row - col) * sgn <= 0
    cum = cum_scr[slot]
    cum_t = rows_scr[slot, 0]
    dt_t = rows_scr[slot, 1]
    ecum_t = rows_scr[slot, 2]
    ws_t = rows_scr[slot, 3]
    dec = rows_scr[slot, 4]
    _prep(dtn_ref, jnp.where(j + 1 >= nc, 1, -1), (j + 1) % 2)

    for gi in range(NG_C):
        gsl = slice(gi * D_STATE, (gi + 1) * D_STATE)
        bg = b_ref[:, gsl]
        cg = c_ref[:, gsl]
        cb_t = _dot_nt(bg, cg)
        ys_t = _dot_nt(s_scr[gi].astype(BF16), cg)
        for half in range(2):
            hr = range(half * 4, half * 4 + 4)
            rs = {rr: gi * HPG_C + rr for rr in hr}
            w_ts = {rr: (cb_t * jnp.exp2(jnp.where(causal_t, cum_t[rs[rr]:rs[rr] + 1, :] - cum[:, rs[rr]:rs[rr] + 1],
                                                   NEG))).astype(BF16)
                    for rr in hr}
            xr_ts = {rr: xt_ref[rs[rr] * HD:(rs[rr] + 1) * HD, :].astype(F32) for rr in hr}
            xd_ts = {rr: (xr_ts[rr] * dt_t[rs[rr]:rs[rr] + 1, :]).astype(BF16) for rr in hr}
            y_ts = {rr: _dot(xd_ts[rr], w_ts[rr]) + ys_t[rr * HD:(rr + 1) * HD, :] * ecum_t[rs[rr]:rs[rr] + 1, :]
                    for rr in hr}
            for rr in hr:
                yt_scr[rs[rr] * HD:(rs[rr] + 1) * HD, :] = y_ts[rr]
                xw_scr[rr * HD:(rr + 1) * HD, :] = (xr_ts[rr] * ws_t[rs[rr]:rs[rr] + 1, :]).astype(BF16)
        upd = _dot(xw_scr[...], bg)
        for rr in range(HPG_C):
            r = gi * HPG_C + rr
            hsl = slice(rr * HD, (rr + 1) * HD)
            s_scr[gi, hsl, :] = s_scr[gi, hsl, :] * dec[r:r + 1, :] + upd[hsl, :]

    @pl.when(j < nc)
    def _():
        yb_scr[ci] = yt_scr[...]

    @pl.when(j >= nc)
    def _():
        yt = yt_scr[...] + yb_scr[ci] + dsk_ref[...] * xt_ref[...].astype(F32)
        y = yt.T * _silu(z_ref[...].astype(F32))
        outs = []
        gw = D_INNER // NG_C
        for gi in range(NG_C):
            yg = y[:, gi * gw:(gi + 1) * gw]
            outs.append(yg * lax.rsqrt(jnp.mean(yg * yg, axis=1, keepdims=True) + EPS))
        yn_ref[...] = (jnp.concatenate(outs, axis=1) * ng_ref[...]).astype(yn_ref.dtype)

    if out_state:
        @pl.when(j == nc - 1)
        def _():
            sst_ref[1] = s_scr[...]

        @pl.when(j == 2 * nc - 1)
        def _():
            sst_ref[0] = s_scr[...]


def _ssd(xt, bc, pa, pm, dtb, alog, dsk, norm_g, bsz, s_len, init=None, out_state=False):
    nc = s_len // CH
    t = bsz * s_len
    gw = NG_C * D_STATE

    def chunk_i(j):
        return jnp.where(j >= nc, j - nc, nc - 1 - j)

    def chunk(b, j):
        return b * nc + chunk_i(j)

    def chunk_next(b, j):
        return chunk(b, jnp.minimum(j + 1, 2 * nc - 1))

    def chunk_fwd(b, j):
        return b * nc + jnp.maximum(j - nc, 0)

    in_specs = [pl.BlockSpec((None, D_INNER, CH), lambda b, j: (b, 0, chunk_i(j))),
                pl.BlockSpec((CH, gw), lambda b, j: (chunk(b, j), 0)),
                pl.BlockSpec((CH, gw), lambda b, j: (chunk(b, j), 1)),
                pl.BlockSpec((CH, LANES), lambda b, j: (chunk(b, j), 0)),
                pl.BlockSpec((CH, LANES), lambda b, j: (chunk_next(b, j), 0)),
                pl.BlockSpec((CH, D_INNER), lambda b, j: (chunk_fwd(b, j), 0)),
                pl.BlockSpec((1, LANES), lambda b, j: (0, 0)),
                pl.BlockSpec((1, LANES), lambda b, j: (0, 0)),
                pl.BlockSpec((D_INNER, LANES), lambda b, j: (0, 0)),
                pl.BlockSpec((1, D_INNER), lambda b, j: (0, 0))]
    args = [xt, bc, bc, pa, pa, pm, dtb, alog, dsk, norm_g]
    state_spec = pl.BlockSpec((None, 2, NG_C, HPG_C * HD, D_STATE), lambda b, j: (b, 0, 0, 0, 0))
    if init is not None:
        in_specs.append(state_spec)
        args.append(init)
    out_specs = [pl.BlockSpec((CH, D_INNER), lambda b, j: (chunk_fwd(b, j), 0))]
    out_shape = [jax.ShapeDtypeStruct((t, D_INNER), BF16)]
    if out_state:
        out_specs.append(state_spec)
        out_shape.append(jax.ShapeDtypeStruct((bsz, 2, NG_C, HPG_C * HD, D_STATE), F32))
    return pl.pallas_call(
        functools.partial(_ssd_kernel, nc=nc, has_init=init is not None, out_state=out_state),
        grid=(bsz, 2 * nc),
        in_specs=in_specs,
        out_specs=out_specs,
        out_shape=out_shape,
        scratch_shapes=[pltpu.VMEM((NG_C, HPG_C * HD, D_STATE), F32),
                        pltpu.VMEM((nc, D_INNER, CH), F32),
                        pltpu.VMEM((D_INNER, CH), F32),
                        pltpu.VMEM((HPG_C * HD, CH), BF16),
                        pltpu.VMEM((2, CH, LANES), F32), pltpu.VMEM((2, 5, H_C, LANES), F32)],
        compiler_params=_cp("parallel", "arbitrary"),
        name="ssd_scan",
    )(*args)


def _rope_tables(s_len):
    nf = HD // 4
    inv = ROPE_THETA ** (-jnp.arange(nf, dtype=F32) / nf)
    tpos = jnp.arange(s_len)
    ang_r = (tpos // GRID_W).astype(F32)[:, None] * inv
    ang_c = (tpos % GRID_W).astype(F32)[:, None] * inv
    cos = jnp.concatenate([jnp.cos(ang_r)] * 2 + [jnp.cos(ang_c)] * 2, axis=1)
    sin = jnp.concatenate([-jnp.sin(ang_r), jnp.sin(ang_r), -jnp.sin(ang_c), jnp.sin(ang_c)], axis=1)
    return jnp.tile(cos, (1, 2)), jnp.tile(sin, (1, 2))


def _block_avg(n):
    idx = np.arange(n) // HD
    return jnp.asarray((idx[:, None] == idx[None, :]).astype(np.float32) / HD, dtype=BF16)


def _pad_cols(w, n):
    return jnp.pad(w, ((0, 0), (0, n - w.shape[1])))


def _row_prompt(i, tm):
    return 0


def _make_row_sample(s_len):
    def row(i, tm):
        return 1 + (i * tm) // s_len
    return row


def kernel(x_prompt, x_sample, cache_k_l0, cache_v_l0, state_mlstm_c_l0, state_mlstm_n_l0, state_mlstm_m_l0, state_ssd_l1, c, c_ctx, ada_w_l0, ada_b_l0, mix_w_in_l0, mlstm_gate_b_l0, mlstm_norm_g_l0, q_norm_g_l0, k_norm_g_l0, mix_w_out_l0, ln1_g_l0, ln1_b_l0, ln2_g_l0, ln2_b_l0, ffn_w1_l0, ffn_w3_l0, ffn_w2_l0, ada_w_l1, ada_b_l1, ssd_w_in_l1, ssd_conv_w_l1, ssd_conv_b_l1, ssd_dt_bias_l1, ssd_a_log_l1, ssd_d_l1, ssd_norm_g_l1, ssd_w_out_l1, ln1_g_l1, ln1_b_l1, ln2_g_l1, ln2_b_l1, router_w_l1, router_b_l1, moe_w1_l1, moe_w3_l1, moe_w2_l1):
    bp, sp, _ = x_prompt.shape
    bs, ss, _ = x_sample.shape
    past = cache_k_l0.shape[1]
    groups = [
        dict(x=x_prompt.reshape(bp * sp, D), b=bp, s=sp, row=_row_prompt, prompt=True),
        dict(x=x_sample.reshape(bs * ss, D), b=bs, s=ss, row=_make_row_sample(ss), prompt=False),
    ]

    cvec = jnp.zeros((16, D), F32).at[0].set(c_ctx).at[1:1 + bs].set(c)
    mods0 = _ada(cvec, ada_w_l0, ada_b_l0).reshape(16, 1, 6 * D)
    mods1 = _ada(cvec, ada_w_l1, ada_b_l1).reshape(16, 1, 6 * D)

    w0_main = jnp.concatenate([mix_w_in_l0[:, :2048], mix_w_in_l0[:, 2080:]], axis=1).astype(BF16)
    w0_aux = _hilo(_pad_cols(mix_w_in_l0[:, 2048:2080], LANES))
    gbias = _pad_cols(mlstm_gate_b_l0.reshape(1, 4 * H_A), LANES)
    avg128 = _block_avg(LANES)
    q_g = jnp.tile(q_norm_g_l0, W_A // HD).reshape(1, W_A)
    k_g = jnp.tile(k_norm_g_l0, KV_B).reshape(1, LANES)
    w_out_h = mix_w_out_l0[:W_A].astype(BF16)
    w_out_o = mix_w_out_l0[W_A:].astype(BF16)
    ffn_w1 = ffn_w1_l0.astype(BF16)
    ffn_w3 = ffn_w3_l0.astype(BF16)
    ffn_w2 = ffn_w2_l0.astype(BF16)
    tables = _rope_tables(ss)
    kc = cache_k_l0.transpose(2, 0, 1, 3).reshape(KV_B, bs * past, HD).astype(BF16)
    vc = cache_v_l0.transpose(2, 0, 1, 3).reshape(KV_B, bs * past, HD)
    vc = jnp.concatenate([vc, jnp.ones_like(vc)], axis=-1).astype(BF16)
    c0 = jnp.concatenate([jnp.swapaxes(state_mlstm_c_l0, -1, -2),
                          jnp.broadcast_to(state_mlstm_n_l0[..., None, :], state_mlstm_c_l0.shape)], axis=-2)
    m0 = jnp.broadcast_to(state_mlstm_m_l0[..., None], state_mlstm_m_l0.shape + (LANES,))

    n_main1 = 2 * D_INNER + 2 * NG_C * D_STATE
    w1_main = ssd_w_in_l1[:, :n_main1].astype(BF16)
    w1_aux = _hilo(_pad_cols(ssd_w_in_l1[:, n_main1:], LANES))
    dtb = _pad_cols(ssd_dt_bias_l1.reshape(1, 2 * H_C), LANES)
    alog = _pad_cols(ssd_a_log_l1.reshape(1, 2 * H_C), LANES)
    dsk = jnp.broadcast_to(jnp.repeat(ssd_d_l1, HD)[:, None], (D_INNER, LANES))
    ssd_ng = ssd_norm_g_l1.reshape(1, D_INNER)
    w_out1 = ssd_w_out_l1.astype(BF16)
    rw_hilo = _hilo(_pad_cols(router_w_l1, LANES))
    rb = jnp.full((1, LANES), NEG, F32).at[0, :N_EXPERTS].set(router_b_l1)
    moe_w1 = moe_w1_l1.astype(BF16)
    moe_w3 = moe_w3_l1.astype(BF16)
    moe_w2 = moe_w2_l1.astype(BF16)
    s0 = state_ssd_l1.reshape(bs, 2, NG_C, HPG_C * HD, D_STATE)

    outs = {}
    for grp in groups:
        x, b, s, row, prompt = grp["x"], grp["b"], grp["s"], grp["row"], grp["prompt"]
        pm, pa = _mm_in(x, mods0, row, w0_main, w0_aux, tm=512, n_chunks=2)
        if prompt:
            hn, cst, mst = _mlstm(pm, pa, gbias, mlstm_norm_g_l0.reshape(1, W_A), b, s, out_state=True)
            qn, kh, vh, kn = _qkprep(pm, q_g, k_g, avg128, s, emit_kn=True)
            o = _attn(qn, kh, vh, b, s)
            outs["k"] = kn.reshape(b, s, KV_B, HD)
            outs["v"] = pm[:, 2688:2816].astype(F32).reshape(b, s, KV_B, HD)
            outs["c"] = jnp.swapaxes(cst[..., :HD, :], -1, -2)
            outs["n"] = cst[..., HD, :]
            outs["m"] = mst[..., 0]
        else:
            hn = _mlstm(pm, pa, gbias, mlstm_norm_g_l0.reshape(1, W_A), b, s, init=(c0, m0))[0]
            qn, kh, vh = _qkprep(pm, q_g, k_g, avg128, s, tables=tables)
            o = _attn(qn, kh, vh, b, s, ctx=(kc, vc))
        x1, u2 = _out_proj([hn, o], [w_out_h, w_out_o], x, mods0, row, ln1_g_l0, ln1_b_l0)
        x2 = _ffn(u2, ffn_w1, ffn_w3, ffn_w2, x1, mods0, row, ln2_g_l0, ln2_b_l0, tf=D_FF // 2)
        pm, pa = _mm_in(x2, mods1, row, w1_main, w1_aux, tm=512, n_chunks=4)
        xt = _conv(pm, ssd_conv_w_l1[:, :D_INNER], ssd_conv_b_l1[:D_INNER], b, s, D_INNER, True)
        bc = _conv(pm, ssd_conv_w_l1[:, D_INNER:], ssd_conv_b_l1[D_INNER:], b, s, 2 * D_INNER, False)
        if prompt:
            yn, sst = _ssd(xt, bc, pa, pm, dtb, alog, dsk, ssd_ng, b, s, out_state=True)
            outs["s"] = sst.reshape(b, 2, H_C, HD, D_STATE)
        else:
            yn = _ssd(xt, bc, pa, pm, dtb, alog, dsk, ssd_ng, b, s, init=s0)[0]
        x1, u2, gate = _out_proj([yn], [w_out1], x2, mods1, row, ln1_g_l1, ln1_b_l1, router=(rw_hilo, rb))
        x3 = _moe(u2, moe_w1, moe_w3, moe_w2, gate, x1, mods1, row, ln2_g_l1, ln2_b_l1,
                  tm=min(1024, b * s if prompt else s))
        outs["y_p" if prompt else "y_s"] = x3.reshape(b, s, D)

    return (outs["y_p"], outs["y_s"], outs["k"], outs["v"], outs["c"], outs["n"], outs["m"], outs["s"])
```

```python
import functools

import jax
import jax.numpy as jnp
import numpy as np
from jax import lax
from jax.experimental import pallas as pl
from jax.experimental.pallas import tpu as pltpu

F32 = jnp.float32
BF16 = jnp.bfloat16

D = 1024
CH = 128
LANES = 128
EPS = 1e-6
DEPTH = 2
ALPHA = (2 * DEPTH) ** 0.25
GRID_W = 64
ROPE_THETA = 10000.0
H_A = 8
HD = 64
W_A = 512
KV_B = 2
H_C = 32
NG_C = 4
HPG_C = 8
D_STATE = 128
D_INNER = 2048
D_FF = 2816
N_EXPERTS = 8
D_FF_E = 1408
NEG = -1e30
LOG2E = 1.4426950408889634
VMEM_LIMIT = 56 * 1024 * 1024


def _cp(*sem):
    return pltpu.CompilerParams(dimension_semantics=sem, vmem_limit_bytes=VMEM_LIMIT)


def _sigmoid(x):
    return 1.0 / (1.0 + jnp.exp(-x))


def _silu(x):
    return x * _sigmoid(x)


def _softplus(x):
    return jnp.maximum(x, 0.0) + jnp.log(1.0 + jnp.exp(-jnp.abs(x)))


def _split2(x):
    hi = x.astype(BF16)
    lo = (x - hi.astype(F32)).astype(BF16)
    return hi, lo


def _split3(x):
    h1 = x.astype(BF16)
    r = x - h1.astype(F32)
    h2 = r.astype(BF16)
    h3 = (r - h2.astype(F32)).astype(BF16)
    return h1, h2, h3


def _dot(a, b):
    return jnp.dot(a, b, preferred_element_type=F32)


def _dot_nt(a, b):
    return lax.dot_general(a, b, (((1,), (1,)), ((), ())), preferred_element_type=F32)


def _tri_dot(tri, x):
    h1, h2, h3 = _split3(x)
    return _dot(tri, h1) + _dot(tri, h2) + _dot(tri, h3)


def _layer_norm(r, g, b):
    mu = jnp.mean(r, axis=-1, keepdims=True)
    d = r - mu
    var = jnp.mean(d * d, axis=-1, keepdims=True)
    return d * lax.rsqrt(var + EPS) * g + b


def _ada_kernel(c_ref, w_ref, b_ref, o_ref):
    s = _silu(c_ref[...])
    sh, sl = _split2(s)
    wh, wl = _split2(w_ref[...])
    o_ref[...] = _dot(sh, wh) + _dot(sl, wh) + _dot(sh, wl) + b_ref[...]


def _ada(cvec, w, b):
    n = w.shape[1]
    tn = 768
    return pl.pallas_call(
        _ada_kernel,
        grid=(n // tn,),
        in_specs=[pl.BlockSpec((16, D), lambda j: (0, 0)),
                  pl.BlockSpec((D, tn), lambda j: (0, j)),
                  pl.BlockSpec((1, tn), lambda j: (0, j))],
        out_specs=pl.BlockSpec((16, tn), lambda j: (0, j)),
        out_shape=jax.ShapeDtypeStruct((16, n), F32),
        compiler_params=_cp("parallel"),
        name="ada_mods",
    )(cvec, w, b.reshape(1, n))


def _dot_hilo(uh, ul, whl_ref):
    both = _dot(uh, whl_ref[...])
    return both[:, :LANES] + both[:, LANES:] + _dot(ul, whl_ref[:, :LANES])


def _hilo(w):
    return jnp.concatenate(_split2(w), axis=1)


def _mm_in_kernel(x_ref, sh_ref, sc_ref, wm_ref, wa_ref, om_ref, oa_ref, *, n_chunks):
    u = x_ref[...] * (1.0 + sc_ref[...]) + sh_ref[...]
    uh, ul = _split2(u)
    oa_ref[...] = _dot_hilo(uh, ul, wa_ref)
    tn = wm_ref.shape[1] // n_chunks
    for c in range(n_chunks):
        om_ref[:, c * tn:(c + 1) * tn] = _dot(uh, wm_ref[:, c * tn:(c + 1) * tn]).astype(om_ref.dtype)


def _mm_in(x, mods3, row_fn, wm, wa_hilo, tm, n_chunks):
    t, nm = x.shape[0], wm.shape[1]
    once = pl.Buffered(1)
    return pl.pallas_call(
        functools.partial(_mm_in_kernel, n_chunks=n_chunks),
        grid=(t // tm,),
        in_specs=[pl.BlockSpec((tm, D), lambda i: (i, 0)),
                  pl.BlockSpec((None, 1, D), lambda i: (row_fn(i, tm), 0, 0)),
                  pl.BlockSpec((None, 1, D), lambda i: (row_fn(i, tm), 0, 1)),
                  pl.BlockSpec((D, nm), lambda i: (0, 0), pipeline_mode=once),
                  pl.BlockSpec((D, 2 * LANES), lambda i: (0, 0), pipeline_mode=once)],
        out_specs=[pl.BlockSpec((tm, nm), lambda i: (i, 0)),
                   pl.BlockSpec((tm, LANES), lambda i: (i, 0))],
        out_shape=[jax.ShapeDtypeStruct((t, nm), BF16), jax.ShapeDtypeStruct((t, LANES), F32)],
        compiler_params=_cp("parallel"),
        name="mm_in",
    )(x, mods3, mods3, wm, wa_hilo)


def _mlstm_kernel(*refs, nc, has_init, out_state):
    q_ref, k_ref, v_ref, om_ref, g_ref, gn_ref, gb_ref, ng_ref = refs[:8]
    pos = 8
    if has_init:
        c0_ref, m0_ref = refs[pos:pos + 2]
        pos += 2
    hn_ref = refs[pos]
    pos += 1
    if out_state:
        cst_ref, mst_ref = refs[pos:pos + 2]
        pos += 2
    c_scr, m_scr, hb_scr, r_scr, rows_scr = refs[pos:pos + 5]

    j = pl.program_id(1)
    is_fwd = j >= nc
    ci = jnp.where(is_fwd, j - nc, nc - 1 - j)
    sgn = jnp.where(is_fwd, 1, -1)

    def _init(d):
        if has_init:
            c_scr[...] = c0_ref[d]
            m_scr[...] = m0_ref[d]
        else:
            c_scr[...] = jnp.zeros(c_scr.shape, F32)
            m_scr[...] = jnp.zeros(m_scr.shape, F32)

    pl.when(j == 0)(lambda: _init(1))
    pl.when(j == nc)(lambda: _init(0))

    def _prep(graw_ref, sg, slot):
        row = lax.broadcasted_iota(jnp.int32, (CH, CH), 0)
        col = lax.broadcasted_iota(jnp.int32, (CH, CH), 1)
        tri = jnp.where((col - row) * sg <= 0, 1.0, 0.0).astype(BF16)
        f_mat = (jnp.zeros((CH, LANES), jnp.int32) + sg) > 0
        g = graw_ref[...] + gb_ref[...]
        li = jnp.where(f_mat, g, pltpu.roll(g, LANES - 16, axis=1))
        lfr = jnp.where(f_mat, pltpu.roll(g, LANES - 8, axis=1), pltpu.roll(g, LANES - 24, axis=1))
        lf = jnp.minimum(lfr, 0.0) - jnp.log(1.0 + jnp.exp(-jnp.abs(lfr)))
        bc = _tri_dot(tri, lf)
        rr = li - bc
        cmx = rr
        for dlt in (1, 2, 4, 8, 16, 32, 64):
            before = jnp.where(row >= dlt, pltpu.roll(cmx, dlt, axis=0), NEG)
            after = jnp.where(row < CH - dlt, pltpu.roll(cmx, CH - dlt, axis=0), NEG)
            cmx = jnp.maximum(cmx, jnp.where(f_mat, before, after))
        r_scr[slot] = rr * LOG2E
        rows_scr[slot, 0] = rr.T[0:H_A, :]
        rows_scr[slot, 1] = bc.T[0:H_A, :]
        rows_scr[slot, 2] = cmx.T[0:H_A, :]

    pl.when(j == 0)(lambda: _prep(g_ref, sgn, 0))

    slot = j % 2
    row = lax.broadcasted_iota(jnp.int32, (CH, CH), 0)
    col = lax.broadcasted_iota(jnp.int32, (CH, CH), 1)
    causal_t = (row - col) * sgn <= 0
    fwd8 = (jnp.zeros((H_A, LANES), jnp.int32) + sgn) > 0
    r = r_scr[slot]
    r_t = rows_scr[slot, 0]
    b_t = rows_scr[slot, 1]
    cm = rows_scr[slot, 2]
    _prep(gn_ref, jnp.where(j + 1 >= nc, 1, -1), (j + 1) % 2)

    m_prev = m_scr[...]
    g_t = jnp.maximum(m_prev, cm)
    mt_t = b_t + g_t
    wst_t = jnp.exp(m_prev - g_t)
    emt_t = jnp.exp(-mt_t)
    fwd_c = fwd8[:, 0:1]
    m_new = jnp.where(fwd_c, mt_t[:, LANES - 1:LANES], mt_t[:, 0:1])
    b_last = jnp.where(fwd_c, b_t[:, LANES - 1:LANES], b_t[:, 0:1])
    ws_t = jnp.exp(r_t + (b_last - m_new))
    decay = jnp.exp(b_last + m_prev[:, 0:1] - m_new)
    m_scr[...] = jnp.broadcast_to(m_new, (H_A, LANES))

    q_t = q_ref[...].astype(F32).T.astype(BF16)
    v_t = v_ref[...].astype(F32).T
    ones = jnp.ones((HD, CH), F32)

    g2_t = g_t * LOG2E
    heads = range(H_A)
    sls = [slice(h * HD, (h + 1) * HD) for h in heads]
    khs = [k_ref[:, sl] for sl in sls]
    qhs = [q_t[sl, :] for sl in sls]
    vexts = [jnp.concatenate([v_t[sl, :], ones], axis=0) for sl in sls]
    cexts = [c_scr[h] for h in heads]
    scs = [_dot(khs[h], qhs[h]) for h in heads]
    w_ts = [jnp.exp2(jnp.where(causal_t, r[:, h:h + 1] - g2_t[h:h + 1, :], NEG)) for h in heads]
    sc_ts = [(scs[h] * 0.125 * w_ts[h]).astype(BF16) for h in heads]
    cqs = [_dot(cexts[h].astype(BF16), qhs[h]) for h in heads]
    nds = [_dot(vexts[h].astype(BF16), sc_ts[h]) + (cqs[h] * 0.125) * wst_t[h:h + 1, :] for h in heads]
    hts = [nds[h][0:HD, :] / jnp.maximum(jnp.abs(nds[h][HD:HD + 1, :]), emt_t[h:h + 1, :]) for h in heads]
    for h in heads:
        c_scr[h] = decay[h:h + 1, :] * cexts[h] + _dot((vexts[h] * ws_t[h:h + 1, :]).astype(BF16), khs[h])
    hcat_t = jnp.concatenate(hts, axis=0)

    @pl.when(j < nc)
    def _():
        hb_scr[ci] = hcat_t

    @pl.when(j >= nc)
    def _():
        ht = hcat_t + hb_scr[ci]
        normed = []
        for h in range(H_A):
            hh = ht[h * HD:(h + 1) * HD, :]
            dlt = hh - jnp.mean(hh, axis=0, keepdims=True)
            normed.append(dlt * lax.rsqrt(jnp.mean(dlt * dlt, axis=0, keepdims=True) + EPS))
        hn = jnp.concatenate(normed, axis=0).T * ng_ref[...] * _sigmoid(om_ref[...].astype(F32))
        hn_ref[...] = hn.astype(hn_ref.dtype)

    if out_state:
        @pl.when(j == nc - 1)
        def _():
            cst_ref[1] = c_scr[...]
            mst_ref[1] = m_scr[...]

        @pl.when(j == 2 * nc - 1)
        def _():
            cst_ref[0] = c_scr[...]
            mst_ref[0] = m_scr[...]


def _mlstm(pm, pa, gbias, norm_g, bsz, s_len, init=None, out_state=False):
    nc = s_len // CH
    t = bsz * s_len

    def chunk(b, j):
        return b * nc + jnp.where(j >= nc, j - nc, nc - 1 - j)

    def chunk_next(b, j):
        return chunk(b, jnp.minimum(j + 1, 2 * nc - 1))

    def chunk_fwd(b, j):
        return b * nc + jnp.maximum(j - nc, 0)

    in_specs = [pl.BlockSpec((CH, W_A), lambda b, j: (chunk(b, j), 0)),
                pl.BlockSpec((CH, W_A), lambda b, j: (chunk(b, j), 1)),
                pl.BlockSpec((CH, W_A), lambda b, j: (chunk(b, j), 2)),
                pl.BlockSpec((CH, W_A), lambda b, j: (chunk_fwd(b, j), 3)),
                pl.BlockSpec((CH, LANES), lambda b, j: (chunk(b, j), 0)),
                pl.BlockSpec((CH, LANES), lambda b, j: (chunk_next(b, j), 0)),
                pl.BlockSpec((1, LANES), lambda b, j: (0, 0)),
                pl.BlockSpec((1, W_A), lambda b, j: (0, 0))]
    args = [pm, pm, pm, pm, pa, pa, gbias, norm_g]
    if init is not None:
        in_specs += [pl.BlockSpec((None, 2, H_A, LANES, HD), lambda b, j: (b, 0, 0, 0, 0)),
                     pl.BlockSpec((None, 2, H_A, LANES), lambda b, j: (b, 0, 0, 0))]
        args += list(init)
    out_specs = [pl.BlockSpec((CH, W_A), lambda b, j: (chunk_fwd(b, j), 0))]
    out_shape = [jax.ShapeDtypeStruct((t, W_A), BF16)]
    if out_state:
        out_specs += [pl.BlockSpec((None, 2, H_A, LANES, HD), lambda b, j: (b, 0, 0, 0, 0)),
                      pl.BlockSpec((None, 2, H_A, LANES), lambda b, j: (b, 0, 0, 0))]
        out_shape += [jax.ShapeDtypeStruct((bsz, 2, H_A, LANES, HD), F32),
                      jax.ShapeDtypeStruct((bsz, 2, H_A, LANES), F32)]
    return pl.pallas_call(
        functools.partial(_mlstm_kernel, nc=nc, has_init=init is not None, out_state=out_state),
        grid=(bsz, 2 * nc),
        in_specs=in_specs,
        out_specs=out_specs,
        out_shape=out_shape,
        scratch_shapes=[pltpu.VMEM((H_A, LANES, HD), F32), pltpu.VMEM((H_A, LANES), F32),
                        pltpu.VMEM((nc, W_A, CH), F32),
                        pltpu.VMEM((2, CH, LANES), F32), pltpu.VMEM((2, 3, H_A, LANES), F32)],
        compiler_params=_cp("parallel", "arbitrary"),
        name="mlstm",
    )(*args)


def _group_rms(x, avg, g):
    s1, s2 = _split2(x * x)
    ms = _dot(s1, avg) + _dot(s2, avg)
    return x * lax.rsqrt(ms + EPS) * g


def _rope(x, cos, sin):
    lane = lax.broadcasted_iota(jnp.int32, x.shape, 1)
    swapped = jnp.where((lane % 32) < 16, pltpu.roll(x, LANES - 16, axis=1), pltpu.roll(x, 16, axis=1))
    return x * cos + swapped * sin


def _qkprep_kernel(*refs, rope, emit_kn):
    q_ref, k_ref, v_ref, qg_ref, kg_ref, avg_ref = refs[:6]
    pos = 6
    if rope:
        cos_ref, sin_ref = refs[pos:pos + 2]
        pos += 2
    qn_ref, kh_ref, vh_ref = refs[pos:pos + 3]
    pos += 3
    avg = avg_ref[...]
    qs = []
    for c in range(W_A // LANES):
        sl = slice(c * LANES, (c + 1) * LANES)
        xn = _group_rms(q_ref[:, sl].astype(F32), avg, qg_ref[:, sl])
        if rope:
            xn = _rope(xn, cos_ref[...], sin_ref[...])
        qs.append(xn * 0.125)
    qn_ref[...] = jnp.concatenate(qs, axis=1).astype(qn_ref.dtype)
    kn = _group_rms(k_ref[...].astype(F32), avg, kg_ref[...])
    if emit_kn:
        refs[pos][...] = kn
    if rope:
        kn = _rope(kn, cos_ref[...], sin_ref[...])
    v = v_ref[...].astype(F32)
    ones = jnp.ones((v.shape[0], HD), F32)
    for gi in range(KV_B):
        sl = slice(gi * HD, (gi + 1) * HD)
        kh_ref[gi] = kn[:, sl].astype(kh_ref.dtype)
        vh_ref[gi] = jnp.concatenate([v[:, sl], ones], axis=1).astype(vh_ref.dtype)


def _qkprep(pm, q_g, k_g, avg128, s_len, tables=None, emit_kn=False):
    t = pm.shape[0]
    tm = 256
    nt = s_len // tm
    qoff, koff, voff = 2048 // W_A, 2560 // LANES, 2688 // LANES
    in_specs = [pl.BlockSpec((tm, W_A), lambda i: (i, qoff)),
                pl.BlockSpec((tm, LANES), lambda i: (i, koff)),
                pl.BlockSpec((tm, LANES), lambda i: (i, voff)),
                pl.BlockSpec((1, W_A), lambda i: (0, 0)),
                pl.BlockSpec((1, LANES), lambda i: (0, 0)),
                pl.BlockSpec((LANES, LANES), lambda i: (0, 0))]
    args = [pm, pm, pm, q_g, k_g, avg128]
    if tables is not None:
        in_specs += [pl.BlockSpec((tm, LANES), lambda i: (i % nt, 0))] * 2
        args += list(tables)
    out_specs = [pl.BlockSpec((tm, W_A), lambda i: (i, 0)),
                 pl.BlockSpec((KV_B, tm, HD), lambda i: (0, i, 0)),
                 pl.BlockSpec((KV_B, tm, LANES), lambda i: (0, i, 0))]
    out_shape = [jax.ShapeDtypeStruct((t, W_A), BF16),
                 jax.ShapeDtypeStruct((KV_B, t, HD), BF16),
                 jax.ShapeDtypeStruct((KV_B, t, LANES), BF16)]
    if emit_kn:
        out_specs.append(pl.BlockSpec((tm, LANES), lambda i: (i, 0)))
        out_shape.append(jax.ShapeDtypeStruct((t, LANES), F32))
    return pl.pallas_call(
        functools.partial(_qkprep_kernel, rope=tables is not None, emit_kn=emit_kn),
        grid=(t // tm,),
        in_specs=in_specs,
        out_specs=out_specs,
        out_shape=out_shape,
        compiler_params=_cp("parallel"),
        name="qkprep",
    )(*args)


def _attn_kernel(*refs, has_ctx):
    if has_ctx:
        q_ref, k_ref, v_ref, kc_ref, vc_ref, o_ref = refs
    else:
        q_ref, k_ref, v_ref, o_ref = refs
    k = k_ref[...]
    v = v_ref[...]
    hq = range(4)
    qhs = [q_ref[:, h * HD:(h + 1) * HD] for h in hq]
    ss = [_dot_nt(qh, k) for qh in qhs]
    ms = [jnp.max(s, axis=1, keepdims=True) for s in ss]
    if has_ctx:
        scs = [_dot_nt(qh, kc_ref[...]) for qh in qhs]
        ms = [jnp.maximum(m, jnp.max(sc, axis=1, keepdims=True)) for m, sc in zip(ms, scs)]
    oes = [_dot(jnp.exp(s - m).astype(BF16), v) for s, m in zip(ss, ms)]
    if has_ctx:
        oes = [oe + _dot(jnp.exp(sc - m).astype(BF16), vc_ref[...]) for oe, sc, m in zip(oes, scs, ms)]
    outs = [oe[:, :HD] / oe[:, HD:HD + 1] for oe in oes]
    o_ref[...] = jnp.concatenate(outs, axis=1).astype(o_ref.dtype)


def _attn(qn, kh, vh, bsz, s_len, ctx=None):
    t = qn.shape[0]
    tq = min(512, s_len)
    nq = s_len // tq
    in_specs = [pl.BlockSpec((tq, 256), lambda b, g, i: (b * nq + i, g)),
                pl.BlockSpec((None, s_len, HD), lambda b, g, i: (g, b, 0)),
                pl.BlockSpec((None, s_len, LANES), lambda b, g, i: (g, b, 0))]
    args = [qn, kh, vh]
    if ctx is not None:
        past = ctx[0].shape[1] // bsz
        in_specs += [pl.BlockSpec((None, past, HD), lambda b, g, i: (g, b, 0)),
                     pl.BlockSpec((None, past, LANES), lambda b, g, i: (g, b, 0))]
        args += list(ctx)
    return pl.pallas_call(
        functools.partial(_attn_kernel, has_ctx=ctx is not None),
        grid=(bsz, KV_B, nq),
        in_specs=in_specs,
        out_specs=pl.BlockSpec((tq, 256), lambda b, g, i: (b * nq + i, g)),
        out_shape=jax.ShapeDtypeStruct((t, W_A), BF16),
        compiler_params=_cp("parallel", "parallel", "parallel"),
        name="attn",
    )(*args)


def _out_kernel(*refs, n_in, router):
    a_refs = refs[:n_in]
    w_refs = refs[n_in:2 * n_in]
    x_ref, g1_ref, sh2_ref, sc2_ref, lng_ref, lnb_ref = refs[2 * n_in:2 * n_in + 6]
    pos = 2 * n_in + 6
    if router:
        rw_ref, rb_ref = refs[pos:pos + 2]
        pos += 2
    x1_ref, u2_ref = refs[pos:pos + 2]
    y = _dot(a_refs[0][...], w_refs[0][...])
    for a_ref, w_ref in zip(a_refs[1:], w_refs[1:]):
        y = y + _dot(a_ref[...], w_ref[...])
    x1 = _layer_norm(ALPHA * x_ref[...] + g1_ref[...] * y, lng_ref[...], lnb_ref[...])
    x1_ref[...] = x1
    u2 = x1 * (1.0 + sc2_ref[...]) + sh2_ref[...]
    uh, ul = _split2(u2)
    u2_ref[...] = uh
    if router:
        gate_ref = refs[pos + 2]
        logits = _dot_hilo(uh, ul, rw_ref) + rb_ref[...]
        lane = lax.broadcasted_iota(jnp.int32, logits.shape, 1)
        m1 = jnp.max(logits, axis=1, keepdims=True)
        i1 = jnp.min(jnp.where(logits == m1, lane, LANES), axis=1, keepdims=True)
        rest = jnp.where(lane == i1, 2.0 * NEG, logits)
        m2 = jnp.max(rest, axis=1, keepdims=True)
        i2 = jnp.min(jnp.where(rest == m2, lane, LANES), axis=1, keepdims=True)
        e = jnp.exp(m2 - m1)
        p1 = 1.0 / (1.0 + e)
        gate_ref[...] = jnp.where(lane == i1, p1, jnp.where(lane == i2, e * p1, 0.0))


def _out_proj(acts, ws, x, mods3, row_fn, ln_g, ln_b, router=None):
    t = x.shape[0]
    tm = 1024
    n_in = len(acts)
    in_specs = [pl.BlockSpec((tm, a.shape[1]), lambda i: (i, 0)) for a in acts]
    in_specs += [pl.BlockSpec(w.shape, lambda i: (0, 0)) for w in ws]
    in_specs += [pl.BlockSpec((tm, D), lambda i: (i, 0))]
    in_specs += [pl.BlockSpec((None, 1, D), functools.partial(lambda i, k: (row_fn(i, tm), 0, k), k=k))
                 for k in (2, 3, 4)]
    in_specs += [pl.BlockSpec((1, D), lambda i: (0, 0))] * 2
    args = list(acts) + list(ws) + [x, mods3, mods3, mods3, ln_g.reshape(1, D), ln_b.reshape(1, D)]
    out_specs = [pl.BlockSpec((tm, D), lambda i: (i, 0)), pl.BlockSpec((tm, D), lambda i: (i, 0))]
    out_shape = [jax.ShapeDtypeStruct((t, D), F32), jax.ShapeDtypeStruct((t, D), BF16)]
    if router is not None:
        in_specs += [pl.BlockSpec((D, 2 * LANES), lambda i: (0, 0)), pl.BlockSpec((1, LANES), lambda i: (0, 0))]
        args += list(router)
        out_specs.append(pl.BlockSpec((tm, LANES), lambda i: (i, 0)))
        out_shape.append(jax.ShapeDtypeStruct((t, LANES), F32))
    return pl.pallas_call(
        functools.partial(_out_kernel, n_in=n_in, router=router is not None),
        grid=(t // tm,),
        in_specs=in_specs,
        out_specs=out_specs,
        out_shape=out_shape,
        compiler_params=_cp("parallel"),
        name="out_proj",
    )(*args)


def _ffn_kernel(u_ref, w1_ref, w3_ref, w2_ref, x1_ref, g2_ref, lng_ref, lnb_ref, o_ref, *, tf):
    u = u_ref[...]
    acc = None
    for c in range(w1_ref.shape[1] // tf):
        cs = slice(c * tf, (c + 1) * tf)
        h = _silu(_dot(u, w1_ref[:, cs])) * _dot(u, w3_ref[:, cs])
        part = _dot(h.astype(BF16), w2_ref[cs, :])
        acc = part if acc is None else acc + part
    o_ref[...] = _layer_norm(ALPHA * x1_ref[...] + g2_ref[...] * acc, lng_ref[...], lnb_ref[...])


def _ffn(u2, w1, w3, w2, x1, mods3, row_fn, ln_g, ln_b, tf):
    t = u2.shape[0]
    tm = 512
    f = w1.shape[1]
    once = pl.Buffered(1)
    return pl.pallas_call(
        functools.partial(_ffn_kernel, tf=tf),
        grid=(t // tm,),
        in_specs=[pl.BlockSpec((tm, D), lambda i: (i, 0)),
                  pl.BlockSpec((D, f), lambda i: (0, 0), pipeline_mode=once),
                  pl.BlockSpec((D, f), lambda i: (0, 0), pipeline_mode=once),
                  pl.BlockSpec((f, D), lambda i: (0, 0), pipeline_mode=once),
                  pl.BlockSpec((tm, D), lambda i: (i, 0)),
                  pl.BlockSpec((None, 1, D), lambda i: (row_fn(i, tm), 0, 5)),
                  pl.BlockSpec((1, D), lambda i: (0, 0)),
                  pl.BlockSpec((1, D), lambda i: (0, 0))],
        out_specs=pl.BlockSpec((tm, D), lambda i: (i, 0)),
        out_shape=jax.ShapeDtypeStruct((t, D), F32),
        compiler_params=_cp("parallel"),
        name="ffn",
    )(u2, w1, w3, w2, x1, mods3, ln_g.reshape(1, D), ln_b.reshape(1, D))


MOE_CAP = 256


def _moe_kernel(u_ref, w1_ref, w3_ref, w2_ref, gate_ref, tril_ref, x1_ref, g2_ref, lng_ref, lnb_ref, o_ref,
                acc_scr, rank_scr, rank_t_scr, gate_t_scr):
    e = pl.program_id(1)
    tm = u_ref.shape[0]

    @pl.when(e == 0)
    def _():
        acc_scr[...] = jnp.zeros(acc_scr.shape, F32)
        gate = gate_ref[...]
        routed = gate > 0.0
        rank = jnp.where(routed, _dot(tril_ref[...], jnp.where(routed, 1.0, 0.0).astype(BF16)), -1.0)
        rank_scr[...] = rank
        rank_t_scr[...] = rank.T
        gate_t_scr[...] = gate.T

    lane = lax.broadcasted_iota(jnp.int32, (tm, LANES), 1)
    rank_col = jnp.sum(jnp.where(lane == e, rank_scr[...], 0.0), axis=1, keepdims=True)
    rank_row = rank_t_scr[pl.ds(e, 1), :]
    gate_row = gate_t_scr[pl.ds(e, 1), :]
    count = jnp.sum(jnp.where(rank_row >= 0.0, 1, 0))
    u = u_ref[...]

    def one_pass(first_slot, cap):
        off = first_slot.astype(F32)
        slot_c = lax.broadcasted_iota(jnp.int32, (cap, tm), 0).astype(F32) + off
        sel = jnp.where(rank_row == slot_c, 1.0, 0.0)
        weight = jnp.sum(sel * gate_row, axis=1, keepdims=True)
        xe = _dot(sel.astype(BF16), u).astype(BF16)
        h = _silu(_dot(xe, w1_ref[...])) * _dot(xe, w3_ref[...])
        y = _dot(h.astype(BF16), w2_ref[...]) * weight
        slot_r = lax.broadcasted_iota(jnp.int32, (tm, cap), 1).astype(F32) + off
        sel_t = jnp.where(rank_col == slot_r, 1.0, 0.0).astype(BF16)
        acc_scr[...] += _dot(sel_t, y.astype(BF16))

    quarter = MOE_CAP // 4
    n_full = count // MOE_CAP
    n_q = (count - n_full * MOE_CAP + quarter - 1) // quarter
    n_full = n_full + jnp.where(n_q >= 3, 1, 0)
    n_q = jnp.where(n_q >= 3, 0, n_q)

    def full_pass(blk, carry):
        one_pass(blk * MOE_CAP, MOE_CAP)
        return carry

    lax.fori_loop(0, n_full, full_pass, 0)

    @pl.when(n_q >= 2)
    def _():
        one_pass(n_full * MOE_CAP, 2 * quarter)

    @pl.when(n_q % 2 == 1)
    def _():
        one_pass(n_full * MOE_CAP + jnp.where(n_q >= 2, 2 * quarter, 0), quarter)

    @pl.when(e == pl.num_programs(1) - 1)
    def _():
        o_ref[...] = _layer_norm(ALPHA * x1_ref[...] + g2_ref[...] * acc_scr[...], lng_ref[...], lnb_ref[...])


def _moe(u2, w1, w3, w2, gate, x1, mods3, row_fn, ln_g, ln_b, tm):
    t = u2.shape[0]
    n_e, _, f = w1.shape
    tril = jnp.tril(jnp.ones((tm, tm), F32), -1).astype(BF16)
    once = pl.Buffered(1)
    in_specs = [pl.BlockSpec((tm, D), lambda i, e: (i, 0)),
                pl.BlockSpec((None, D, f), lambda i, e: (e, 0, 0)),
                pl.BlockSpec((None, D, f), lambda i, e: (e, 0, 0)),
                pl.BlockSpec((None, f, D), lambda i, e: (e, 0, 0)),
                pl.BlockSpec((tm, LANES), lambda i, e: (i, 0)),
                pl.BlockSpec((tm, tm), lambda i, e: (0, 0), pipeline_mode=once),
                pl.BlockSpec((tm, D), lambda i, e: (i, 0), pipeline_mode=once),
                pl.BlockSpec((None, 1, D), lambda i, e: (row_fn(i, tm), 0, 5)),
                pl.BlockSpec((1, D), lambda i, e: (0, 0)),
                pl.BlockSpec((1, D), lambda i, e: (0, 0))]
    return pl.pallas_call(
        _moe_kernel,
        grid=(t // tm, n_e),
        in_specs=in_specs,
        out_specs=pl.BlockSpec((tm, D), lambda i, e: (i, 0)),
        out_shape=jax.ShapeDtypeStruct((t, D), F32),
        scratch_shapes=[pltpu.VMEM((tm, D), F32), pltpu.VMEM((tm, LANES), F32),
                        pltpu.VMEM((LANES, tm), F32), pltpu.VMEM((LANES, tm), F32)],
        compiler_params=_cp("parallel", "arbitrary"),
        name="moe",
    )(u2, w1, w3, w2, gate, tril, x1, mods3, ln_g.reshape(1, D), ln_b.reshape(1, D))


def _conv_kernel(x_ref, w_ref, b_ref, o_ref, *, s_len, transpose):
    rows = x_ref.shape[0]
    bps = s_len // CH
    r = lax.broadcasted_iota(jnp.int32, (CH, 2 * CH), 0)
    c = lax.broadcasted_iota(jnp.int32, (CH, 2 * CH), 1)
    prev_win = jnp.where(c == r + CH - 1, 1.0, 0.0).astype(BF16)
    next_win = jnp.where(c == r + 1, 1.0, 0.0).astype(BF16)
    r1 = lax.broadcasted_iota(jnp.int32, (CH, CH), 0)
    c1 = lax.broadcasted_iota(jnp.int32, (CH, CH), 1)
    prev_in = jnp.where(c1 == r1 - 1, 1.0, 0.0).astype(BF16)
    next_in = jnp.where(c1 == r1 + 1, 1.0, 0.0).astype(BF16)
    w0, w1, w2, bias = w_ref[0:1, :], w_ref[1:2, :], w_ref[2:3, :], b_ref[...]
    seq_blocks = []
    for bi in range(rows // CH):
        lo, hi = bi * CH, (bi + 1) * CH
        cur = x_ref[lo:hi, :]
        if bi % bps == 0:
            prev = _dot(prev_in, cur)
        else:
            prev = _dot(prev_win, x_ref[lo - CH:hi, :])
        if bi % bps == bps - 1:
            nxt = _dot(next_in, cur)
        else:
            nxt = _dot(next_win, x_ref[lo:hi + CH, :])
        y = _silu(w0 * prev + w1 * cur.astype(F32) + w2 * nxt + bias)
        if transpose:
            seq_blocks.append(y.T)
            if bi % bps == bps - 1:
                o_ref[bi // bps] = jnp.concatenate(seq_blocks, axis=1).astype(o_ref.dtype)
                seq_blocks = []
        else:
            o_ref[lo:hi, :] = y.astype(o_ref.dtype)


def _conv(pm, conv_w, conv_b, bsz, s_len, col_off, transpose):
    t = pm.shape[0]
    tc = 512
    n_ch = conv_w.shape[1]
    off = col_off // tc
    n_seq = max(1, min(bsz, 2048 // s_len))
    rows = n_seq * s_len
    if transpose:
        out_spec = pl.BlockSpec((n_seq, tc, s_len), lambda b, c: (b, c, 0))
        out_shape = jax.ShapeDtypeStruct((bsz, n_ch, s_len), BF16)
    else:
        out_spec = pl.BlockSpec((rows, tc), lambda b, c: (b, c))
        out_shape = jax.ShapeDtypeStruct((t, n_ch), BF16)
    return pl.pallas_call(
        functools.partial(_conv_kernel, s_len=s_len, transpose=transpose),
        grid=(bsz // n_seq, n_ch // tc),
        in_specs=[pl.BlockSpec((rows, tc), lambda b, c: (b, off + c)),
                  pl.BlockSpec((3, tc), lambda b, c: (0, c)),
                  pl.BlockSpec((1, tc), lambda b, c: (0, c))],
        out_specs=out_spec,
        out_shape=out_shape,
        compiler_params=_cp("parallel", "parallel"),
        name="ssd_conv",
    )(pm, conv_w, conv_b.reshape(1, n_ch))


def _ssd_kernel(*refs, nc, has_init, out_state):
    xt_ref, b_ref, c_ref, dt_ref, dtn_ref, z_ref, dtb_ref, alog_ref, dsk_ref, ng_ref = refs[:10]
    pos = 10
    if has_init:
        s0_ref = refs[pos]
        pos += 1
    yn_ref = refs[pos]
    pos += 1
    if out_state:
        sst_ref = refs[pos]
        pos += 1
    s_scr, yb_scr, yt_scr, xw_scr, cum_scr, rows_scr = refs[pos:pos + 6]

    j = pl.program_id(1)
    is_fwd = j >= nc
    ci = jnp.where(is_fwd, j - nc, nc - 1 - j)
    sgn = jnp.where(is_fwd, 1, -1)

    def _init(d):
        if has_init:
            s_scr[...] = s0_ref[d]
        else:
            s_scr[...] = jnp.zeros(s_scr.shape, F32)

    pl.when(j == 0)(lambda: _init(1))
    pl.when(j == nc)(lambda: _init(0))

    def _prep(dtraw_ref, sg, slot):
        row = lax.broadcasted_iota(jnp.int32, (CH, CH), 0)
        col = lax.broadcasted_iota(jnp.int32, (CH, CH), 1)
        tri = jnp.where((col - row) * sg <= 0, 1.0, 0.0).astype(BF16)
        f_row = (jnp.zeros((1, LANES), jnp.int32) + sg) > 0
        f_mat = (jnp.zeros((CH, LANES), jnp.int32) + sg) > 0
        f_hc = (jnp.zeros((H_C, LANES), jnp.int32) + sg) > 0
        raw = dtraw_ref[...] + dtb_ref[...]
        raw = jnp.where(f_mat, raw, pltpu.roll(raw, LANES - H_C, axis=1))
        alog = alog_ref[...]
        alog = jnp.where(f_row, alog, pltpu.roll(alog, LANES - H_C, axis=1))
        dtv = _softplus(raw)
        cumv = _tri_dot(tri, dtv * (-jnp.exp(alog)))
        cumv_t = cumv.T[0:H_C, :]
        dtv_t = dtv.T[0:H_C, :]
        last = jnp.where(f_hc, jnp.broadcast_to(cumv_t[:, CH - 1:CH], (H_C, LANES)),
                         jnp.broadcast_to(cumv_t[:, 0:1], (H_C, LANES)))
        cum_scr[slot] = cumv * LOG2E
        rows_scr[slot, 0] = cumv_t * LOG2E
        rows_scr[slot, 1] = dtv_t
        rows_scr[slot, 2] = jnp.exp(cumv_t)
        rows_scr[slot, 3] = jnp.exp(last - cumv_t) * dtv_t
        rows_scr[slot, 4] = jnp.exp(last)

    pl.when(j == 0)(lambda: _prep(dt_ref, sgn, 0))

    slot = j % 2
    row = lax.broadcasted_iota(jnp.int32, (CH, CH), 0)
    col = lax.broadcasted_iota(jnp.int32, (CH, CH), 1)
    causal_t = (row - col) * sgn <= 0
    cum = cum_scr[slot]
    cum_t = rows_scr[slot, 0]
    dt_t = rows_scr[slot, 1]
    ecum_t = rows_scr[slot, 2]
    ws_t = rows_scr[slot, 3]
    dec = rows_scr[slot, 4]
    _prep(dtn_ref, jnp.where(j + 1 >= nc, 1, -1), (j + 1) % 2)

    for gi in range(NG_C):
        gsl = slice(gi * D_STATE, (gi + 1) * D_STATE)
        bg = b_ref[:, gsl]
        cg = c_ref[:, gsl]
        cb_t = _dot_nt(bg, cg)
        ys_t = _dot_nt(s_scr[gi].astype(BF16), cg)
        for half in range(2):
            hr = range(half * 4, half * 4 + 4)
            rs = {rr: gi * HPG_C + rr for rr in hr}
            w_ts = {rr: (cb_t * jnp.exp2(jnp.where(causal_t, cum_t[rs[rr]:rs[rr] + 1, :] - cum[:, rs[rr]:rs[rr] + 1],
                                                   NEG))).astype(BF16)
                    for rr in hr}
            xr_ts = {rr: xt_ref[rs[rr] * HD:(rs[rr] + 1) * HD, :].astype(F32) for rr in hr}
            xd_ts = {rr: (xr_ts[rr] * dt_t[rs[rr]:rs[rr] + 1, :]).astype(BF16) for rr in hr}
            y_ts = {rr: _dot(xd_ts[rr], w_ts[rr]) + ys_t[rr * HD:(rr + 1) * HD, :] * ecum_t[rs[rr]:rs[rr] + 1, :]
                    for rr in hr}
            for rr in hr:
                yt_scr[rs[rr] * HD:(rs[rr] + 1) * HD, :] = y_ts[rr]
                xw_scr[rr * HD:(rr + 1) * HD, :] = (xr_ts[rr] * ws_t[rs[rr]:rs[rr] + 1, :]).astype(BF16)
        upd = _dot(xw_scr[...], bg)
        for rr in range(HPG_C):
            r = gi * HPG_C + rr
            hsl = slice(rr * HD, (rr + 1) * HD)
            s_scr[gi, hsl, :] = s_scr[gi, hsl, :] * dec[r:r + 1, :] + upd[hsl, :]

    @pl.when(j < nc)
    def _():
        yb_scr[ci] = yt_scr[...]

    @pl.when(j >= nc)
    def _():
        yt = yt_scr[...] + yb_scr[ci] + dsk_ref[...] * xt_ref[...].astype(F32)
        y = yt.T * _silu(z_ref[...].astype(F32))
        outs = []
        gw = D_INNER // NG_C
        for gi in range(NG_C):
            yg = y[:, gi * gw:(gi + 1) * gw]
            outs.append(yg * lax.rsqrt(jnp.mean(yg * yg, axis=1, keepdims=True) + EPS))
        yn_ref[...] = (jnp.concatenate(outs, axis=1) * ng_ref[...]).astype(yn_ref.dtype)

    if out_state:
        @pl.when(j == nc - 1)
        def _():
            sst_ref[1] = s_scr[...]

        @pl.when(j == 2 * nc - 1)
        def _():
            sst_ref[0] = s_scr[...]


def _ssd(xt, bc, pa, pm, dtb, alog, dsk, norm_g, bsz, s_len, init=None, out_state=False):
    nc = s_len // CH
    t = bsz * s_len
    gw = NG_C * D_STATE

    def chunk_i(j):
        return jnp.where(j >= nc, j - nc, nc - 1 - j)

    def chunk(b, j):
        return b * nc + chunk_i(j)

    def chunk_next(b, j):
        return chunk(b, jnp.minimum(j + 1, 2 * nc - 1))

    def chunk_fwd(b, j):
        return b * nc + jnp.maximum(j - nc, 0)

    in_specs = [pl.BlockSpec((None, D_INNER, CH), lambda b, j: (b, 0, chunk_i(j))),
                pl.BlockSpec((CH, gw), lambda b, j: (chunk(b, j), 0)),
                pl.BlockSpec((CH, gw), lambda b, j: (chunk(b, j), 1)),
                pl.BlockSpec((CH, LANES), lambda b, j: (chunk(b, j), 0)),
                pl.BlockSpec((CH, LANES), lambda b, j: (chunk_next(b, j), 0)),
                pl.BlockSpec((CH, D_INNER), lambda b, j: (chunk_fwd(b, j), 0)),
                pl.BlockSpec((1, LANES), lambda b, j: (0, 0)),
                pl.BlockSpec((1, LANES), lambda b, j: (0, 0)),
                pl.BlockSpec((D_INNER, LANES), lambda b, j: (0, 0)),
                pl.BlockSpec((1, D_INNER), lambda b, j: (0, 0))]
    args = [xt, bc, bc, pa, pa, pm, dtb, alog, dsk, norm_g]
    state_spec = pl.BlockSpec((None, 2, NG_C, HPG_C * HD, D_STATE), lambda b, j: (b, 0, 0, 0, 0))
    if init is not None:
        in_specs.append(state_spec)
        args.append(init)
    out_specs = [pl.BlockSpec((CH, D_INNER), lambda b, j: (chunk_fwd(b, j), 0))]
    out_shape = [jax.ShapeDtypeStruct((t, D_INNER), BF16)]
    if out_state:
        out_specs.append(state_spec)
        out_shape.append(jax.ShapeDtypeStruct((bsz, 2, NG_C, HPG_C * HD, D_STATE), F32))
    return pl.pallas_call(
        functools.partial(_ssd_kernel, nc=nc, has_init=init is not None, out_state=out_state),
        grid=(bsz, 2 * nc),
        in_specs=in_specs,
        out_specs=out_specs,
        out_shape=out_shape,
        scratch_shapes=[pltpu.VMEM((NG_C, HPG_C * HD, D_STATE), F32),
                        pltpu.VMEM((nc, D_INNER, CH), F32),
                        pltpu.VMEM((D_INNER, CH), F32),
                        pltpu.VMEM((HPG_C * HD, CH), BF16),
                        pltpu.VMEM((2, CH, LANES), F32), pltpu.VMEM((2, 5, H_C, LANES), F32)],
        compiler_params=_cp("parallel", "arbitrary"),
        name="ssd_scan",
    )(*args)


def _rope_tables(s_len):
    nf = HD // 4
    inv = ROPE_THETA ** (-jnp.arange(nf, dtype=F32) / nf)
    tpos = jnp.arange(s_len)
    ang_r = (tpos // GRID_W).astype(F32)[:, None] * inv
    ang_c = (tpos % GRID_W).astype(F32)[:, None] * inv
    cos = jnp.concatenate([jnp.cos(ang_r)] * 2 + [jnp.cos(ang_c)] * 2, axis=1)
    sin = jnp.concatenate([-jnp.sin(ang_r), jnp.sin(ang_r), -jnp.sin(ang_c), jnp.sin(ang_c)], axis=1)
    return jnp.tile(cos, (1, 2)), jnp.tile(sin, (1, 2))


def _block_avg(n):
    idx = np.arange(n) // HD
    return jnp.asarray((idx[:, None] == idx[None, :]).astype(np.float32) / HD, dtype=BF16)


def _pad_cols(w, n):
    return jnp.pad(w, ((0, 0), (0, n - w.shape[1])))


def _row_prompt(i, tm):
    return 0


def _make_row_sample(s_len):
    def row(i, tm):
        return 1 + (i * tm) // s_len
    return row


def kernel(x_prompt, x_sample, cache_k_l0, cache_v_l0, state_mlstm_c_l0, state_mlstm_n_l0, state_mlstm_m_l0, state_ssd_l1, c, c_ctx, ada_w_l0, ada_b_l0, mix_w_in_l0, mlstm_gate_b_l0, mlstm_norm_g_l0, q_norm_g_l0, k_norm_g_l0, mix_w_out_l0, ln1_g_l0, ln1_b_l0, ln2_g_l0, ln2_b_l0, ffn_w1_l0, ffn_w3_l0, ffn_w2_l0, ada_w_l1, ada_b_l1, ssd_w_in_l1, ssd_conv_w_l1, ssd_conv_b_l1, ssd_dt_bias_l1, ssd_a_log_l1, ssd_d_l1, ssd_norm_g_l1, ssd_w_out_l1, ln1_g_l1, ln1_b_l1, ln2_g_l1, ln2_b_l1, router_w_l1, router_b_l1, moe_w1_l1, moe_w3_l1, moe_w2_l1):
    bp, sp, _ = x_prompt.shape
    bs, ss, _ = x_sample.shape
    past = cache_k_l0.shape[1]
    groups = [
        dict(x=x_prompt.reshape(bp * sp, D), b=bp, s=sp, row=_row_prompt, prompt=True),
        dict(x=x_sample.reshape(bs * ss, D), b=bs, s=ss, row=_make_row_sample(ss), prompt=False),
    ]

    cvec = jnp.zeros((16, D), F32).at[0].set(c_ctx).at[1:1 + bs].set(c)
    mods0 = _ada(cvec, ada_w_l0, ada_b_l0).reshape(16, 1, 6 * D)
    mods1 = _ada(cvec, ada_w_l1, ada_b_l1).reshape(16, 1, 6 * D)

    w0_main = jnp.concatenate([mix_w_in_l0[:, :2048], mix_w_in_l0[:, 2080:]], axis=1).astype(BF16)
    w0_aux = _hilo(_pad_cols(mix_w_in_l0[:, 2048:2080], LANES))
    gbias = _pad_cols(mlstm_gate_b_l0.reshape(1, 4 * H_A), LANES)
    avg128 = _block_avg(LANES)
    q_g = jnp.tile(q_norm_g_l0, W_A // HD).reshape(1, W_A)
    k_g = jnp.tile(k_norm_g_l0, KV_B).reshape(1, LANES)
    w_out_h = mix_w_out_l0[:W_A].astype(BF16)
    w_out_o = mix_w_out_l0[W_A:].astype(BF16)
    ffn_w1 = ffn_w1_l0.astype(BF16)
    ffn_w3 = ffn_w3_l0.astype(BF16)
    ffn_w2 = ffn_w2_l0.astype(BF16)
    tables = _rope_tables(ss)
    kc = cache_k_l0.transpose(2, 0, 1, 3).reshape(KV_B, bs * past, HD).astype(BF16)
    vc = cache_v_l0.transpose(2, 0, 1, 3).reshape(KV_B, bs * past, HD)
    vc = jnp.concatenate([vc, jnp.ones_like(vc)], axis=-1).astype(BF16)
    c0 = jnp.concatenate([jnp.swapaxes(state_mlstm_c_l0, -1, -2),
                          jnp.broadcast_to(state_mlstm_n_l0[..., None, :], state_mlstm_c_l0.shape)], axis=-2)
    m0 = jnp.broadcast_to(state_mlstm_m_l0[..., None], state_mlstm_m_l0.shape + (LANES,))

    n_main1 = 2 * D_INNER + 2 * NG_C * D_STATE
    w1_main = ssd_w_in_l1[:, :n_main1].astype(BF16)
    w1_aux = _hilo(_pad_cols(ssd_w_in_l1[:, n_main1:], LANES))
    dtb = _pad_cols(ssd_dt_bias_l1.reshape(1, 2 * H_C), LANES)
    alog = _pad_cols(ssd_a_log_l1.reshape(1, 2 * H_C), LANES)
    dsk = jnp.broadcast_to(jnp.repeat(ssd_d_l1, HD)[:, None], (D_INNER, LANES))
    ssd_ng = ssd_norm_g_l1.reshape(1, D_INNER)
    w_out1 = ssd_w_out_l1.astype(BF16)
    rw_hilo = _hilo(_pad_cols(router_w_l1, LANES))
    rb = jnp.full((1, LANES), NEG, F32).at[0, :N_EXPERTS].set(router_b_l1)
    moe_w1 = moe_w1_l1.astype(BF16)
    moe_w3 = moe_w3_l1.astype(BF16)
    moe_w2 = moe_w2_l1.astype(BF16)
    s0 = state_ssd_l1.reshape(bs, 2, NG_C, HPG_C * HD, D_STATE)

    outs = {}
    for grp in groups:
        x, b, s, row, prompt = grp["x"], grp["b"], grp["s"], grp["row"], grp["prompt"]
        pm, pa = _mm_in(x, mods0, row, w0_main, w0_aux, tm=512, n_chunks=2)
        if prompt:
            hn, cst, mst = _mlstm(pm, pa, gbias, mlstm_norm_g_l0.reshape(1, W_A), b, s, out_state=True)
            qn, kh, vh, kn = _qkprep(pm, q_g, k_g, avg128, s, emit_kn=True)
            o = _attn(qn, kh, vh, b, s)
            outs["k"] = kn.reshape(b, s, KV_B, HD)
            outs["v"] = pm[:, 2688:2816].astype(F32).reshape(b, s, KV_B, HD)
            outs["c"] = jnp.swapaxes(cst[..., :HD, :], -1, -2)
            outs["n"] = cst[..., HD, :]
            outs["m"] = mst[..., 0]
        else:
            hn = _mlstm(pm, pa, gbias, mlstm_norm_g_l0.reshape(1, W_A), b, s, init=(c0, m0))[0]
            qn, kh, vh = _qkprep(pm, q_g, k_g, avg128, s, tables=tables)
            o = _attn(qn, kh, vh, b, s, ctx=(kc, vc))
        x1, u2 = _out_proj([hn, o], [w_out_h, w_out_o], x, mods0, row, ln1_g_l0, ln1_b_l0)
        x2 = _ffn(u2, ffn_w1, ffn_w3, ffn_w2, x1, mods0, row, ln2_g_l0, ln2_b_l0, tf=D_FF // 2)
        pm, pa = _mm_in(x2, mods1, row, w1_main, w1_aux, tm=512, n_chunks=4)
        xt = _conv(pm, ssd_conv_w_l1[:, :D_INNER], ssd_conv_b_l1[:D_INNER], b, s, D_INNER, True)
        bc = _conv(pm, ssd_conv_w_l1[:, D_INNER:], ssd_conv_b_l1[D_INNER:], b, s, 2 * D_INNER, False)
        if prompt:
            yn, sst = _ssd(xt, bc, pa, pm, dtb, alog, dsk, ssd_ng, b, s, out_state=True)
            outs["s"] = sst.reshape(b, 2, H_C, HD, D_STATE)
        else:
            yn = _ssd(xt, bc, pa, pm, dtb, alog, dsk, ssd_ng, b, s, init=s0)[0]
        x1, u2, gate = _out_proj([yn], [w_out1], x2, mods1, row, ln1_g_l1, ln1_b_l1, router=(rw_hilo, rb))
        x3 = _moe(u2, moe_w1, moe_w3, moe_w2, gate, x1, mods1, row, ln2_g_l1, ln2_b_l1,
                  tm=min(1024, b * s if prompt else s))
        outs["y_p" if prompt else "y_s"] = x3.reshape(b, s, D)

    return (outs["y_p"], outs["y_s"], outs["k"], outs["v"], outs["c"], outs["n"], outs["m"], outs["s"])
```
